```python
import math
import jax, jax.numpy as jnp
from jax import lax
import numpy as np

D_MODEL = 2048
BATCH = 1
SEQ = 8192
DEPTH = 4
DEC_BATCH = 16
DEC_SEQ = 64
PAST_LEN = 2048

CHUNK = 64
EPS = 1e-6
GMLP_CHUNK = 128
N_GROUPS_A = 4
GROUP_A = D_MODEL // 16
WIDTH_A = N_GROUPS_A * GROUP_A
N_HEADS_B = 6
DV_B = D_MODEL // 16
DK_B = DV_B // 2
GATE_RANK_B = 16
GATE_TEMP_B = 16.0
GLA_CHUNK = CHUNK
WIDTH_B = N_HEADS_B * DV_B
N_HEADS_C = 6
N_KV_C = 2
HD_C = D_MODEL // 16
N_IDX_HEADS = 8
D_IDX = 64
TOPK_MAX = 256
Q_BLOCK = 128
WIDTH_C = N_HEADS_C * HD_C
MIX_WIDTH = WIDTH_A + WIDTH_B + WIDTH_C
N_BUCKETS = 32
MAX_DISTANCE = 128
D_FF = 4 * D_MODEL

IN_SPLITS = (WIDTH_A, WIDTH_A,
             N_HEADS_B * DK_B, N_HEADS_B * DK_B, WIDTH_B,
             GATE_RANK_B, WIDTH_B,
             WIDTH_C, N_KV_C * HD_C, N_KV_C * HD_C,
             N_IDX_HEADS * D_IDX, D_IDX, N_IDX_HEADS)
IN_WIDTH = (2 * WIDTH_A + 2 * N_HEADS_B * DK_B + 2 * WIDTH_B + GATE_RANK_B
            + WIDTH_C + 2 * N_KV_C * HD_C + N_IDX_HEADS * D_IDX + D_IDX + N_IDX_HEADS)

kernel_name = 'hybrid_gmlp_gla_dsa_streaming_step'


def rmsnorm(x, g):
    xf = x.astype(jnp.float32)
    y = xf * lax.rsqrt(jnp.mean(xf * xf, axis=-1, keepdims=True) + EPS)
    return (y * g.astype(jnp.float32)).astype(x.dtype)


def split_cols(h):
    parts = []
    off = 0
    for n in IN_SPLITS:
        parts.append(h[..., off:off + n])
        off += n
    return parts


def gmlp_mix(u, v, g_v, w_s, b_s):
    B, T, _ = u.shape
    n = -(-T // GMLP_CHUNK)
    vn = rmsnorm(v.reshape(B, T, N_GROUPS_A, GROUP_A), g_v.reshape(N_GROUPS_A, GROUP_A))
    vp = jnp.pad(vn, ((0, 0), (0, n * GMLP_CHUNK - T), (0, 0), (0, 0)))
    vp = vp.reshape(B, n, GMLP_CHUNK, N_GROUPS_A, GROUP_A)
    tri = jnp.tril(jnp.ones((GMLP_CHUNK, GMLP_CHUNK), dtype=bool))
    wm = jnp.where(tri[None], w_s, jnp.zeros_like(w_s))
    z = jnp.einsum('gij,bnjgc->bnigc', wm, vp) + b_s.T[None, None, :, :, None]
    z = z.reshape(B, n * GMLP_CHUNK, WIDTH_A)[:, :T]
    return u * z, vn.reshape(B, T, WIDTH_A)


def gla_scan(q, k, v, log_a, s0):
    B, T, H, _ = q.shape
    C = min(GLA_CHUNK, T)
    n = T // C

    def to_blocks(a):
        return a.reshape(B, n, C, *a.shape[2:]).swapaxes(0, 1)

    tri = jnp.tril(jnp.ones((C, C), dtype=bool))

    def step(S, inp):
        qc, kc, vc, gc = inp
        b = jnp.cumsum(gc, axis=1)
        o_inter = jnp.einsum('bthd,bhdv->bthv', qc * jnp.exp(b), S)
        diff = b[:, :, None] - b[:, None, :]
        decay = jnp.exp(jnp.where(tri[None, :, :, None, None], diff, -jnp.inf))
        att = jnp.einsum('bthd,bshd,btshd->btsh', qc, kc, decay)
        o_intra = jnp.einsum('btsh,bshv->bthv', att, vc)
        b_last = b[:, -1]
        S = jnp.exp(b_last)[..., None] * S + jnp.einsum(
            'bshd,bshv->bhdv', kc * jnp.exp(b_last[:, None] - b), vc)
        return S, o_inter + o_intra

    S, o = lax.scan(step, s0, (to_blocks(q), to_blocks(k), to_blocks(v), to_blocks(log_a)))
    return o.swapaxes(0, 1).reshape(B, T, H, -1), S


def t5_bucket(rel):
    half = N_BUCKETS // 2
    max_exact = half // 2
    n = jnp.abs(rel)
    nf = jnp.maximum(n, 1).astype(jnp.float32)
    large = max_exact + (jnp.log(nf / max_exact) / math.log(MAX_DISTANCE / max_exact)
                         * (half - max_exact)).astype(jnp.int32)
    large = jnp.minimum(large, half - 1)
    return jnp.where(rel > 0, half, 0) + jnp.where(n < max_exact, n, large)


def dsa_attend(q, qi, wi, q_pos, k_all, v_all, ki_all, rel_bias, topk):
    B, Tq, H, Dh = q.shape
    L = k_all.shape[1]
    f32 = jnp.float32
    k_pos = jnp.arange(L, dtype=jnp.int32)
    dots = jnp.einsum('bthd,bsd->bths', qi, ki_all).astype(f32) * D_IDX ** -0.5
    score = jnp.einsum('bth,bths->bts', wi.astype(f32), jax.nn.relu(dots))
    admissible = (k_pos[None, :] // CHUNK) <= (q_pos[:, None] // CHUNK)
    score = jnp.where(admissible[None], score, -jnp.inf)
    top_val, top_idx = lax.top_k(score, topk)
    valid = jnp.isfinite(top_val)
    take = jax.vmap(lambda a, i: a[i])
    kg = take(k_all, top_idx)
    vg = take(v_all, top_idx)
    qg = q.reshape(B, Tq, N_KV_C, H // N_KV_C, Dh)
    logits = jnp.einsum('btgrd,btkgd->btgrk', qg, kg).astype(f32) * Dh ** -0.5
    bias = rel_bias[t5_bucket(top_idx - q_pos[None, :, None])]
    bias = bias.reshape(B, Tq, topk, N_KV_C, H // N_KV_C).transpose(0, 1, 3, 4, 2).astype(f32)
    logits = jnp.where(valid[:, :, None, None, :], logits + bias, -jnp.inf)
    p = jax.nn.softmax(logits, axis=-1)
    out = jnp.einsum('btgrk,btkgd->btgrd', p.astype(vg.dtype), vg)
    return out.reshape(B, Tq, H * Dh)


def token_mix(x, g_mix, w_in, g_av, w_s, b_s, w_gate, b_gate, g_bo, w_o, rel_bias,
              past_k, past_v, past_ki, s0):
    B, T, _ = x.shape
    P = 0 if past_k is None else past_k.shape[1]
    f32 = jnp.float32
    h = rmsnorm(x, g_mix) @ w_in
    ua, va, qb, kb, vb, lrb, og, qc, kc, vc, qic, kic, wic = split_cols(h)

    ya, va_rows = gmlp_mix(jax.nn.gelu(ua), jax.nn.gelu(va), g_av, w_s, b_s)

    qb = qb.reshape(B, T, N_HEADS_B, DK_B).astype(f32) * DK_B ** -0.5
    kb = kb.reshape(B, T, N_HEADS_B, DK_B).astype(f32)
    vb = vb.reshape(B, T, N_HEADS_B, DV_B).astype(f32)
    log_a = jax.nn.log_sigmoid((lrb @ w_gate + b_gate).astype(f32)).reshape(
        B, T, N_HEADS_B, DK_B) / GATE_TEMP_B
    S0 = jnp.zeros((B, N_HEADS_B, DK_B, DV_B), f32) if s0 is None else s0.astype(f32)
    ob, S_new = gla_scan(qb, kb, vb, log_a, S0)
    yb = rmsnorm(ob, g_bo.reshape(N_HEADS_B, DV_B)).reshape(B, T, WIDTH_B).astype(x.dtype) * jax.nn.silu(og)

    qc = qc.reshape(B, T, N_HEADS_C, HD_C)
    kc = kc.reshape(B, T, N_KV_C, HD_C)
    vc = vc.reshape(B, T, N_KV_C, HD_C)
    qic = qic.reshape(B, T, N_IDX_HEADS, D_IDX)
    wic = wic * N_IDX_HEADS ** -0.5
    if past_k is None:
        k_all, v_all, ki_all = kc, vc, kic
    else:
        k_all = jnp.concatenate([past_k.astype(kc.dtype), kc], axis=1)
        v_all = jnp.concatenate([past_v.astype(vc.dtype), vc], axis=1)
        ki_all = jnp.concatenate([past_ki.astype(kic.dtype), kic], axis=1)
    topk = min(TOPK_MAX, (P + T) // 4)
    q_pos = P + jnp.arange(T, dtype=jnp.int32)
    qblk = min(Q_BLOCK, T)
    nb = T // qblk

    def blk(a):
        return a.reshape(B, nb, qblk, *a.shape[2:]).swapaxes(0, 1)

    yc = lax.map(lambda a: dsa_attend(a[0], a[1], a[2], a[3], k_all, v_all, ki_all, rel_bias, topk),
                 (blk(qc), blk(qic), blk(wic), q_pos.reshape(nb, qblk)))
    yc = yc.swapaxes(0, 1).reshape(B, T, WIDTH_C)

    y = jnp.concatenate([ya, yb, yc.astype(ya.dtype)], axis=-1) @ w_o
    return y, kc, vc, kic, S_new.astype(x.dtype), va_rows


def sq_relu_ffn(x, g, w_up, w_down):
    a = jax.nn.relu(rmsnorm(x, g) @ w_up)
    return (a * a) @ w_down


def setup_inputs(seed: int = 0) -> dict:
    key = jax.random.key(seed)
    ks = jax.random.split(key, 24)
    f32 = jnp.float32

    def nrm(k, shape, s=1.0):
        return jax.random.normal(k, shape, f32) * s

    return {
        'x_prompt': nrm(ks[0], (BATCH, SEQ, D_MODEL)),
        'x_sample': nrm(ks[1], (DEC_BATCH, DEC_SEQ, D_MODEL)),
        'cache_c_k': nrm(ks[2], (DEPTH, DEC_BATCH, PAST_LEN, N_KV_C, HD_C)),
        'cache_c_v': nrm(ks[3], (DEPTH, DEC_BATCH, PAST_LEN, N_KV_C, HD_C)),
        'cache_c_kidx': nrm(ks[4], (DEPTH, DEC_BATCH, PAST_LEN, D_IDX)),
        'state_b_s': nrm(ks[5], (DEPTH, DEC_BATCH, N_HEADS_B, DK_B, DV_B)),
        'norm_mix': 1.0 + nrm(ks[6], (DEPTH, D_MODEL), 0.02),
        'w_in': nrm(ks[7], (DEPTH, D_MODEL, IN_WIDTH), D_MODEL ** -0.5),
        'norm_a_v': 1.0 + nrm(ks[8], (DEPTH, WIDTH_A), 0.02),
        'w_s_a': nrm(ks[9], (DEPTH, N_GROUPS_A, GMLP_CHUNK, GMLP_CHUNK), GMLP_CHUNK ** -0.5),
        'b_s_a': 1.0 + nrm(ks[10], (DEPTH, N_GROUPS_A, GMLP_CHUNK), 0.02),
        'w_gate_b': nrm(ks[11], (DEPTH, GATE_RANK_B, N_HEADS_B * DK_B), GATE_RANK_B ** -0.5),
        'b_gate_b': nrm(ks[12], (DEPTH, N_HEADS_B * DK_B), 0.1),
        'norm_b_o': 1.0 + nrm(ks[13], (DEPTH, WIDTH_B), 0.02),
        'rel_bias': nrm(ks[14], (N_BUCKETS, N_HEADS_C), 0.5),
        'w_o': nrm(ks[15], (DEPTH, MIX_WIDTH, D_MODEL), MIX_WIDTH ** -0.5),
        'norm_ffn': 1.0 + nrm(ks[16], (DEPTH, D_MODEL), 0.02),
        'w_up': nrm(ks[17], (DEPTH, D_MODEL, D_FF), D_MODEL ** -0.5),
        'w_down': nrm(ks[18], (DEPTH, D_FF, D_MODEL), D_FF ** -0.5),
        'norm_final': 1.0 + nrm(ks[19], (D_MODEL,), 0.02),
    }


def reference(x_prompt, x_sample, cache_c_k, cache_c_v, cache_c_kidx, state_b_s,
              norm_mix, w_in, norm_a_v, w_s_a, b_s_a, w_gate_b, b_gate_b, norm_b_o,
              rel_bias, w_o, norm_ffn, w_up, w_down, norm_final):
    xp = x_prompt
    xs = x_sample
    pk_l, pv_l, pki_l, ps_l = [], [], [], []
    sk_l, sv_l, ski_l, ss_l, sav_l = [], [], [], [], []
    for l in range(DEPTH):
        lw = (norm_mix[l], w_in[l], norm_a_v[l], w_s_a[l], b_s_a[l], w_gate_b[l], b_gate_b[l],
              norm_b_o[l], w_o[l], rel_bias)
        yp, pk, pv, pki, ps, _ = token_mix(xp, *lw, None, None, None, None)
        xp = xp + yp
        xp = xp + sq_relu_ffn(xp, norm_ffn[l], w_up[l], w_down[l])
        ys, sk, sv, ski, ss, sav = token_mix(xs, *lw, cache_c_k[l], cache_c_v[l], cache_c_kidx[l], state_b_s[l])
        xs = xs + ys
        xs = xs + sq_relu_ffn(xs, norm_ffn[l], w_up[l], w_down[l])
        pk_l.append(pk); pv_l.append(pv); pki_l.append(pki); ps_l.append(ps)
        sk_l.append(sk); sv_l.append(sv); ski_l.append(ski); ss_l.append(ss); sav_l.append(sav)
    y_prompt = rmsnorm(xp, norm_final)
    y_sample = rmsnorm(xs, norm_final)
    new_c_k_prompt = jnp.stack(pk_l)
    new_c_v_prompt = jnp.stack(pv_l)
    new_c_kidx_prompt = jnp.stack(pki_l)
    new_b_s_prompt = jnp.stack(ps_l)
    new_c_k_sample = jnp.stack(sk_l)
    new_c_v_sample = jnp.stack(sv_l)
    new_c_kidx_sample = jnp.stack(ski_l)
    new_b_s_sample = jnp.stack(ss_l)
    new_a_v_sample = jnp.stack(sav_l)
    return (y_prompt, y_sample, new_c_k_prompt, new_c_v_prompt, new_c_kidx_prompt, new_b_s_prompt,
            new_c_k_sample, new_c_v_sample, new_c_kidx_sample, new_b_s_sample, new_a_v_sample)
```

```python
import functools

import jax
import jax.numpy as jnp
from jax import lax
from jax.experimental import pallas as pl
from jax.experimental.pallas import tpu as pltpu

BF16 = jnp.bfloat16
F32 = jnp.float32
I32 = jnp.int32

D_MODEL = 2048
EPS = 1e-6
CHUNK = 64
GROUP_A = 128
N_GROUPS_A = 4
WIDTH_A = 512
N_HEADS_B = 6
DK_B = 64
DV_B = 128
GATE_RANK_B = 16
GATE_TEMP_B = 16.0
WIDTH_B = 768
N_HEADS_C = 6
N_KV_C = 2
HD_C = 128
N_IDX_HEADS = 8
D_IDX = 64
TOPK_MAX = 256
WIDTH_C = 768
N_BUCKETS = 32
MAX_DISTANCE = 128
D_FF = 4 * D_MODEL
QK_B = N_HEADS_B * DK_B
KV_C = N_KV_C * HD_C
QI_C = N_IDX_HEADS * D_IDX

LANES = 128
VMEM_LIMIT = 56 * 1024 * 1024

C_QB = 0
C_KB = 384
C_VB = 768
C_OG = 1536
C_QC = 2304
C_KC = 3072
C_VC = 3328
C_QIC = 3584
C_UA = 4096
C_VA = 4608
C_MISC = 5120
MISC_LRB = D_IDX
MISC_WIC = D_IDX + GATE_RANK_B
H_WIDTH = 5376

INT_MIN = -2147483648
NEAR_SPAN = 128
FAR_KW = 512


def _pick(n, cands):
    for c in cands:
        if n % c == 0:
            return c
    raise ValueError(f"no tile for {n}")


def _cparams(sem):
    return pltpu.CompilerParams(dimension_semantics=sem, vmem_limit_bytes=VMEM_LIMIT)


def _rms(x):
    return x * lax.rsqrt(jnp.mean(x * x, axis=-1, keepdims=True) + EPS)


def _dot(a, b):
    return jnp.dot(a, b, preferred_element_type=F32)


def _dot_nt(a, b):
    return lax.dot_general(a, b, (((1,), (1,)), ((), ())), preferred_element_type=F32)


def _dot_tn(a, b):
    return lax.dot_general(a, b, (((0,), (0,)), ((), ())), preferred_element_type=F32)


def _inproj_kernel(x_ref, g_ref, w_ref, o_ref, xn_ref):
    @pl.when(pl.program_id(1) == 0)
    def _():
        xn_ref[...] = (_rms(x_ref[...]) * g_ref[...]).astype(BF16)

    o_ref[...] = _dot(xn_ref[...], w_ref[...])


def _inproj(x, g, w):
    m = x.shape[0]
    tm = _pick(m, (1024, 512, 256, 128))
    tn = 768
    return pl.pallas_call(
        _inproj_kernel,
        grid=(m // tm, H_WIDTH // tn),
        in_specs=[pl.BlockSpec((tm, D_MODEL), lambda i, j: (i, 0)),
                  pl.BlockSpec((1, D_MODEL), lambda i, j: (0, 0)),
                  pl.BlockSpec((D_MODEL, tn), lambda i, j: (0, j))],
        out_specs=pl.BlockSpec((tm, tn), lambda i, j: (i, j)),
        out_shape=jax.ShapeDtypeStruct((m, H_WIDTH), F32),
        scratch_shapes=[pltpu.VMEM((tm, D_MODEL), BF16)],
        compiler_params=_cparams(("arbitrary", "arbitrary")),
        name="inproj",
    )(x, g, w)


def _gmlp_kernel(h_ref, gv_ref, w_ref, b_ref, ya_ref, vn_ref, *, n_prompt_blocks):
    i = pl.program_id(0)
    hv = h_ref[...]
    u = jax.nn.gelu(hv[:, :WIDTH_A])
    v = jax.nn.gelu(hv[:, WIDTH_A:])
    r = lax.broadcasted_iota(I32, (GROUP_A, GROUP_A), 0)
    c = lax.broadcasted_iota(I32, (GROUP_A, GROUP_A), 1)
    samp = (i >= n_prompt_blocks).astype(I32)
    keep = (c <= r) & (((r >> 6) * samp) == ((c >> 6) * samp))
    for g in range(N_GROUPS_A):
        sl = slice(g * GROUP_A, (g + 1) * GROUP_A)
        vn = _rms(v[:, sl]) * gv_ref[:, sl]
        vn_ref[:, sl] = vn
        wm = jnp.where(keep, w_ref[0, g], 0.0).astype(BF16)
        z = _dot(wm, vn.astype(BF16)) + b_ref[0][:, g:g + 1]
        ya_ref[:, sl] = (u[:, sl] * z).astype(BF16)


def _gmlp(h, gv, w2, b2, n_prompt_rows):
    m = h.shape[0]
    npb = n_prompt_rows // GROUP_A
    nsb = (m - n_prompt_rows) // GROUP_A
    return pl.pallas_call(
        functools.partial(_gmlp_kernel, n_prompt_blocks=npb),
        grid=(npb + nsb,),
        in_specs=[pl.BlockSpec((GROUP_A, 2 * WIDTH_A), lambda i: (i, C_UA // (2 * WIDTH_A))),
                  pl.BlockSpec((1, WIDTH_A), lambda i: (0, 0)),
                  pl.BlockSpec((1, N_GROUPS_A, GROUP_A, GROUP_A),
                               lambda i: (jnp.where(i >= npb, 1, 0), 0, 0, 0)),
                  pl.BlockSpec((1, GROUP_A, N_GROUPS_A),
                               lambda i: (jnp.where(i >= npb, 1, 0), 0, 0))],
        out_specs=[pl.BlockSpec((GROUP_A, WIDTH_A), lambda i: (i, 0)),
                   pl.BlockSpec((GROUP_A, WIDTH_A), lambda i: (jnp.maximum(i - npb, 0), 0))],
        out_shape=[jax.ShapeDtypeStruct((m, WIDTH_A), BF16),
                   jax.ShapeDtypeStruct((nsb * GROUP_A, WIDTH_A), F32)],
        compiler_params=_cparams(("arbitrary",)),
        name="gmlp",
    )(h, gv, w2, b2)


def _gla_kernel(q_ref, k_ref, v_ref, og_ref, misc_ref, wg_ref, bg_ref, gbo_ref, s0_ref,
                yb_ref, sout_ref, st_ref, *, n_prompt_chunks):
    c = pl.program_id(0)

    @pl.when((c == 0) | (c >= n_prompt_chunks))
    def _():
        for h in range(N_HEADS_B):
            st_ref[h] = s0_ref[0, h].T

    lrb = misc_ref[:, MISC_LRB:MISC_LRB + GATE_RANK_B].astype(BF16)
    x = _dot(lrb, wg_ref[...]) + bg_ref[...]
    g = (jnp.minimum(x, 0.0) - jnp.log1p(jnp.exp(-jnp.abs(x)))) * (1.0 / GATE_TEMP_B)
    rr = lax.broadcasted_iota(I32, (CHUNK, CHUNK), 0)
    cc = lax.broadcasted_iota(I32, (CHUNK, CHUNK), 1)
    tril = cc <= rr
    tri = jnp.where(tril, 1.0, 0.0).astype(BF16)
    g1 = g.astype(BF16)
    r1 = g - g1.astype(F32)
    g2 = r1.astype(BF16)
    g3 = (r1 - g2.astype(F32)).astype(BF16)
    b = _dot(tri, g1) + _dot(tri, g2) + _dot(tri, g3)
    b_last = b[CHUNK - 1:CHUNK, :]
    b_mid = b[CHUNK // 2 - 1:CHUNK // 2, :]
    qs = q_ref[...] * (DK_B ** -0.5)
    kk = k_ref[...]
    q_inter = (qs * jnp.exp(b)).astype(BF16)
    q_intra = (qs * jnp.exp(b - b_mid)).astype(BF16)
    k_intra = (kk * jnp.exp(b_mid - b)).astype(BF16)
    k_state = (kk * jnp.exp(b_last - b)).astype(BF16)
    dec = jnp.exp(b_last)
    for h in range(N_HEADS_B):
        sk = slice(h * DK_B, (h + 1) * DK_B)
        sv = slice(h * DV_B, (h + 1) * DV_B)
        st = st_ref[h]
        vh = v_ref[:, sv].astype(BF16)
        o = _dot_nt(q_inter[:, sk], st.astype(BF16))
        att = jnp.where(tril, _dot_nt(q_intra[:, sk], k_intra[:, sk]), 0.0)
        o = o + _dot(att.astype(BF16), vh)
        st_new = st * dec[:, sk] + _dot_tn(vh, k_state[:, sk])
        st_ref[h] = st_new
        sout_ref[0, h] = st_new.T
        on = _rms(o) * gbo_ref[:, sv]
        yb_ref[:, sv] = (on * jax.nn.silu(og_ref[:, sv])).astype(BF16)


def _gla(h, wg, bg, gbo, s_all, n_prompt_rows):
    m = h.shape[0]
    npc = n_prompt_rows // CHUNK
    nch = m // CHUNK
    nseq = s_all.shape[0]

    def seq(c):
        return jnp.where(c < npc, 0, c - npc + 1)

    return pl.pallas_call(
        functools.partial(_gla_kernel, n_prompt_chunks=npc),
        grid=(nch,),
        in_specs=[pl.BlockSpec((CHUNK, QK_B), lambda c: (c, C_QB // QK_B)),
                  pl.BlockSpec((CHUNK, QK_B), lambda c: (c, C_KB // QK_B)),
                  pl.BlockSpec((CHUNK, WIDTH_B), lambda c: (c, C_VB // WIDTH_B)),
                  pl.BlockSpec((CHUNK, WIDTH_B), lambda c: (c, C_OG // WIDTH_B)),
                  pl.BlockSpec((CHUNK, LANES), lambda c: (c, C_MISC // LANES)),
                  pl.BlockSpec((GATE_RANK_B, QK_B), lambda c: (0, 0)),
                  pl.BlockSpec((1, QK_B), lambda c: (0, 0)),
                  pl.BlockSpec((1, WIDTH_B), lambda c: (0, 0)),
                  pl.BlockSpec((1, N_HEADS_B, DK_B, DV_B), lambda c: (seq(c), 0, 0, 0))],
        out_specs=[pl.BlockSpec((CHUNK, WIDTH_B), lambda c: (c, 0)),
                   pl.BlockSpec((1, N_HEADS_B, DK_B, DV_B), lambda c: (seq(c), 0, 0, 0))],
        out_shape=[jax.ShapeDtypeStruct((m, WIDTH_B), BF16),
                   jax.ShapeDtypeStruct((nseq, N_HEADS_B, DK_B, DV_B), F32)],
        scratch_shapes=[pltpu.VMEM((N_HEADS_B, DV_B, DK_B), F32)],
        compiler_params=_cparams(("arbitrary",)),
        name="gla",
    )(h, h, h, h, h, wg, bg, gbo, s_all)


def _fold_lanes(x):
    n = x.shape[1]
    acc = x[:, :LANES]
    for j in range(1, n // LANES):
        acc = acc + x[:, j * LANES:(j + 1) * LANES]
    return acc


def _dsa_core(q, qi, wi, far_ki, far_k, far_v, nf, far_adm, near_ki, near_k, near_v, near_adm,
              bias_near_ref, bias_far_ref, out_ref, fkeys_ref, nkeys_ref, m_ref, l_ref, acc_ref,
              *, tq, kw, nw, topk):
    qi_b = (qi * (D_IDX ** -0.5)).astype(BF16)
    q_b = q.astype(BF16)
    kf = float(topk)

    def keys_of(ki_blk, adm):
        acc = None
        for h in range(N_IDX_HEADS):
            d = _dot_nt(qi_b[:, h * D_IDX:(h + 1) * D_IDX], ki_blk)
            t = wi[:, h:h + 1] * jnp.maximum(d, 0.0)
            acc = t if acc is None else acc + t
        bits = lax.bitcast_convert_type(acc, I32)
        key = bits ^ ((bits >> 31) & 0x7FFFFFFF)
        key = jnp.where(acc == 0.0, 0, key)
        if adm is not None:
            key = jnp.where(adm, key, INT_MIN)
        return key

    def fill(c, carry):
        fkeys_ref[c] = keys_of(far_ki(c), None if far_adm is None else far_adm(c))
        return carry

    lax.fori_loop(0, nf, fill, 0)
    nkeys_ref[...] = keys_of(near_ki, near_adm)

    def count(thr, strict):
        def cmp(x):
            hit = (x > thr) if strict else (x >= thr)
            return _fold_lanes(jnp.where(hit, 1.0, 0.0))

        part = lax.fori_loop(0, nf, lambda c, a: a + cmp(fkeys_ref[c]), jnp.zeros((tq, LANES), F32))
        part = part + cmp(nkeys_ref[...])
        return jnp.sum(part, axis=-1, keepdims=True)

    def search(t, cand):
        trial = cand + jnp.left_shift(jnp.int32(1), 31 - t)
        return jnp.where(count(trial, False) >= kf, trial, cand)

    thr = lax.fori_loop(0, 32, search, jnp.full((tq, 1), INT_MIN, I32))
    need = kf - count(thr, True)

    m_ref[...] = jnp.full(m_ref.shape, -1e30, F32)
    l_ref[...] = jnp.zeros(l_ref.shape, F32)
    acc_ref[...] = jnp.zeros(acc_ref.shape, F32)
    scale = HD_C ** -0.5

    def upper(n):
        a = lax.broadcasted_iota(I32, (n, n), 0)
        b = lax.broadcasted_iota(I32, (n, n), 1)
        return jnp.where(a < b, 1.0, 0.0).astype(BF16)

    def attend(keys, k_blk, v_blk, bias_of, run, ut):
        gt = keys > thr
        eq = (keys == thr) & (keys != INT_MIN)
        eqf = jnp.where(eq, 1.0, 0.0)
        before = _dot(eqf.astype(BF16), ut) + run
        sel = gt | (eq & (before < need))
        run = run + jnp.sum(eqf, axis=-1, keepdims=True)
        for hh in range(N_HEADS_C):
            g = hh // (N_HEADS_C // N_KV_C)
            s = _dot_nt(q_b[:, hh * HD_C:(hh + 1) * HD_C], k_blk[:, g * HD_C:(g + 1) * HD_C])
            s = jnp.where(sel, s * scale + bias_of(hh), -1e30)
            m_old = m_ref[hh]
            m_new = jnp.maximum(m_old, jnp.max(s, axis=-1, keepdims=True))
            p = jnp.where(sel, jnp.exp(s - m_new), 0.0)
            alpha = jnp.exp(m_old - m_new)
            l_ref[hh] = alpha * l_ref[hh] + jnp.sum(p, axis=-1, keepdims=True)
            sl = slice(hh * HD_C, (hh + 1) * HD_C)
            acc_ref[:, sl] = alpha * acc_ref[:, sl] + _dot(p.astype(BF16), v_blk[:, g * HD_C:(g + 1) * HD_C])
            m_ref[hh] = m_new
        return run

    ut_far = upper(kw)

    def far_step(c, run):
        return attend(fkeys_ref[c], far_k(c), far_v(c), lambda hh: bias_far_ref[hh], run, ut_far)

    run = lax.fori_loop(0, nf, far_step, jnp.zeros((tq, 1), F32))
    attend(nkeys_ref[...], near_k, near_v, lambda hh: bias_near_ref[hh], run, upper(nw))

    for hh in range(N_HEADS_C):
        sl = slice(hh * HD_C, (hh + 1) * HD_C)
        out_ref[:, sl] = (acc_ref[:, sl] / l_ref[hh]).astype(BF16)


def _dsa_prompt_kernel(q_ref, qi_ref, misc_ref, kv_ref, ki_ref, bias_near_ref, bias_far_ref,
                       out_ref, fkeys_ref, nkeys_ref, m_ref, l_ref, acc_ref, *, tq, topk):
    i = pl.program_id(0)
    kw = FAR_KW
    nw = 2 * tq
    far_limit = tq * (i - 1)
    nf = (i + 2) // (kw // tq)
    wi = misc_ref[:, MISC_WIC:MISC_WIC + N_IDX_HEADS] * (N_IDX_HEADS ** -0.5)

    def rows(c):
        return pl.ds(pl.multiple_of(c * kw, kw), kw)

    def far_adm(c):
        return (c * kw + lax.broadcasted_iota(I32, (tq, kw), 1)) < far_limit

    left = pl.ds(pl.multiple_of(jnp.maximum(i - 1, 0) * tq, tq), tq)
    right = pl.ds(pl.multiple_of(i * tq, tq), tq)
    near_ki = jnp.concatenate([ki_ref[left, :], ki_ref[right, :]], axis=0)
    near_kv = jnp.concatenate([kv_ref[left, :], kv_ref[right, :]], axis=0)
    col = lax.broadcasted_iota(I32, (tq, nw), 1)
    row = lax.broadcasted_iota(I32, (tq, nw), 0)
    first_col = jnp.where(i > 0, 0, tq)
    near_adm = (col >= first_col) & (((col - tq) >> 6) <= (row >> 6))

    _dsa_core(q_ref[...], qi_ref[...], wi,
              lambda c: ki_ref[rows(c), :], lambda c: kv_ref[rows(c), :KV_C],
              lambda c: kv_ref[rows(c), KV_C:], nf, far_adm,
              near_ki, near_kv[:, :KV_C], near_kv[:, KV_C:], near_adm,
              bias_near_ref, bias_far_ref, out_ref, fkeys_ref, nkeys_ref, m_ref, l_ref, acc_ref,
              tq=tq, kw=kw, nw=nw, topk=topk)


def _dsa_prompt(h, kvb, kib, bias_near, bias_far, t):
    tq = NEAR_SPAN
    topk = min(TOPK_MAX, t // 4)
    return pl.pallas_call(
        functools.partial(_dsa_prompt_kernel, tq=tq, topk=topk),
        grid=(t // tq,),
        in_specs=[pl.BlockSpec((tq, WIDTH_C), lambda i: (i, C_QC // WIDTH_C)),
                  pl.BlockSpec((tq, QI_C), lambda i: (i, C_QIC // QI_C)),
                  pl.BlockSpec((tq, LANES), lambda i: (i, C_MISC // LANES)),
                  pl.BlockSpec((t, 2 * KV_C), lambda i: (0, 0)),
                  pl.BlockSpec((t, D_IDX), lambda i: (0, 0)),
                  pl.BlockSpec((N_HEADS_C, tq, 2 * tq), lambda i: (0, 0, 0)),
                  pl.BlockSpec(memory_space=pltpu.SMEM)],
        out_specs=pl.BlockSpec((tq, WIDTH_C), lambda i: (i, 0)),
        out_shape=jax.ShapeDtypeStruct((t, WIDTH_C), BF16),
        scratch_shapes=[pltpu.VMEM((t // FAR_KW, tq, FAR_KW), I32),
                        pltpu.VMEM((tq, 2 * tq), I32),
                        pltpu.VMEM((N_HEADS_C, tq, 1), F32),
                        pltpu.VMEM((N_HEADS_C, tq, 1), F32),
                        pltpu.VMEM((tq, WIDTH_C), F32)],
        compiler_params=_cparams(("arbitrary",)),
        name="dsa_prompt",
    )(h, h, h, kvb, kib, bias_near, bias_far)


def _dsa_sample_kernel(q_ref, qi_ref, misc_ref, kvn_ref, pk_ref, pv_ref, pki_ref, bias_near_ref,
                       bias_far_ref, out_ref, fkeys_ref, nkeys_ref, m_ref, l_ref, acc_ref,
                       *, kw, nf, topk):
    tq = CHUNK
    far_len = kw * nf
    misc = misc_ref[...]
    wi = misc[:, MISC_WIC:MISC_WIC + N_IDX_HEADS] * (N_IDX_HEADS ** -0.5)
    tail = pl.ds(far_len, NEAR_SPAN)
    near_ki = jnp.concatenate([pki_ref[0, tail, :], misc[:, :D_IDX]], axis=0).astype(BF16)
    near_k = jnp.concatenate([pk_ref[0, tail, :], kvn_ref[:, :KV_C]], axis=0).astype(BF16)
    near_v = jnp.concatenate([pv_ref[0, tail, :], kvn_ref[:, KV_C:]], axis=0).astype(BF16)

    def rows(c):
        return pl.ds(pl.multiple_of(c * kw, kw), kw)

    _dsa_core(q_ref[...], qi_ref[...], wi,
              lambda c: pki_ref[0, rows(c), :].astype(BF16), lambda c: pk_ref[0, rows(c), :].astype(BF16),
              lambda c: pv_ref[0, rows(c), :].astype(BF16), nf, None,
              near_ki, near_k, near_v, None,
              bias_near_ref, bias_far_ref, out_ref, fkeys_ref, nkeys_ref, m_ref, l_ref, acc_ref,
              tq=tq, kw=kw, nw=NEAR_SPAN + CHUNK, topk=topk)


def _dsa_sample(h, pk, pv, pki, bias_near, bias_far, t):
    nb, p = pk.shape[0], pk.shape[1]
    tq = CHUNK
    far_len = p - NEAR_SPAN
    kw = _pick(far_len, (640, 512, 384, 256, 128))
    nf = far_len // kw
    nw = NEAR_SPAN + CHUNK
    topk = min(TOPK_MAX, (p + CHUNK) // 4)
    row0 = t // tq
    return pl.pallas_call(
        functools.partial(_dsa_sample_kernel, kw=kw, nf=nf, topk=topk),
        grid=(nb,),
        in_specs=[pl.BlockSpec((tq, WIDTH_C), lambda b: (row0 + b, C_QC // WIDTH_C)),
                  pl.BlockSpec((tq, QI_C), lambda b: (row0 + b, C_QIC // QI_C)),
                  pl.BlockSpec((tq, LANES), lambda b: (row0 + b, C_MISC // LANES)),
                  pl.BlockSpec((tq, 2 * KV_C), lambda b: (row0 + b, C_KC // (2 * KV_C))),
                  pl.BlockSpec((1, p, KV_C), lambda b: (b, 0, 0)),
                  pl.BlockSpec((1, p, KV_C), lambda b: (b, 0, 0)),
                  pl.BlockSpec((1, p, D_IDX), lambda b: (b, 0, 0)),
                  pl.BlockSpec((N_HEADS_C, tq, nw), lambda b: (0, 0, 0)),
                  pl.BlockSpec(memory_space=pltpu.SMEM)],
        out_specs=pl.BlockSpec((tq, WIDTH_C), lambda b: (b, 0)),
        out_shape=jax.ShapeDtypeStruct((nb * tq, WIDTH_C), BF16),
        scratch_shapes=[pltpu.VMEM((nf, tq, kw), I32),
                        pltpu.VMEM((tq, nw), I32),
                        pltpu.VMEM((N_HEADS_C, tq, 1), F32),
                        pltpu.VMEM((N_HEADS_C, tq, 1), F32),
                        pltpu.VMEM((tq, WIDTH_C), F32)],
        compiler_params=_cparams(("arbitrary",)),
        name="dsa_sample",
    )(h, h, h, h, pk, pv, pki, bias_near, bias_far)


def _outproj_kernel(x_ref, ya_ref, yb_ref, yc_ref, wa_ref, wb_ref, wc_ref, o_ref):
    o_ref[...] = (x_ref[...] + _dot(ya_ref[...], wa_ref[...]) + _dot(yb_ref[...], wb_ref[...])
                  + _dot(yc_ref[...], wc_ref[...]))


def _outproj(x, ya, yb, yc, wa, wb, wc):
    m = x.shape[0]
    tm = _pick(m, (1024, 512, 256, 128))
    tn = 512
    return pl.pallas_call(
        _outproj_kernel,
        grid=(m // tm, D_MODEL // tn),
        in_specs=[pl.BlockSpec((tm, tn), lambda i, j: (i, j)),
                  pl.BlockSpec((tm, WIDTH_A), lambda i, j: (i, 0)),
                  pl.BlockSpec((tm, WIDTH_B), lambda i, j: (i, 0)),
                  pl.BlockSpec((tm, WIDTH_C), lambda i, j: (i, 0)),
                  pl.BlockSpec((WIDTH_A, tn), lambda i, j: (0, j)),
                  pl.BlockSpec((WIDTH_B, tn), lambda i, j: (0, j)),
                  pl.BlockSpec((WIDTH_C, tn), lambda i, j: (0, j))],
        out_specs=pl.BlockSpec((tm, tn), lambda i, j: (i, j)),
        out_shape=jax.ShapeDtypeStruct((m, D_MODEL), F32),
        compiler_params=_cparams(("arbitrary", "arbitrary")),
        name="outproj",
    )(x, ya, yb, yc, wa, wb, wc)


def _ffn_kernel(x_ref, g_ref, wu_ref, wd_ref, o_ref, xn_ref, acc_ref):
    f = pl.program_id(1)

    @pl.when(f == 0)
    def _():
        xn_ref[...] = (_rms(x_ref[...]) * g_ref[...]).astype(BF16)
        acc_ref[...] = jnp.zeros(acc_ref.shape, F32)

    a = jnp.maximum(_dot(xn_ref[...], wu_ref[...]), 0.0)
    acc_ref[...] += _dot((a * a).astype(BF16), wd_ref[...])

    @pl.when(f == pl.num_programs(1) - 1)
    def _():
        o_ref[...] = x_ref[...] + acc_ref[...]


def _ffn(x, g, wu, wd):
    m = x.shape[0]
    tm = _pick(m, (512, 256, 128))
    tf = 512
    return pl.pallas_call(
        _ffn_kernel,
        grid=(m // tm, D_FF // tf),
        in_specs=[pl.BlockSpec((tm, D_MODEL), lambda i, f: (i, 0)),
                  pl.BlockSpec((1, D_MODEL), lambda i, f: (0, 0)),
                  pl.BlockSpec((D_MODEL, tf), lambda i, f: (0, f)),
                  pl.BlockSpec((tf, D_MODEL), lambda i, f: (f, 0))],
        out_specs=pl.BlockSpec((tm, D_MODEL), lambda i, f: (i, 0)),
        out_shape=jax.ShapeDtypeStruct((m, D_MODEL), F32),
        scratch_shapes=[pltpu.VMEM((tm, D_MODEL), BF16), pltpu.VMEM((tm, D_MODEL), F32)],
        compiler_params=_cparams(("arbitrary", "arbitrary")),
        name="ffn",
    )(x, g, wu, wd)


def _norm_kernel(x_ref, g_ref, o_ref):
    o_ref[...] = _rms(x_ref[...]) * g_ref[...]


def _final_norm(x, g):
    m = x.shape[0]
    tm = _pick(m, (1024, 512, 256, 128))
    return pl.pallas_call(
        _norm_kernel,
        grid=(m // tm,),
        in_specs=[pl.BlockSpec((tm, D_MODEL), lambda i: (i, 0)),
                  pl.BlockSpec((1, D_MODEL), lambda i: (0, 0))],
        out_specs=pl.BlockSpec((tm, D_MODEL), lambda i: (i, 0)),
        out_shape=jax.ShapeDtypeStruct((m, D_MODEL), F32),
        compiler_params=_cparams(("arbitrary",)),
        name="final_norm",
    )(x, g)


def _t5_bucket(rel):
    half = N_BUCKETS // 2
    max_exact = half // 2
    n = jnp.abs(rel)
    nf = jnp.maximum(n, 1).astype(F32)
    large = max_exact + (jnp.log(nf / max_exact) / jnp.log(MAX_DISTANCE / max_exact)
                         * (half - max_exact)).astype(I32)
    large = jnp.minimum(large, half - 1)
    return jnp.where(rel > 0, half, 0) + jnp.where(n < max_exact, n, large)


def _bias_tables(rel_bias):
    t = jnp.arange(NEAR_SPAN, dtype=I32)[:, None]
    j = jnp.arange(2 * NEAR_SPAN, dtype=I32)[None, :]
    near = jnp.transpose(rel_bias[_t5_bucket(j - NEAR_SPAN - t)], (2, 0, 1))
    far = rel_bias[_t5_bucket(jnp.int32(-(NEAR_SPAN + 1)))]
    return near.astype(F32), far.astype(F32)


def _permute_w_in(w):
    sizes = (WIDTH_A, WIDTH_A, QK_B, QK_B, WIDTH_B, GATE_RANK_B, WIDTH_B, WIDTH_C, KV_C, KV_C, QI_C,
             D_IDX, N_IDX_HEADS)
    parts, off = [], 0
    for n in sizes:
        parts.append(w[..., off:off + n])
        off += n
    ua, va, qb, kb, vb, lrb, og, qc, kc, vc, qic, kic, wic = parts
    pad = jnp.zeros(w.shape[:-1] + (H_WIDTH - (C_MISC + D_IDX + GATE_RANK_B + N_IDX_HEADS),), w.dtype)
    return jnp.concatenate([qb, kb, vb, og, qc, kc, vc, qic, ua, va, kic, lrb, wic, pad], axis=-1).astype(BF16)


def kernel(x_prompt, x_sample, cache_c_k, cache_c_v, cache_c_kidx, state_b_s, norm_mix, w_in, norm_a_v,
           w_s_a, b_s_a, w_gate_b, b_gate_b, norm_b_o, rel_bias, w_o, norm_ffn, w_up, w_down, norm_final):
    depth = w_in.shape[0]
    nbp, t, _ = x_prompt.shape
    nbs, ts, _ = x_sample.shape
    p = cache_c_k.shape[2]
    assert nbp == 1 and ts == CHUNK and t % FAR_KW == 0 and (nbs * ts) % GROUP_A == 0
    assert p % NEAR_SPAN == 0 and p >= 2 * NEAR_SPAN

    x = jnp.concatenate([x_prompt[0], x_sample.reshape(nbs * ts, D_MODEL)], axis=0)
    w_in_p = _permute_w_in(w_in)
    w_o_b = w_o.astype(BF16)
    w_up_b = w_up.astype(BF16)
    w_down_b = w_down.astype(BF16)
    w_gate = w_gate_b.astype(BF16)
    half = GROUP_A // 2
    w_s2 = jnp.stack([w_s_a, jnp.tile(w_s_a[:, :, :half, :half], (1, 1, 2, 2))], axis=1)
    b_s2 = jnp.stack([b_s_a, jnp.tile(b_s_a[:, :, :half], (1, 1, 2))], axis=1)
    b_s2 = jnp.swapaxes(b_s2, 2, 3)
    s_all = jnp.concatenate([jnp.zeros((depth, 1) + state_b_s.shape[2:], F32), state_b_s], axis=1)
    bias_near, bias_far = _bias_tables(rel_bias)
    bias_near_s = bias_near[:, :CHUNK, :NEAR_SPAN + CHUNK]
    pk_all = cache_c_k.reshape(depth, nbs, p, KV_C)
    pv_all = cache_c_v.reshape(depth, nbs, p, KV_C)

    outs = [[] for _ in range(9)]
    for l in range(depth):
        h = _inproj(x, norm_mix[l][None], w_in_p[l])
        hp, hs = h[:t], h[t:]
        ya, vn = _gmlp(h, norm_a_v[l][None], w_s2[l], b_s2[l], t)
        yb, s_out = _gla(h, w_gate[l], b_gate_b[l][None], norm_b_o[l][None], s_all[l], t)
        kvb = hp[:, C_KC:C_KC + 2 * KV_C].astype(BF16)
        kib = hp[:, C_MISC:C_MISC + D_IDX].astype(BF16)
        yc_p = _dsa_prompt(h, kvb, kib, bias_near, bias_far, t)
        yc_s = _dsa_sample(h, pk_all[l], pv_all[l], cache_c_kidx[l], bias_near_s, bias_far, t)
        yc = jnp.concatenate([yc_p, yc_s], axis=0)
        wo = w_o_b[l]
        x = _outproj(x, ya, yb, yc, wo[:WIDTH_A], wo[WIDTH_A:WIDTH_A + WIDTH_B], wo[WIDTH_A + WIDTH_B:])
        x = _ffn(x, norm_ffn[l][None], w_up_b[l], w_down_b[l])

        outs[0].append(hp[:, C_KC:C_KC + KV_C].reshape(1, t, N_KV_C, HD_C))
        outs[1].append(hp[:, C_VC:C_VC + KV_C].reshape(1, t, N_KV_C, HD_C))
        outs[2].append(hp[:, C_MISC:C_MISC + D_IDX].reshape(1, t, D_IDX))
        outs[3].append(s_out[:1])
        outs[4].append(hs[:, C_KC:C_KC + KV_C].reshape(nbs, ts, N_KV_C, HD_C))
        outs[5].append(hs[:, C_VC:C_VC + KV_C].reshape(nbs, ts, N_KV_C, HD_C))
        outs[6].append(hs[:, C_MISC:C_MISC + D_IDX].reshape(nbs, ts, D_IDX))
        outs[7].append(s_out[1:])
        outs[8].append(vn.reshape(nbs, ts, WIDTH_A))

    y = _final_norm(x, norm_final[None])
    return (y[:t][None], y[t:].reshape(nbs, ts, D_MODEL)) + tuple(jnp.stack(o) for o in outs)
```

```python
import functools

import jax
import jax.numpy as jnp
from jax import lax
from jax.experimental import pallas as pl
from jax.experimental.pallas import tpu as pltpu

BF16 = jnp.bfloat16
F32 = jnp.float32
I32 = jnp.int32

D_MODEL = 2048
EPS = 1e-6
CHUNK = 64
GROUP_A = 128
N_GROUPS_A = 4
WIDTH_A = 512
N_HEADS_B = 6
DK_B = 64
DV_B = 128
GATE_RANK_B = 16
GATE_TEMP_B = 16.0
WIDTH_B = 768
N_HEADS_C = 6
N_KV_C = 2
HD_C = 128
N_IDX_HEADS = 8
D_IDX = 64
TOPK_MAX = 256
WIDTH_C = 768
N_BUCKETS = 32
MAX_DISTANCE = 128
D_FF = 4 * D_MODEL
QK_B = N_HEADS_B * DK_B
KV_C = N_KV_C * HD_C
QI_C = N_IDX_HEADS * D_IDX

LANES = 128
VMEM_LIMIT = 56 * 1024 * 1024

C_QB = 0
C_KB = 384
C_VB = 768
C_OG = 1536
C_QC = 2304
C_KC = 3072
C_VC = 3328
C_QIC = 3584
C_UA = 4096
C_VA = 4608
C_MISC = 5120
MISC_LRB = D_IDX
MISC_WIC = D_IDX + GATE_RANK_B
H_WIDTH = 5376

INT_MIN = -2147483648
MASKED = -1e30
LOG2E = 1.4426950408889634
NEAR_SPAN = 128
FAR_KW = 512


def _pick(n, cands):
    for c in cands:
        if n % c == 0:
            return c
    raise ValueError(f"no tile for {n}")


def _cparams(sem):
    return pltpu.CompilerParams(dimension_semantics=sem, vmem_limit_bytes=VMEM_LIMIT)


def _rms(x):
    return x * lax.rsqrt(jnp.mean(x * x, axis=-1, keepdims=True) + EPS)


def _dot(a, b):
    return jnp.dot(a, b, preferred_element_type=F32)


def _dot_nt(a, b):
    return lax.dot_general(a, b, (((1,), (1,)), ((), ())), preferred_element_type=F32)


def _dot_tn(a, b):
    return lax.dot_general(a, b, (((0,), (0,)), ((), ())), preferred_element_type=F32)


def _inproj_kernel(x_ref, g_ref, w_ref, o_ref, xn_ref):
    @pl.when(pl.program_id(1) == 0)
    def _():
        xn_ref[...] = (_rms(x_ref[...]) * g_ref[...]).astype(BF16)

    o_ref[...] = _dot(xn_ref[...], w_ref[...])


def _inproj(x, g, w):
    m = x.shape[0]
    tm = _pick(m, (1024, 512, 256, 128))
    tn = 768
    return pl.pallas_call(
        _inproj_kernel,
        grid=(m // tm, H_WIDTH // tn),
        in_specs=[pl.BlockSpec((tm, D_MODEL), lambda i, j: (i, 0)),
                  pl.BlockSpec((1, D_MODEL), lambda i, j: (0, 0)),
                  pl.BlockSpec((D_MODEL, tn), lambda i, j: (0, j))],
        out_specs=pl.BlockSpec((tm, tn), lambda i, j: (i, j)),
        out_shape=jax.ShapeDtypeStruct((m, H_WIDTH), F32),
        scratch_shapes=[pltpu.VMEM((tm, D_MODEL), BF16)],
        compiler_params=_cparams(("arbitrary", "arbitrary")),
        name="inproj",
    )(x, g, w)


def _gmlp_kernel(h_ref, gv_ref, w_ref, b_ref, ya_ref, vn_ref, *, n_prompt_blocks):
    i = pl.program_id(0)
    hv = h_ref[...]
    u = jax.nn.gelu(hv[:, :WIDTH_A])
    v = jax.nn.gelu(hv[:, WIDTH_A:])
    r = lax.broadcasted_iota(I32, (GROUP_A, GROUP_A), 0)
    c = lax.broadcasted_iota(I32, (GROUP_A, GROUP_A), 1)
    samp = (i >= n_prompt_blocks).astype(I32)
    keep = (c <= r) & (((r >> 6) * samp) == ((c >> 6) * samp))
    for g in range(N_GROUPS_A):
        sl = slice(g * GROUP_A, (g + 1) * GROUP_A)
        vn = _rms(v[:, sl]) * gv_ref[:, sl]
        vn_ref[:, sl] = vn
        wm = jnp.where(keep, w_ref[0, g], 0.0).astype(BF16)
        z = _dot(wm, vn.astype(BF16)) + b_ref[0][:, g:g + 1]
        ya_ref[:, sl] = (u[:, sl] * z).astype(BF16)


def _gmlp(h, gv, w2, b2, n_prompt_rows):
    m = h.shape[0]
    npb = n_prompt_rows // GROUP_A
    nsb = (m - n_prompt_rows) // GROUP_A
    return pl.pallas_call(
        functools.partial(_gmlp_kernel, n_prompt_blocks=npb),
        grid=(npb + nsb,),
        in_specs=[pl.BlockSpec((GROUP_A, 2 * WIDTH_A), lambda i: (i, C_UA // (2 * WIDTH_A))),
                  pl.BlockSpec((1, WIDTH_A), lambda i: (0, 0)),
                  pl.BlockSpec((1, N_GROUPS_A, GROUP_A, GROUP_A),
                               lambda i: (jnp.where(i >= npb, 1, 0), 0, 0, 0)),
                  pl.BlockSpec((1, GROUP_A, N_GROUPS_A),
                               lambda i: (jnp.where(i >= npb, 1, 0), 0, 0))],
        out_specs=[pl.BlockSpec((GROUP_A, WIDTH_A), lambda i: (i, 0)),
                   pl.BlockSpec((GROUP_A, WIDTH_A), lambda i: (jnp.maximum(i - npb, 0), 0))],
        out_shape=[jax.ShapeDtypeStruct((m, WIDTH_A), BF16),
                   jax.ShapeDtypeStruct((nsb * GROUP_A, WIDTH_A), F32)],
        compiler_params=_cparams(("arbitrary",)),
        name="gmlp",
    )(h, gv, w2, b2)


def _gla_kernel(q_ref, k_ref, v_ref, og_ref, misc_ref, wg_ref, bg_ref, gbo_ref, s0_ref,
                yb_ref, sout_ref, st_ref, *, n_prompt_chunks):
    c = pl.program_id(0)

    @pl.when((c == 0) | (c >= n_prompt_chunks))
    def _():
        for h in range(N_HEADS_B):
            st_ref[h] = s0_ref[0, h].T

    lrb = misc_ref[:, MISC_LRB:MISC_LRB + GATE_RANK_B].astype(BF16)
    x = _dot(lrb, wg_ref[...]) + bg_ref[...]
    g = (jnp.minimum(x, 0.0) - jnp.log1p(jnp.exp(-jnp.abs(x)))) * (1.0 / GATE_TEMP_B)
    rr = lax.broadcasted_iota(I32, (CHUNK, CHUNK), 0)
    cc = lax.broadcasted_iota(I32, (CHUNK, CHUNK), 1)
    tril = cc <= rr
    tri = jnp.where(tril, 1.0, 0.0).astype(BF16)
    g1 = g.astype(BF16)
    r1 = g - g1.astype(F32)
    g2 = r1.astype(BF16)
    g3 = (r1 - g2.astype(F32)).astype(BF16)
    b = _dot(tri, g1) + _dot(tri, g2) + _dot(tri, g3)
    b_last = b[CHUNK - 1:CHUNK, :]
    b_mid = b[CHUNK // 2 - 1:CHUNK // 2, :]
    qs = q_ref[...] * (DK_B ** -0.5)
    kk = k_ref[...]
    q_inter = (qs * jnp.exp(b)).astype(BF16)
    q_intra = (qs * jnp.exp(b - b_mid)).astype(BF16)
    k_intra = (kk * jnp.exp(b_mid - b)).astype(BF16)
    k_state = (kk * jnp.exp(b_last - b)).astype(BF16)
    dec = jnp.exp(b_last)
    for h in range(N_HEADS_B):
        sk = slice(h * DK_B, (h + 1) * DK_B)
        sv = slice(h * DV_B, (h + 1) * DV_B)
        st = st_ref[h]
        vh = v_ref[:, sv].astype(BF16)
        o = _dot_nt(q_inter[:, sk], st.astype(BF16))
        att = jnp.where(tril, _dot_nt(q_intra[:, sk], k_intra[:, sk]), 0.0)
        o = o + _dot(att.astype(BF16), vh)
        st_new = st * dec[:, sk] + _dot_tn(vh, k_state[:, sk])
        st_ref[h] = st_new
        sout_ref[0, h] = st_new.T
        on = _rms(o) * gbo_ref[:, sv]
        yb_ref[:, sv] = (on * jax.nn.silu(og_ref[:, sv])).astype(BF16)


def _gla(h, wg, bg, gbo, s_all, n_prompt_rows):
    m = h.shape[0]
    npc = n_prompt_rows // CHUNK
    nch = m // CHUNK
    nseq = s_all.shape[0]

    def seq(c):
        return jnp.where(c < npc, 0, c - npc + 1)

    return pl.pallas_call(
        functools.partial(_gla_kernel, n_prompt_chunks=npc),
        grid=(nch,),
        in_specs=[pl.BlockSpec((CHUNK, QK_B), lambda c: (c, C_QB // QK_B)),
                  pl.BlockSpec((CHUNK, QK_B), lambda c: (c, C_KB // QK_B)),
                  pl.BlockSpec((CHUNK, WIDTH_B), lambda c: (c, C_VB // WIDTH_B)),
                  pl.BlockSpec((CHUNK, WIDTH_B), lambda c: (c, C_OG // WIDTH_B)),
                  pl.BlockSpec((CHUNK, LANES), lambda c: (c, C_MISC // LANES)),
                  pl.BlockSpec((GATE_RANK_B, QK_B), lambda c: (0, 0)),
                  pl.BlockSpec((1, QK_B), lambda c: (0, 0)),
                  pl.BlockSpec((1, WIDTH_B), lambda c: (0, 0)),
                  pl.BlockSpec((1, N_HEADS_B, DK_B, DV_B), lambda c: (seq(c), 0, 0, 0))],
        out_specs=[pl.BlockSpec((CHUNK, WIDTH_B), lambda c: (c, 0)),
                   pl.BlockSpec((1, N_HEADS_B, DK_B, DV_B), lambda c: (seq(c), 0, 0, 0))],
        out_shape=[jax.ShapeDtypeStruct((m, WIDTH_B), BF16),
                   jax.ShapeDtypeStruct((nseq, N_HEADS_B, DK_B, DV_B), F32)],
        scratch_shapes=[pltpu.VMEM((N_HEADS_B, DV_B, DK_B), F32)],
        compiler_params=_cparams(("arbitrary",)),
        name="gla",
    )(h, h, h, h, h, wg, bg, gbo, s_all)


def _fold_lanes(x):
    n = x.shape[1]
    acc = x[:, :LANES]
    for j in range(1, n // LANES):
        acc = acc + x[:, j * LANES:(j + 1) * LANES]
    return acc


def _dsa_core(q, qi, wi, far_ki, far_k, far_v, nf, far_adm, near_ki, near_k, near_v, near_adm,
              bias_near_ref, out_ref, fkeys_ref, nkeys_ref, m_ref, acc_ref, *, tq, kw, nw, topk):
    rep = N_HEADS_C // N_KV_C
    qi_b = (qi * (D_IDX ** -0.5)).astype(BF16)
    qs = q * (HD_C ** -0.5 * LOG2E)
    q_g = [jnp.concatenate([qs[:, (g * rep + r) * HD_C:(g * rep + r + 1) * HD_C] for r in range(rep)],
                           axis=0).astype(BF16) for g in range(N_KV_C)]
    kf = float(topk)

    def keys_of(ki_blk, adm):
        acc = None
        for h in range(N_IDX_HEADS):
            d = _dot_nt(qi_b[:, h * D_IDX:(h + 1) * D_IDX], ki_blk)
            t = wi[:, h:h + 1] * jnp.maximum(d, 0.0)
            acc = t if acc is None else acc + t
        bits = lax.bitcast_convert_type(acc, I32)
        key = bits ^ ((bits >> 31) & 0x7FFFFFFF)
        key = jnp.where(acc == 0.0, 0, key)
        if adm is not None:
            key = jnp.where(adm, key, INT_MIN)
        return key

    def fill(c, carry):
        fkeys_ref[c] = keys_of(far_ki(c), None if far_adm is None else far_adm(c))
        return carry

    lax.fori_loop(0, nf, fill, 0)
    nkeys_ref[...] = keys_of(near_ki, near_adm)

    def count(thr, strict):
        def cmp(x):
            hit = (x > thr) if strict else (x >= thr)
            return _fold_lanes(jnp.where(hit, 1.0, 0.0))

        part = lax.fori_loop(0, nf, lambda c, a: a + cmp(fkeys_ref[c]), jnp.zeros((tq, LANES), F32))
        part = part + cmp(nkeys_ref[...])
        return jnp.sum(part, axis=-1, keepdims=True)

    def search(t, cand):
        trial = cand + jnp.left_shift(jnp.int32(1), 31 - t)
        return jnp.where(count(trial, False) >= kf, trial, cand)

    thr = lax.fori_loop(0, 32, search, jnp.full((tq, 1), INT_MIN, I32))
    need = kf - count(thr, True)
    tie_rows = (count(thr, False) > kf) & (thr > INT_MIN)
    has_tie = jnp.max(jnp.where(tie_rows, 1.0, 0.0)) > 0.0

    @pl.when(has_tie)
    def _():
        def upper(n):
            a = lax.broadcasted_iota(I32, (n, n), 0)
            b = lax.broadcasted_iota(I32, (n, n), 1)
            return jnp.where(a < b, 1.0, 0.0).astype(BF16)

        def demote(keys, run, ut):
            eq = (keys == thr) & (keys > INT_MIN)
            eqf = jnp.where(eq, 1.0, 0.0)
            before = _dot(eqf.astype(BF16), ut) + run
            keys = jnp.where(eq & (before >= need), INT_MIN, keys)
            return keys, run + jnp.sum(eqf, axis=-1, keepdims=True)

        ut_far = upper(kw)

        def step(c, run):
            keys, run = demote(fkeys_ref[c], run, ut_far)
            fkeys_ref[c] = keys
            return run

        run = lax.fori_loop(0, nf, step, jnp.zeros((tq, 1), F32))
        keys, _ = demote(nkeys_ref[...], run, upper(nw))
        nkeys_ref[...] = keys

    thr_sel = jnp.maximum(thr, INT_MIN + 1)
    m_ref[...] = jnp.full(m_ref.shape, MASKED, F32)
    acc_ref[...] = jnp.zeros(acc_ref.shape, F32)

    def attend(keys, k_blk, v_blk, bias_ref, n):
        selb = jnp.where(keys >= thr_sel, 0.0, MASKED)
        ones = jnp.ones((n, HD_C), BF16)
        groups = range(N_KV_C)
        sls = [slice(g * HD_C, (g + 1) * HD_C) for g in groups]
        ss = [_dot_nt(q_g[g], k_blk[:, sls[g]]) for g in groups]
        ps = []
        for g in groups:
            s = (ss[g].reshape(rep, tq, n) + selb[None]).reshape(rep * tq, n)
            if bias_ref is not None:
                s = s + bias_ref[g]
            m_old = m_ref[g]
            m_new = jnp.maximum(m_old, jnp.max(s, axis=-1, keepdims=True))
            ps.append(jnp.exp2(s - m_new).astype(BF16))
            acc_ref[g] = jnp.exp2(m_old - m_new) * acc_ref[g]
            m_ref[g] = m_new
        pv = [_dot(ps[g], jnp.concatenate([v_blk[:, sls[g]], ones], axis=1)) for g in groups]
        for g in groups:
            acc_ref[g] += pv[g]

    def far_step(c, carry):
        attend(fkeys_ref[c], far_k(c), far_v(c), None, kw)
        return carry

    lax.fori_loop(0, nf, far_step, 0)
    attend(nkeys_ref[...], near_k, near_v, bias_near_ref, nw)

    for g in range(N_KV_C):
        a = acc_ref[g]
        for r in range(rep):
            hh = g * rep + r
            rows = slice(r * tq, (r + 1) * tq)
            out_ref[:, hh * HD_C:(hh + 1) * HD_C] = (a[rows, :HD_C] / a[rows, HD_C:HD_C + 1]).astype(BF16)


def _dsa_prompt_kernel(q_ref, qi_ref, misc_ref, kv_ref, ki_ref, bias_near_ref,
                       out_ref, fkeys_ref, nkeys_ref, m_ref, acc_ref, *, tq, topk):
    i = pl.program_id(0)
    kw = FAR_KW
    nw = 2 * tq
    far_limit = tq * (i - 1)
    nf = (i + 2) // (kw // tq)
    wi = misc_ref[:, MISC_WIC:MISC_WIC + N_IDX_HEADS] * (N_IDX_HEADS ** -0.5)

    def rows(c):
        return pl.ds(pl.multiple_of(c * kw, kw), kw)

    def far_adm(c):
        return (c * kw + lax.broadcasted_iota(I32, (tq, kw), 1)) < far_limit

    left = pl.ds(pl.multiple_of(jnp.maximum(i - 1, 0) * tq, tq), tq)
    right = pl.ds(pl.multiple_of(i * tq, tq), tq)
    near_ki = jnp.concatenate([ki_ref[left, :], ki_ref[right, :]], axis=0)
    near_kv = jnp.concatenate([kv_ref[left, :], kv_ref[right, :]], axis=0)
    col = lax.broadcasted_iota(I32, (tq, nw), 1)
    row = lax.broadcasted_iota(I32, (tq, nw), 0)
    first_col = jnp.where(i > 0, 0, tq)
    near_adm = (col >= first_col) & (((col - tq) >> 6) <= (row >> 6))

    _dsa_core(q_ref[...], qi_ref[...], wi,
              lambda c: ki_ref[rows(c), :], lambda c: kv_ref[rows(c), :KV_C],
              lambda c: kv_ref[rows(c), KV_C:], nf, far_adm,
              near_ki, near_kv[:, :KV_C], near_kv[:, KV_C:], near_adm,
              bias_near_ref, out_ref, fkeys_ref, nkeys_ref, m_ref, acc_ref,
              tq=tq, kw=kw, nw=nw, topk=topk)


def _dsa_scratch(nf, tq, kw, nw):
    rep = N_HEADS_C // N_KV_C
    return [pltpu.VMEM((nf, tq, kw), I32),
            pltpu.VMEM((tq, nw), I32),
            pltpu.VMEM((N_KV_C, rep * tq, 1), F32),
            pltpu.VMEM((N_KV_C, rep * tq, 2 * HD_C), F32)]


def _dsa_prompt(h, kvb, kib, bias_near, t):
    tq = NEAR_SPAN
    topk = min(TOPK_MAX, t // 4)
    return pl.pallas_call(
        functools.partial(_dsa_prompt_kernel, tq=tq, topk=topk),
        grid=(t // tq,),
        in_specs=[pl.BlockSpec((tq, WIDTH_C), lambda i: (i, C_QC // WIDTH_C)),
                  pl.BlockSpec((tq, QI_C), lambda i: (i, C_QIC // QI_C)),
                  pl.BlockSpec((tq, LANES), lambda i: (i, C_MISC // LANES)),
                  pl.BlockSpec((t, 2 * KV_C), lambda i: (0, 0)),
                  pl.BlockSpec((t, D_IDX), lambda i: (0, 0)),
                  pl.BlockSpec(bias_near.shape, lambda i: (0, 0, 0))],
        out_specs=pl.BlockSpec((tq, WIDTH_C), lambda i: (i, 0)),
        out_shape=jax.ShapeDtypeStruct((t, WIDTH_C), BF16),
        scratch_shapes=_dsa_scratch(t // FAR_KW, tq, FAR_KW, 2 * tq),
        compiler_params=_cparams(("arbitrary",)),
        name="dsa_prompt",
    )(h, h, h, kvb, kib, bias_near)


def _dsa_sample_kernel(q_ref, qi_ref, misc_ref, kvn_ref, pk_ref, pv_ref, pki_ref, bias_near_ref,
                       out_ref, fkeys_ref, nkeys_ref, m_ref, acc_ref, *, kw, nf, topk):
    tq = CHUNK
    far_len = kw * nf
    misc = misc_ref[...]
    wi = misc[:, MISC_WIC:MISC_WIC + N_IDX_HEADS] * (N_IDX_HEADS ** -0.5)
    tail = pl.ds(far_len, NEAR_SPAN)
    near_ki = jnp.concatenate([pki_ref[0, tail, :], misc[:, :D_IDX]], axis=0).astype(BF16)
    near_k = jnp.concatenate([pk_ref[0, tail, :], kvn_ref[:, :KV_C]], axis=0).astype(BF16)
    near_v = jnp.concatenate([pv_ref[0, tail, :], kvn_ref[:, KV_C:]], axis=0).astype(BF16)

    def rows(c):
        return pl.ds(pl.multiple_of(c * kw, kw), kw)

    _dsa_core(q_ref[...], qi_ref[...], wi,
              lambda c: pki_ref[0, rows(c), :].astype(BF16), lambda c: pk_ref[0, rows(c), :].astype(BF16),
              lambda c: pv_ref[0, rows(c), :].astype(BF16), nf, None,
              near_ki, near_k, near_v, None,
              bias_near_ref, out_ref, fkeys_ref, nkeys_ref, m_ref, acc_ref,
              tq=tq, kw=kw, nw=NEAR_SPAN + CHUNK, topk=topk)


def _dsa_sample(h, pk, pv, pki, bias_near, t):
    nb, p = pk.shape[0], pk.shape[1]
    tq = CHUNK
    far_len = p - NEAR_SPAN
    kw = _pick(far_len, (640, 512, 384, 256, 128))
    nf = far_len // kw
    nw = NEAR_SPAN + CHUNK
    topk = min(TOPK_MAX, (p + CHUNK) // 4)
    row0 = t // tq
    return pl.pallas_call(
        functools.partial(_dsa_sample_kernel, kw=kw, nf=nf, topk=topk),
        grid=(nb,),
        in_specs=[pl.BlockSpec((tq, WIDTH_C), lambda b: (row0 + b, C_QC // WIDTH_C)),
                  pl.BlockSpec((tq, QI_C), lambda b: (row0 + b, C_QIC // QI_C)),
                  pl.BlockSpec((tq, LANES), lambda b: (row0 + b, C_MISC // LANES)),
                  pl.BlockSpec((tq, 2 * KV_C), lambda b: (row0 + b, C_KC // (2 * KV_C))),
                  pl.BlockSpec((1, p, KV_C), lambda b: (b, 0, 0)),
                  pl.BlockSpec((1, p, KV_C), lambda b: (b, 0, 0)),
                  pl.BlockSpec((1, p, D_IDX), lambda b: (b, 0, 0)),
                  pl.BlockSpec(bias_near.shape, lambda b: (0, 0, 0))],
        out_specs=pl.BlockSpec((tq, WIDTH_C), lambda b: (b, 0)),
        out_shape=jax.ShapeDtypeStruct((nb * tq, WIDTH_C), BF16),
        scratch_shapes=_dsa_scratch(nf, tq, kw, nw),
        compiler_params=_cparams(("arbitrary",)),
        name="dsa_sample",
    )(h, h, h, h, pk, pv, pki, bias_near)


def _outproj_kernel(x_ref, ya_ref, yb_ref, yc_ref, wa_ref, wb_ref, wc_ref, o_ref):
    o_ref[...] = (x_ref[...] + _dot(ya_ref[...], wa_ref[...]) + _dot(yb_ref[...], wb_ref[...])
                  + _dot(yc_ref[...], wc_ref[...]))


def _outproj(x, ya, yb, yc, wa, wb, wc):
    m = x.shape[0]
    tm = _pick(m, (1024, 512, 256, 128))
    tn = 512
    return pl.pallas_call(
        _outproj_kernel,
        grid=(m // tm, D_MODEL // tn),
        in_specs=[pl.BlockSpec((tm, tn), lambda i, j: (i, j)),
                  pl.BlockSpec((tm, WIDTH_A), lambda i, j: (i, 0)),
                  pl.BlockSpec((tm, WIDTH_B), lambda i, j: (i, 0)),
                  pl.BlockSpec((tm, WIDTH_C), lambda i, j: (i, 0)),
                  pl.BlockSpec((WIDTH_A, tn), lambda i, j: (0, j)),
                  pl.BlockSpec((WIDTH_B, tn), lambda i, j: (0, j)),
                  pl.BlockSpec((WIDTH_C, tn), lambda i, j: (0, j))],
        out_specs=pl.BlockSpec((tm, tn), lambda i, j: (i, j)),
        out_shape=jax.ShapeDtypeStruct((m, D_MODEL), F32),
        compiler_params=_cparams(("arbitrary", "arbitrary")),
        name="outproj",
    )(x, ya, yb, yc, wa, wb, wc)


def _ffn_kernel(x_ref, g_ref, wu_ref, wd_ref, o_ref, xn_ref, acc_ref):
    f = pl.program_id(1)

    @pl.when(f == 0)
    def _():
        xn_ref[...] = (_rms(x_ref[...]) * g_ref[...]).astype(BF16)
        acc_ref[...] = jnp.zeros(acc_ref.shape, F32)

    a = jnp.maximum(_dot(xn_ref[...], wu_ref[...]), 0.0)
    acc_ref[...] += _dot((a * a).astype(BF16), wd_ref[...])

    @pl.when(f == pl.num_programs(1) - 1)
    def _():
        o_ref[...] = x_ref[...] + acc_ref[...]


def _ffn(x, g, wu, wd):
    m = x.shape[0]
    tm = _pick(m, (512, 256, 128))
    tf = 512
    return pl.pallas_call(
        _ffn_kernel,
        grid=(m // tm, D_FF // tf),
        in_specs=[pl.BlockSpec((tm, D_MODEL), lambda i, f: (i, 0)),
                  pl.BlockSpec((1, D_MODEL), lambda i, f: (0, 0)),
                  pl.BlockSpec((D_MODEL, tf), lambda i, f: (0, f)),
                  pl.BlockSpec((tf, D_MODEL), lambda i, f: (f, 0))],
        out_specs=pl.BlockSpec((tm, D_MODEL), lambda i, f: (i, 0)),
        out_shape=jax.ShapeDtypeStruct((m, D_MODEL), F32),
        scratch_shapes=[pltpu.VMEM((tm, D_MODEL), BF16), pltpu.VMEM((tm, D_MODEL), F32)],
        compiler_params=_cparams(("arbitrary", "arbitrary")),
        name="ffn",
    )(x, g, wu, wd)


def _norm_kernel(x_ref, g_ref, o_ref):
    o_ref[...] = _rms(x_ref[...]) * g_ref[...]


def _final_norm(x, g):
    m = x.shape[0]
    tm = _pick(m, (1024, 512, 256, 128))
    return pl.pallas_call(
        _norm_kernel,
        grid=(m // tm,),
        in_specs=[pl.BlockSpec((tm, D_MODEL), lambda i: (i, 0)),
                  pl.BlockSpec((1, D_MODEL), lambda i: (0, 0))],
        out_specs=pl.BlockSpec((tm, D_MODEL), lambda i: (i, 0)),
        out_shape=jax.ShapeDtypeStruct((m, D_MODEL), F32),
        compiler_params=_cparams(("arbitrary",)),
        name="final_norm",
    )(x, g)


def _t5_bucket(rel):
    half = N_BUCKETS // 2
    max_exact = half // 2
    n = jnp.abs(rel)
    nf = jnp.maximum(n, 1).astype(F32)
    large = max_exact + (jnp.log(nf / max_exact) / jnp.log(MAX_DISTANCE / max_exact)
                         * (half - max_exact)).astype(I32)
    large = jnp.minimum(large, half - 1)
    return jnp.where(rel > 0, half, 0) + jnp.where(n < max_exact, n, large)


def _bias_table(rel_bias, tq, nw):
    t = jnp.arange(tq, dtype=I32)[:, None]
    j = jnp.arange(nw, dtype=I32)[None, :]
    near = jnp.transpose(rel_bias[_t5_bucket(j - NEAR_SPAN - t)], (2, 0, 1))
    far = rel_bias[_t5_bucket(jnp.int32(-(NEAR_SPAN + 1)))]
    rep = N_HEADS_C // N_KV_C
    return ((near - far[:, None, None]) * LOG2E).astype(F32).reshape(N_KV_C, rep * tq, nw)


def _permute_w_in(w):
    sizes = (WIDTH_A, WIDTH_A, QK_B, QK_B, WIDTH_B, GATE_RANK_B, WIDTH_B, WIDTH_C, KV_C, KV_C, QI_C,
             D_IDX, N_IDX_HEADS)
    parts, off = [], 0
    for n in sizes:
        parts.append(w[..., off:off + n])
        off += n
    ua, va, qb, kb, vb, lrb, og, qc, kc, vc, qic, kic, wic = parts
    pad = jnp.zeros(w.shape[:-1] + (H_WIDTH - (C_MISC + D_IDX + GATE_RANK_B + N_IDX_HEADS),), w.dtype)
    return jnp.concatenate([qb, kb, vb, og, qc, kc, vc, qic, ua, va, kic, lrb, wic, pad], axis=-1).astype(BF16)


def kernel(x_prompt, x_sample, cache_c_k, cache_c_v, cache_c_kidx, state_b_s, norm_mix, w_in, norm_a_v,
           w_s_a, b_s_a, w_gate_b, b_gate_b, norm_b_o, rel_bias, w_o, norm_ffn, w_up, w_down, norm_final):
    depth = w_in.shape[0]
    nbp, t, _ = x_prompt.shape
    nbs, ts, _ = x_sample.shape
    p = cache_c_k.shape[2]
    assert nbp == 1 and ts == CHUNK and t % FAR_KW == 0 and (nbs * ts) % GROUP_A == 0
    assert p % NEAR_SPAN == 0 and p >= 2 * NEAR_SPAN

    x = jnp.concatenate([x_prompt[0], x_sample.reshape(nbs * ts, D_MODEL)], axis=0)
    w_in_p = _permute_w_in(w_in)
    w_o_b = w_o.astype(BF16)
    w_up_b = w_up.astype(BF16)
    w_down_b = w_down.astype(BF16)
    w_gate = w_gate_b.astype(BF16)
    half = GROUP_A // 2
    w_s2 = jnp.stack([w_s_a, jnp.tile(w_s_a[:, :, :half, :half], (1, 1, 2, 2))], axis=1)
    b_s2 = jnp.stack([b_s_a, jnp.tile(b_s_a[:, :, :half], (1, 1, 2))], axis=1)
    b_s2 = jnp.swapaxes(b_s2, 2, 3)
    s_all = jnp.concatenate([jnp.zeros((depth, 1) + state_b_s.shape[2:], F32), state_b_s], axis=1)
    bias_near = _bias_table(rel_bias, NEAR_SPAN, 2 * NEAR_SPAN)
    bias_near_s = _bias_table(rel_bias, CHUNK, NEAR_SPAN + CHUNK)
    pk_all = cache_c_k.reshape(depth, nbs, p, KV_C)
    pv_all = cache_c_v.reshape(depth, nbs, p, KV_C)

    outs = [[] for _ in range(9)]
    for l in range(depth):
        h = _inproj(x, norm_mix[l][None], w_in_p[l])
        hp, hs = h[:t], h[t:]
        ya, vn = _gmlp(h, norm_a_v[l][None], w_s2[l], b_s2[l], t)
        yb, s_out = _gla(h, w_gate[l], b_gate_b[l][None], norm_b_o[l][None], s_all[l], t)
        kvb = hp[:, C_KC:C_KC + 2 * KV_C].astype(BF16)
        kib = hp[:, C_MISC:C_MISC + D_IDX].astype(BF16)
        yc_p = _dsa_prompt(h, kvb, kib, bias_near, t)
        yc_s = _dsa_sample(h, pk_all[l], pv_all[l], cache_c_kidx[l], bias_near_s, t)
        yc = jnp.concatenate([yc_p, yc_s], axis=0)
        wo = w_o_b[l]
        x = _outproj(x, ya, yb, yc, wo[:WIDTH_A], wo[WIDTH_A:WIDTH_A + WIDTH_B], wo[WIDTH_A + WIDTH_B:])
        x = _ffn(x, norm_ffn[l][None], w_up_b[l], w_down_b[l])

        outs[0].append(hp[:, C_KC:C_KC + KV_C].reshape(1, t, N_KV_C, HD_C))
        outs[1].append(hp[:, C_VC:C_VC + KV_C].reshape(1, t, N_KV_C, HD_C))
        outs[2].append(hp[:, C_MISC:C_MISC + D_IDX].reshape(1, t, D_IDX))
        outs[3].append(s_out[:1])
        outs[4].append(hs[:, C_KC:C_KC + KV_C].reshape(nbs, ts, N_KV_C, HD_C))
        outs[5].append(hs[:, C_VC:C_VC + KV_C].reshape(nbs, ts, N_KV_C, HD_C))
        outs[6].append(hs[:, C_MISC:C_MISC + D_IDX].reshape(nbs, ts, D_IDX))
        outs[7].append(s_out[1:])
        outs[8].append(vn.reshape(nbs, ts, WIDTH_A))

    y = _final_norm(x, norm_final[None])
    return (y[:t][None], y[t:].reshape(nbs, ts, D_MODEL)) + tuple(jnp.stack(o) for o in outs)
```

```python
import functools

import jax
import jax.numpy as jnp
from jax import lax
from jax.experimental import pallas as pl
from jax.experimental.pallas import tpu as pltpu

BF16 = jnp.bfloat16
F32 = jnp.float32
I32 = jnp.int32
I16 = jnp.int16

D_MODEL = 2048
EPS = 1e-6
CHUNK = 64
GROUP_A = 128
N_GROUPS_A = 4
WIDTH_A = 512
N_HEADS_B = 6
DK_B = 64
DV_B = 128
GATE_RANK_B = 16
GATE_TEMP_B = 16.0
WIDTH_B = 768
N_HEADS_C = 6
N_KV_C = 2
HD_C = 128
N_IDX_HEADS = 8
D_IDX = 64
TOPK_MAX = 256
WIDTH_C = 768
N_BUCKETS = 32
MAX_DISTANCE = 128
D_FF = 4 * D_MODEL
QK_B = N_HEADS_B * DK_B
KV_C = N_KV_C * HD_C
QI_C = N_IDX_HEADS * D_IDX

LANES = 128
SUBLANES = 8
ONES_ROWS = 16
VMEM_LIMIT = 56 * 1024 * 1024

C_QB = 0
C_KB = 384
C_VB = 768
C_OG = 1536
C_QC = 2304
C_KC = 3072
C_VC = 3328
C_QIC = 3584
C_UA = 4096
C_VA = 4608
C_MISC = 5120
MISC_LRB = D_IDX
MISC_WIC = D_IDX + GATE_RANK_B
H_WIDTH = 5376

INT_MIN = -2147483648
HALF = 32768
MASKED = -1e30
LOG2E = 1.4426950408889634
NEAR_SPAN = 128
FAR_KW = 512


def _pick(n, cands):
    for c in cands:
        if n % c == 0:
            return c
    raise ValueError(f"no tile for {n}")


def _cparams(sem):
    return pltpu.CompilerParams(dimension_semantics=sem, vmem_limit_bytes=VMEM_LIMIT)


def _rms(x):
    return x * lax.rsqrt(jnp.mean(x * x, axis=-1, keepdims=True) + EPS)


def _dot(a, b):
    return jnp.dot(a, b, preferred_element_type=F32)


def _dot_nt(a, b):
    return lax.dot_general(a, b, (((1,), (1,)), ((), ())), preferred_element_type=F32)


def _dot_tn(a, b):
    return lax.dot_general(a, b, (((0,), (0,)), ((), ())), preferred_element_type=F32)


def _inproj_kernel(x_ref, g_ref, w_ref, o_ref, xn_ref):
    @pl.when(pl.program_id(1) == 0)
    def _():
        xn_ref[...] = (_rms(x_ref[...]) * g_ref[...]).astype(BF16)

    o_ref[...] = _dot(xn_ref[...], w_ref[...])


def _inproj(x, g, w):
    m = x.shape[0]
    tm = _pick(m, (1024, 512, 256, 128))
    tn = 768
    return pl.pallas_call(
        _inproj_kernel,
        grid=(m // tm, H_WIDTH // tn),
        in_specs=[pl.BlockSpec((tm, D_MODEL), lambda i, j: (i, 0)),
                  pl.BlockSpec((1, D_MODEL), lambda i, j: (0, 0)),
                  pl.BlockSpec((D_MODEL, tn), lambda i, j: (0, j))],
        out_specs=pl.BlockSpec((tm, tn), lambda i, j: (i, j)),
        out_shape=jax.ShapeDtypeStruct((m, H_WIDTH), F32),
        scratch_shapes=[pltpu.VMEM((tm, D_MODEL), BF16)],
        compiler_params=_cparams(("arbitrary", "arbitrary")),
        name="inproj",
    )(x, g, w)


def _gmlp_kernel(h_ref, gv_ref, w_ref, b_ref, ya_ref, vn_ref, *, n_prompt_blocks):
    i = pl.program_id(0)
    hv = h_ref[...]
    u = jax.nn.gelu(hv[:, :WIDTH_A])
    v = jax.nn.gelu(hv[:, WIDTH_A:])
    r = lax.broadcasted_iota(I32, (GROUP_A, GROUP_A), 0)
    c = lax.broadcasted_iota(I32, (GROUP_A, GROUP_A), 1)
    samp = (i >= n_prompt_blocks).astype(I32)
    keep = (c <= r) & (((r >> 6) * samp) == ((c >> 6) * samp))
    for g in range(N_GROUPS_A):
        sl = slice(g * GROUP_A, (g + 1) * GROUP_A)
        vn = _rms(v[:, sl]) * gv_ref[:, sl]
        vn_ref[:, sl] = vn
        wm = jnp.where(keep, w_ref[0, g], 0.0).astype(BF16)
        z = _dot(wm, vn.astype(BF16)) + b_ref[0][:, g:g + 1]
        ya_ref[:, sl] = (u[:, sl] * z).astype(BF16)


def _gmlp(h, gv, w2, b2, n_prompt_rows):
    m = h.shape[0]
    npb = n_prompt_rows // GROUP_A
    nsb = (m - n_prompt_rows) // GROUP_A
    return pl.pallas_call(
        functools.partial(_gmlp_kernel, n_prompt_blocks=npb),
        grid=(npb + nsb,),
        in_specs=[pl.BlockSpec((GROUP_A, 2 * WIDTH_A), lambda i: (i, C_UA // (2 * WIDTH_A))),
                  pl.BlockSpec((1, WIDTH_A), lambda i: (0, 0)),
                  pl.BlockSpec((1, N_GROUPS_A, GROUP_A, GROUP_A),
                               lambda i: (jnp.where(i >= npb, 1, 0), 0, 0, 0)),
                  pl.BlockSpec((1, GROUP_A, N_GROUPS_A),
                               lambda i: (jnp.where(i >= npb, 1, 0), 0, 0))],
        out_specs=[pl.BlockSpec((GROUP_A, WIDTH_A), lambda i: (i, 0)),
                   pl.BlockSpec((GROUP_A, WIDTH_A), lambda i: (jnp.maximum(i - npb, 0), 0))],
        out_shape=[jax.ShapeDtypeStruct((m, WIDTH_A), BF16),
                   jax.ShapeDtypeStruct((nsb * GROUP_A, WIDTH_A), F32)],
        compiler_params=_cparams(("arbitrary",)),
        name="gmlp",
    )(h, gv, w2, b2)


def _gla_kernel(q_ref, k_ref, v_ref, og_ref, misc_ref, wg_ref, bg_ref, gbo_ref, s0_ref,
                yb_ref, sout_ref, st_ref, *, n_prompt_chunks):
    c = pl.program_id(0)

    @pl.when((c == 0) | (c >= n_prompt_chunks))
    def _():
        for h in range(N_HEADS_B):
            st_ref[h] = s0_ref[0, h].T

    lrb = misc_ref[:, MISC_LRB:MISC_LRB + GATE_RANK_B].astype(BF16)
    x = _dot(lrb, wg_ref[...]) + bg_ref[...]
    g = (jnp.minimum(x, 0.0) - jnp.log1p(jnp.exp(-jnp.abs(x)))) * (1.0 / GATE_TEMP_B)
    rr = lax.broadcasted_iota(I32, (CHUNK, CHUNK), 0)
    cc = lax.broadcasted_iota(I32, (CHUNK, CHUNK), 1)
    tril = cc <= rr
    tri = jnp.where(tril, 1.0, 0.0).astype(BF16)
    g1 = g.astype(BF16)
    r1 = g - g1.astype(F32)
    g2 = r1.astype(BF16)
    g3 = (r1 - g2.astype(F32)).astype(BF16)
    b = _dot(tri, g1) + _dot(tri, g2) + _dot(tri, g3)
    b_last = b[CHUNK - 1:CHUNK, :]
    b_mid = b[CHUNK // 2 - 1:CHUNK // 2, :]
    qs = q_ref[...] * (DK_B ** -0.5)
    kk = k_ref[...]
    q_inter = (qs * jnp.exp(b)).astype(BF16)
    q_intra = (qs * jnp.exp(b - b_mid)).astype(BF16)
    k_intra = (kk * jnp.exp(b_mid - b)).astype(BF16)
    k_state = (kk * jnp.exp(b_last - b)).astype(BF16)
    dec = jnp.exp(b_last)
    for h in range(N_HEADS_B):
        sk = slice(h * DK_B, (h + 1) * DK_B)
        sv = slice(h * DV_B, (h + 1) * DV_B)
        st = st_ref[h]
        vh = v_ref[:, sv].astype(BF16)
        o = _dot_nt(q_inter[:, sk], st.astype(BF16))
        att = jnp.where(tril, _dot_nt(q_intra[:, sk], k_intra[:, sk]), 0.0)
        o = o + _dot(att.astype(BF16), vh)
        st_new = st * dec[:, sk] + _dot_tn(vh, k_state[:, sk])
        st_ref[h] = st_new
        sout_ref[0, h] = st_new.T
        on = _rms(o) * gbo_ref[:, sv]
        yb_ref[:, sv] = (on * jax.nn.silu(og_ref[:, sv])).astype(BF16)


def _gla(h, wg, bg, gbo, s_all, n_prompt_rows):
    m = h.shape[0]
    npc = n_prompt_rows // CHUNK
    nch = m // CHUNK
    nseq = s_all.shape[0]

    def seq(c):
        return jnp.where(c < npc, 0, c - npc + 1)

    return pl.pallas_call(
        functools.partial(_gla_kernel, n_prompt_chunks=npc),
        grid=(nch,),
        in_specs=[pl.BlockSpec((CHUNK, QK_B), lambda c: (c, C_QB // QK_B)),
                  pl.BlockSpec((CHUNK, QK_B), lambda c: (c, C_KB // QK_B)),
                  pl.BlockSpec((CHUNK, WIDTH_B), lambda c: (c, C_VB // WIDTH_B)),
                  pl.BlockSpec((CHUNK, WIDTH_B), lambda c: (c, C_OG // WIDTH_B)),
                  pl.BlockSpec((CHUNK, LANES), lambda c: (c, C_MISC // LANES)),
                  pl.BlockSpec((GATE_RANK_B, QK_B), lambda c: (0, 0)),
                  pl.BlockSpec((1, QK_B), lambda c: (0, 0)),
                  pl.BlockSpec((1, WIDTH_B), lambda c: (0, 0)),
                  pl.BlockSpec((1, N_HEADS_B, DK_B, DV_B), lambda c: (seq(c), 0, 0, 0))],
        out_specs=[pl.BlockSpec((CHUNK, WIDTH_B), lambda c: (c, 0)),
                   pl.BlockSpec((1, N_HEADS_B, DK_B, DV_B), lambda c: (seq(c), 0, 0, 0))],
        out_shape=[jax.ShapeDtypeStruct((m, WIDTH_B), BF16),
                   jax.ShapeDtypeStruct((nseq, N_HEADS_B, DK_B, DV_B), F32)],
        scratch_shapes=[pltpu.VMEM((N_HEADS_B, DV_B, DK_B), F32)],
        compiler_params=_cparams(("arbitrary",)),
        name="gla",
    )(h, h, h, h, h, wg, bg, gbo, s_all)


def _fold_rows(x, op):
    tile = SUBLANES * (4 // x.dtype.itemsize)
    parts = [x[j * tile:(j + 1) * tile] for j in range(x.shape[0] // tile)]
    while len(parts) > 1:
        nxt = [op(parts[j], parts[j + 1]) for j in range(0, len(parts) - 1, 2)]
        parts = nxt + parts[len(parts) - len(parts) % 2:]
    return parts[0]


def _loop(n, body, init):
    if isinstance(n, int):
        carry = init
        for c in range(n):
            carry = body(c, carry)
        return carry
    return lax.fori_loop(0, n, body, init)


def _dsa_core(q, qi, misc_t, far_ki, far_k, far_vt, nf, far_adm, near_ki, near_k, near_vt, near_adm,
              bias_near_ref, out_ref, scratch, *, tq, kw, nw, topk):
    fkeys_ref, nkeys_ref, fhalf_ref, nhalf_ref, m_ref, acc_ref, ss_ref = scratch
    rep = N_HEADS_C // N_KV_C
    groups = range(N_KV_C)
    sls = [slice(g * HD_C, (g + 1) * HD_C) for g in groups]
    qi_s = (qi * (D_IDX ** -0.5)).astype(BF16)
    qi_all = jnp.concatenate([qi_s[:, h * D_IDX:(h + 1) * D_IDX] for h in range(N_IDX_HEADS)], axis=0)
    wi_t = misc_t[MISC_WIC:MISC_WIC + N_IDX_HEADS, :] * (N_IDX_HEADS ** -0.5)
    qs = q * (HD_C ** -0.5 * LOG2E)
    q_g = [jnp.concatenate([qs[:, (g * rep + r) * HD_C:(g * rep + r + 1) * HD_C] for r in range(rep)],
                           axis=0).astype(BF16) for g in groups]
    kf = float(topk)

    def keys_of(ki_blk, adm):
        d = _dot_nt(ki_blk, qi_all)
        acc = None
        for h in range(N_IDX_HEADS):
            t = wi_t[h:h + 1, :] * jnp.maximum(d[:, h * tq:(h + 1) * tq], 0.0)
            acc = t if acc is None else acc + t
        bits = lax.bitcast_convert_type(acc, I32)
        key = bits ^ ((bits >> 31) & 0x7FFFFFFF)
        key = jnp.where(acc == 0.0, 0, key)
        if adm is not None:
            key = jnp.where(adm, key, INT_MIN)
        return key

    def fill(c, carry):
        key = keys_of(far_ki(c), None if far_adm is None else far_adm(c))
        fkeys_ref[c] = key
        fhalf_ref[c] = (key >> 16).astype(I16)
        return carry

    _loop(nf, fill, 0)
    key = keys_of(near_ki, near_adm)
    nkeys_ref[...] = key
    nhalf_ref[...] = (key >> 16).astype(I16)

    def count(thr, strict):
        def cmp(x):
            hit = (x > thr) if strict else (x >= thr)
            return _fold_rows(jnp.where(hit, 1.0, 0.0), jnp.add)

        part = _loop(nf, lambda c, a: a + cmp(fkeys_ref[c]), jnp.zeros((SUBLANES, tq), F32))
        part = part + cmp(nkeys_ref[...])
        return jnp.sum(part, axis=0, keepdims=True)

    def count_half(trial, strict=False):
        t16 = trial.astype(I16)

        def cmp(x):
            hit = (x > t16) if strict else (x >= t16)
            return _fold_rows(jnp.where(hit, jnp.int16(1), jnp.int16(0)), jnp.add)

        part = _loop(nf, lambda c, a: a + cmp(fhalf_ref[c]), jnp.zeros((2 * SUBLANES, tq), I16))
        part = part + cmp(nhalf_ref[...])
        return jnp.sum(part.astype(I32), axis=0, keepdims=True)

    def search_half(base):
        def step(t, cand):
            trial = cand + jnp.left_shift(jnp.int32(1), 15 - t)
            return jnp.where(base + count_half(trial) >= topk, trial, cand)

        return lax.fori_loop(0, 16, step, jnp.full((1, tq), -HALF, I32))

    hi = search_half(0)
    above = count_half(hi, strict=True)

    def low_half(key):
        return jnp.where((key >> 16) == hi, (key & 0xFFFF) - HALF, -HALF).astype(I16)

    def refill(c, carry):
        fhalf_ref[c] = low_half(fkeys_ref[c])
        return carry

    _loop(nf, refill, 0)
    nhalf_ref[...] = low_half(nkeys_ref[...])
    lo = search_half(above)
    thr = jnp.left_shift(hi, 16) + (lo + HALF)

    need = kf - count(thr, True)
    tie_cols = (count(thr, False) > kf) & (thr > INT_MIN)
    has_tie = jnp.max(jnp.where(tie_cols, 1.0, 0.0)) > 0.0

    @pl.when(has_tie)
    def _():
        def lower(n):
            a = lax.broadcasted_iota(I32, (n, n), 0)
            b = lax.broadcasted_iota(I32, (n, n), 1)
            return jnp.where(b < a, 1.0, 0.0).astype(BF16)

        def demote(keys, run, lt):
            eq = (keys == thr) & (keys > INT_MIN)
            eqf = jnp.where(eq, 1.0, 0.0)
            before = _dot(lt, eqf.astype(BF16)) + run
            keys = jnp.where(eq & (before >= need), INT_MIN, keys)
            return keys, run + jnp.sum(eqf, axis=0, keepdims=True)

        lt_far = lower(kw)

        def step(c, run):
            keys, run = demote(fkeys_ref[c], run, lt_far)
            fkeys_ref[c] = keys
            return run

        run = _loop(nf, step, jnp.zeros((1, tq), F32))
        keys, _ = demote(nkeys_ref[...], run, lower(nw))
        nkeys_ref[...] = keys

    thr_sel = jnp.maximum(thr, INT_MIN + 1)
    m_ref[...] = jnp.full(m_ref.shape, MASKED, F32)
    acc_ref[...] = jnp.zeros(acc_ref.shape, F32)

    def logits(k_blk):
        return [_dot_nt(k_blk[:, sls[g]], q_g[g]) for g in groups]

    def attend(ss, keys, vt_blk, bias_ref, n):
        selb = jnp.where(keys >= thr_sel, 0.0, MASKED)
        selb = jnp.concatenate([selb] * rep, axis=1)
        ones = jnp.ones((ONES_ROWS, n), BF16)
        ps = []
        for g in groups:
            s = ss[g] + selb
            if bias_ref is not None:
                s = s + bias_ref[g]
            m_old = m_ref[g]
            m_new = jnp.maximum(m_old, jnp.max(_fold_rows(s, jnp.maximum), axis=0, keepdims=True))
            ps.append(jnp.exp2(s - m_new).astype(BF16))
            acc_ref[g] = jnp.exp2(m_old - m_new) * acc_ref[g]
            m_ref[g] = m_new
        for g in groups:
            acc_ref[g] += _dot(jnp.concatenate([vt_blk[sls[g], :], ones], axis=0), ps[g])

    last = fkeys_ref.shape[0] - 1

    def put_logits(slot, c):
        for g, s in enumerate(logits(far_k(c))):
            ss_ref[slot, g] = s

    def far_body(c, cur):
        put_logits(1 - cur, min(c + 1, last) if isinstance(c, int) else jnp.minimum(c + 1, last))
        attend([ss_ref[cur, g] for g in groups], fkeys_ref[c], far_vt(c), None, kw)

    def far_step(c, carry):
        if isinstance(c, int):
            far_body(c, c % 2)
        else:
            for cur in range(2):
                pl.when(c % 2 == cur)(functools.partial(far_body, c, cur))
        return carry

    put_logits(0, 0)
    _loop(nf, far_step, 0)
    attend(logits(near_k), nkeys_ref[...], near_vt, bias_near_ref, nw)

    for g in groups:
        a = acc_ref[g]
        o = a[:HD_C] / a[HD_C:HD_C + 1]
        for r in range(rep):
            hh = g * rep + r
            out_ref[:, hh * HD_C:(hh + 1) * HD_C] = o[:, r * tq:(r + 1) * tq].T.astype(BF16)


def _dsa_scratch(nf, tq, kw, nw):
    rep = N_HEADS_C // N_KV_C
    return [pltpu.VMEM((nf, kw, tq), I32),
            pltpu.VMEM((nw, tq), I32),
            pltpu.VMEM((nf, kw, tq), I16),
            pltpu.VMEM((nw, tq), I16),
            pltpu.VMEM((N_KV_C, 1, rep * tq), F32),
            pltpu.VMEM((N_KV_C, HD_C + ONES_ROWS, rep * tq), F32),
            pltpu.VMEM((2, N_KV_C, kw, rep * tq), F32)]


def _dsa_prompt_kernel(q_ref, qi_ref, misc_ref, k_ref, vt_far_ref, vt_near_ref, ki_ref, bias_near_ref,
                       out_ref, *scratch, tq, topk):
    i = pl.program_id(0)
    kw = FAR_KW
    nw = 2 * tq
    far_limit = tq * (i - 1)
    nf = (i + 2) // (kw // tq)

    def rows(c):
        return pl.ds(pl.multiple_of(c * kw, kw), kw)

    def far_adm(c):
        return (c * kw + lax.broadcasted_iota(I32, (kw, tq), 0)) < far_limit

    lb = jnp.maximum(i - 1, 0)
    left = pl.ds(pl.multiple_of(lb * tq, tq), tq)
    right = pl.ds(pl.multiple_of(i * tq, tq), tq)
    near_ki = jnp.concatenate([ki_ref[left, :], ki_ref[right, :]], axis=0)
    near_k = jnp.concatenate([k_ref[left, :], k_ref[right, :]], axis=0)
    near_vt = jnp.concatenate([vt_near_ref[lb], vt_near_ref[i]], axis=1)
    key = lax.broadcasted_iota(I32, (nw, tq), 0)
    qry = lax.broadcasted_iota(I32, (nw, tq), 1)
    first_key = jnp.where(i > 0, 0, tq)
    near_adm = (key >= first_key) & (((key - tq) >> 6) <= (qry >> 6))

    _dsa_core(q_ref[...], qi_ref[...], misc_ref[...].T,
              lambda c: ki_ref[rows(c), :], lambda c: k_ref[rows(c), :], lambda c: vt_far_ref[c],
              nf, far_adm, near_ki, near_k, near_vt, near_adm,
              bias_near_ref, out_ref, scratch,
              tq=tq, kw=kw, nw=nw, topk=topk)


def _dsa_prompt(h, kb, vt_far, vt_near, kib, bias_near, t):
    tq = NEAR_SPAN
    topk = min(TOPK_MAX, t // 4)
    return pl.pallas_call(
        functools.partial(_dsa_prompt_kernel, tq=tq, topk=topk),
        grid=(t // tq,),
        in_specs=[pl.BlockSpec((tq, WIDTH_C), lambda i: (i, C_QC // WIDTH_C)),
                  pl.BlockSpec((tq, QI_C), lambda i: (i, C_QIC // QI_C)),
                  pl.BlockSpec((tq, LANES), lambda i: (i, C_MISC // LANES)),
                  pl.BlockSpec((t, KV_C), lambda i: (0, 0)),
                  pl.BlockSpec(vt_far.shape, lambda i: (0, 0, 0)),
                  pl.BlockSpec(vt_near.shape, lambda i: (0, 0, 0)),
                  pl.BlockSpec((t, D_IDX), lambda i: (0, 0)),
                  pl.BlockSpec(bias_near.shape, lambda i: (0, 0, 0))],
        out_specs=pl.BlockSpec((tq, WIDTH_C), lambda i: (i, 0)),
        out_shape=jax.ShapeDtypeStruct((t, WIDTH_C), BF16),
        scratch_shapes=_dsa_scratch(t // FAR_KW, tq, FAR_KW, 2 * tq),
        compiler_params=_cparams(("arbitrary",)),
        name="dsa_prompt",
    )(h, h, h, kb, vt_far, vt_near, kib, bias_near)


def _dsa_sample_kernel(q_ref, qi_ref, misc_ref, kvn_ref, pk_ref, pv_ref, pki_ref, bias_near_ref,
                       out_ref, *scratch, kw, nf, topk):
    tq = CHUNK
    far_len = kw * nf
    misc = misc_ref[...]
    kvn = kvn_ref[...]

    def group_rows(ref, start, n, g):
        return ref[0, pl.ds(2 * start + g, n, stride=2), :]

    def past_k(start, n):
        return jnp.concatenate([group_rows(pk_ref, start, n, g) for g in range(N_KV_C)], axis=1)

    def past_vt(start, n):
        return jnp.concatenate([group_rows(pv_ref, start, n, g).T for g in range(N_KV_C)], axis=0)

    near_ki = jnp.concatenate([pki_ref[0, pl.ds(far_len, NEAR_SPAN), :], misc[:, :D_IDX]], axis=0).astype(BF16)
    near_k = jnp.concatenate([past_k(far_len, NEAR_SPAN), kvn[:, :KV_C]], axis=0).astype(BF16)
    near_vt = jnp.concatenate([past_vt(far_len, NEAR_SPAN), kvn[:, KV_C:].T], axis=1).astype(BF16)

    _dsa_core(q_ref[...], qi_ref[...], misc.T,
              lambda c: pki_ref[0, pl.ds(c * kw, kw), :].astype(BF16),
              lambda c: past_k(c * kw, kw).astype(BF16), lambda c: past_vt(c * kw, kw).astype(BF16),
              nf, None, near_ki, near_k, near_vt, None,
              bias_near_ref, out_ref, scratch,
              tq=tq, kw=kw, nw=NEAR_SPAN + CHUNK, topk=topk)


def _dsa_sample(h, pk, pv, pki, bias_near, t):
    nb, p = pki.shape[0], pki.shape[1]
    tq = CHUNK
    far_len = p - NEAR_SPAN
    kw = _pick(far_len, (640, 512, 384, 256, 128))
    nf = far_len // kw
    nw = NEAR_SPAN + CHUNK
    topk = min(TOPK_MAX, (p + CHUNK) // 4)
    row0 = t // tq
    return pl.pallas_call(
        functools.partial(_dsa_sample_kernel, kw=kw, nf=nf, topk=topk),
        grid=(nb,),
        in_specs=[pl.BlockSpec((tq, WIDTH_C), lambda b: (row0 + b, C_QC // WIDTH_C)),
                  pl.BlockSpec((tq, QI_C), lambda b: (row0 + b, C_QIC // QI_C)),
                  pl.BlockSpec((tq, LANES), lambda b: (row0 + b, C_MISC // LANES)),
                  pl.BlockSpec((tq, 2 * KV_C), lambda b: (row0 + b, C_KC // (2 * KV_C))),
                  pl.BlockSpec((1, N_KV_C * p, HD_C), lambda b: (b, 0, 0)),
                  pl.BlockSpec((1, N_KV_C * p, HD_C), lambda b: (b, 0, 0)),
                  pl.BlockSpec((1, p, D_IDX), lambda b: (b, 0, 0)),
                  pl.BlockSpec(bias_near.shape, lambda b: (0, 0, 0))],
        out_specs=pl.BlockSpec((tq, WIDTH_C), lambda b: (b, 0)),
        out_shape=jax.ShapeDtypeStruct((nb * tq, WIDTH_C), BF16),
        scratch_shapes=_dsa_scratch(nf, tq, kw, nw),
        compiler_params=_cparams(("arbitrary",)),
        name="dsa_sample",
    )(h, h, h, h, pk, pv, pki, bias_near)


def _outproj_kernel(x_ref, ya_ref, yb_ref, yc_ref, wa_ref, wb_ref, wc_ref, o_ref):
    o_ref[...] = (x_ref[...] + _dot(ya_ref[...], wa_ref[...]) + _dot(yb_ref[...], wb_ref[...])
                  + _dot(yc_ref[...], wc_ref[...]))


def _outproj(x, ya, yb, yc, wa, wb, wc):
    m = x.shape[0]
    tm = _pick(m, (1024, 512, 256, 128))
    tn = 512
    return pl.pallas_call(
        _outproj_kernel,
        grid=(m // tm, D_MODEL // tn),
        in_specs=[pl.BlockSpec((tm, tn), lambda i, j: (i, j)),
                  pl.BlockSpec((tm, WIDTH_A), lambda i, j: (i, 0)),
                  pl.BlockSpec((tm, WIDTH_B), lambda i, j: (i, 0)),
                  pl.BlockSpec((tm, WIDTH_C), lambda i, j: (i, 0)),
                  pl.BlockSpec((WIDTH_A, tn), lambda i, j: (0, j)),
                  pl.BlockSpec((WIDTH_B, tn), lambda i, j: (0, j)),
                  pl.BlockSpec((WIDTH_C, tn), lambda i, j: (0, j))],
        out_specs=pl.BlockSpec((tm, tn), lambda i, j: (i, j)),
        out_shape=jax.ShapeDtypeStruct((m, D_MODEL), F32),
        compiler_params=_cparams(("arbitrary", "arbitrary")),
        name="outproj",
    )(x, ya, yb, yc, wa, wb, wc)


def _ffn_kernel(x_ref, g_ref, wu_ref, wd_ref, o_ref, xn_ref, acc_ref):
    f = pl.program_id(1)

    @pl.when(f == 0)
    def _():
        xn_ref[...] = (_rms(x_ref[...]) * g_ref[...]).astype(BF16)
        acc_ref[...] = jnp.zeros(acc_ref.shape, F32)

    a = jnp.maximum(_dot(xn_ref[...], wu_ref[...]), 0.0)
    acc_ref[...] += _dot((a * a).astype(BF16), wd_ref[...])

    @pl.when(f == pl.num_programs(1) - 1)
    def _():
        o_ref[...] = x_ref[...] + acc_ref[...]


def _ffn(x, g, wu, wd):
    m = x.shape[0]
    tm = _pick(m, (512, 256, 128))
    tf = 512
    return pl.pallas_call(
        _ffn_kernel,
        grid=(m // tm, D_FF // tf),
        in_specs=[pl.BlockSpec((tm, D_MODEL), lambda i, f: (i, 0)),
                  pl.BlockSpec((1, D_MODEL), lambda i, f: (0, 0)),
                  pl.BlockSpec((D_MODEL, tf), lambda i, f: (0, f)),
                  pl.BlockSpec((tf, D_MODEL), lambda i, f: (f, 0))],
        out_specs=pl.BlockSpec((tm, D_MODEL), lambda i, f: (i, 0)),
        out_shape=jax.ShapeDtypeStruct((m, D_MODEL), F32),
        scratch_shapes=[pltpu.VMEM((tm, D_MODEL), BF16), pltpu.VMEM((tm, D_MODEL), F32)],
        compiler_params=_cparams(("arbitrary", "arbitrary")),
        name="ffn",
    )(x, g, wu, wd)


def _norm_kernel(x_ref, g_ref, o_ref):
    o_ref[...] = _rms(x_ref[...]) * g_ref[...]


def _final_norm(x, g):
    m = x.shape[0]
    tm = _pick(m, (1024, 512, 256, 128))
    return pl.pallas_call(
        _norm_kernel,
        grid=(m // tm,),
        in_specs=[pl.BlockSpec((tm, D_MODEL), lambda i: (i, 0)),
                  pl.BlockSpec((1, D_MODEL), lambda i: (0, 0))],
        out_specs=pl.BlockSpec((tm, D_MODEL), lambda i: (i, 0)),
        out_shape=jax.ShapeDtypeStruct((m, D_MODEL), F32),
        compiler_params=_cparams(("arbitrary",)),
        name="final_norm",
    )(x, g)


def _t5_bucket(rel):
    half = N_BUCKETS // 2
    max_exact = half // 2
    n = jnp.abs(rel)
    nf = jnp.maximum(n, 1).astype(F32)
    large = max_exact + (jnp.log(nf / max_exact) / jnp.log(MAX_DISTANCE / max_exact)
                         * (half - max_exact)).astype(I32)
    large = jnp.minimum(large, half - 1)
    return jnp.where(rel > 0, half, 0) + jnp.where(n < max_exact, n, large)


def _bias_table(rel_bias, tq, nw):
    j = jnp.arange(nw, dtype=I32)[:, None]
    t = jnp.arange(tq, dtype=I32)[None, :]
    near = rel_bias[_t5_bucket(j - NEAR_SPAN - t)]
    far = rel_bias[_t5_bucket(jnp.int32(-(NEAR_SPAN + 1)))]
    rep = N_HEADS_C // N_KV_C
    near = ((near - far) * LOG2E).astype(F32).reshape(nw, tq, N_KV_C, rep)
    return jnp.transpose(near, (2, 0, 3, 1)).reshape(N_KV_C, nw, rep * tq)


def _permute_w_in(w):
    sizes = (WIDTH_A, WIDTH_A, QK_B, QK_B, WIDTH_B, GATE_RANK_B, WIDTH_B, WIDTH_C, KV_C, KV_C, QI_C,
             D_IDX, N_IDX_HEADS)
    parts, off = [], 0
    for n in sizes:
        parts.append(w[..., off:off + n])
        off += n
    ua, va, qb, kb, vb, lrb, og, qc, kc, vc, qic, kic, wic = parts
    pad = jnp.zeros(w.shape[:-1] + (H_WIDTH - (C_MISC + D_IDX + GATE_RANK_B + N_IDX_HEADS),), w.dtype)
    return jnp.concatenate([qb, kb, vb, og, qc, kc, vc, qic, ua, va, kic, lrb, wic, pad], axis=-1).astype(BF16)


def kernel(x_prompt, x_sample, cache_c_k, cache_c_v, cache_c_kidx, state_b_s, norm_mix, w_in, norm_a_v,
           w_s_a, b_s_a, w_gate_b, b_gate_b, norm_b_o, rel_bias, w_o, norm_ffn, w_up, w_down, norm_final):
    depth = w_in.shape[0]
    nbp, t, _ = x_prompt.shape
    nbs, ts, _ = x_sample.shape
    p = cache_c_k.shape[2]
    assert nbp == 1 and ts == CHUNK and t % FAR_KW == 0 and (nbs * ts) % GROUP_A == 0
    assert p % NEAR_SPAN == 0 and p >= 2 * NEAR_SPAN

    x = jnp.concatenate([x_prompt[0], x_sample.reshape(nbs * ts, D_MODEL)], axis=0)
    w_in_p = _permute_w_in(w_in)
    w_o_b = w_o.astype(BF16)
    w_up_b = w_up.astype(BF16)
    w_down_b = w_down.astype(BF16)
    w_gate = w_gate_b.astype(BF16)
    half = GROUP_A // 2
    w_s2 = jnp.stack([w_s_a, jnp.tile(w_s_a[:, :, :half, :half], (1, 1, 2, 2))], axis=1)
    b_s2 = jnp.stack([b_s_a, jnp.tile(b_s_a[:, :, :half], (1, 1, 2))], axis=1)
    b_s2 = jnp.swapaxes(b_s2, 2, 3)
    s_all = jnp.concatenate([jnp.zeros((depth, 1) + state_b_s.shape[2:], F32), state_b_s], axis=1)
    bias_near = _bias_table(rel_bias, NEAR_SPAN, 2 * NEAR_SPAN)
    bias_near_s = _bias_table(rel_bias, CHUNK, NEAR_SPAN + CHUNK)
    pk_all = cache_c_k.reshape(depth, nbs, N_KV_C * p, HD_C)
    pv_all = cache_c_v.reshape(depth, nbs, N_KV_C * p, HD_C)

    outs = [[] for _ in range(9)]
    for l in range(depth):
        h = _inproj(x, norm_mix[l][None], w_in_p[l])
        hp, hs = h[:t], h[t:]
        ya, vn = _gmlp(h, norm_a_v[l][None], w_s2[l], b_s2[l], t)
        yb, s_out = _gla(h, w_gate[l], b_gate_b[l][None], norm_b_o[l][None], s_all[l], t)
        kb = hp[:, C_KC:C_KC + KV_C].astype(BF16)
        vt = hp[:, C_VC:C_VC + KV_C].astype(BF16).T
        vt_far = jnp.swapaxes(vt.reshape(KV_C, t // FAR_KW, FAR_KW), 0, 1)
        vt_near = jnp.swapaxes(vt.reshape(KV_C, t // NEAR_SPAN, NEAR_SPAN), 0, 1)
        kib = hp[:, C_MISC:C_MISC + D_IDX].astype(BF16)
        yc_p = _dsa_prompt(h, kb, vt_far, vt_near, kib, bias_near, t)
        yc_s = _dsa_sample(h, pk_all[l], pv_all[l], cache_c_kidx[l], bias_near_s, t)
        yc = jnp.concatenate([yc_p, yc_s], axis=0)
        wo = w_o_b[l]
        x = _outproj(x, ya, yb, yc, wo[:WIDTH_A], wo[WIDTH_A:WIDTH_A + WIDTH_B], wo[WIDTH_A + WIDTH_B:])
        x = _ffn(x, norm_ffn[l][None], w_up_b[l], w_down_b[l])

        outs[0].append(hp[:, C_KC:C_KC + KV_C].reshape(1, t, N_KV_C, HD_C))
        outs[1].append(hp[:, C_VC:C_VC + KV_C].reshape(1, t, N_KV_C, HD_C))
        outs[2].append(hp[:, C_MISC:C_MISC + D_IDX].reshape(1, t, D_IDX))
        outs[3].append(s_out[:1])
        outs[4].append(hs[:, C_KC:C_KC + KV_C].reshape(nbs, ts, N_KV_C, HD_C))
        outs[5].append(hs[:, C_VC:C_VC + KV_C].reshape(nbs, ts, N_KV_C, HD_C))
        outs[6].append(hs[:, C_MISC:C_MISC + D_IDX].reshape(nbs, ts, D_IDX))
        outs[7].append(s_out[1:])
        outs[8].append(vn.reshape(nbs, ts, WIDTH_A))

    y = _final_norm(x, norm_final[None])
    return (y[:t][None], y[t:].reshape(nbs, ts, D_MODEL)) + tuple(jnp.stack(o) for o in outs)
```

```python
import functools
import math

import jax
import jax.numpy as jnp
from jax import lax
from jax.experimental import pallas as pl
from jax.experimental.pallas import tpu as pltpu

BF16 = jnp.bfloat16
F32 = jnp.float32
I32 = jnp.int32

D_MODEL = 2048
EPS = 1e-6
CHUNK = 64
GROUP_A = 128
N_GROUPS_A = 4
WIDTH_A = 512
N_HEADS_B = 6
DK_B = 64
DV_B = 128
GATE_RANK_B = 16
GATE_TEMP_B = 16.0
WIDTH_B = 768
N_HEADS_C = 6
N_KV_C = 2
HD_C = 128
N_IDX_HEADS = 8
D_IDX = 64
TOPK_MAX = 256
WIDTH_C = 768
N_BUCKETS = 32
MAX_DISTANCE = 128
D_FF = 4 * D_MODEL
QK_B = N_HEADS_B * DK_B
KV_C = N_KV_C * HD_C
QI_C = N_IDX_HEADS * D_IDX

LANES = 128
SUBLANES = 8
ONES_ROWS = 16
VMEM_LIMIT = 56 * 1024 * 1024

C_QB = 0
C_KB = 384
C_VB = 768
C_OG = 1536
C_QC = 2304
C_KC = 3072
C_VC = 3328
C_QIC = 3584
C_UA = 4096
C_VA = 4608
C_MISC = 5120
MISC_LRB = D_IDX
MISC_WIC = D_IDX + GATE_RANK_B
H_WIDTH = 5376

INT_MIN = -2147483648
INT_MAX = 2147483647
SNAP_EVERY = 8
BISECT_AFTER = 40
MAX_SEARCH_STEPS = 80
MASKED = -1e30
LOG2E = 1.4426950408889634
NEAR_SPAN = 128
FAR_KW = 512


def _pick(n, cands):
    for c in cands:
        if n % c == 0:
            return c
    raise ValueError(f"no tile for {n}")


def _cparams(sem):
    return pltpu.CompilerParams(dimension_semantics=sem, vmem_limit_bytes=VMEM_LIMIT)


def _rms(x):
    return x * lax.rsqrt(jnp.mean(x * x, axis=-1, keepdims=True) + EPS)


def _dot(a, b):
    return jnp.dot(a, b, preferred_element_type=F32)


def _dot_nt(a, b):
    return lax.dot_general(a, b, (((1,), (1,)), ((), ())), preferred_element_type=F32)


def _dot_tn(a, b):
    return lax.dot_general(a, b, (((0,), (0,)), ((), ())), preferred_element_type=F32)


def _inproj_kernel(x_ref, g_ref, w_ref, o_ref, xn_ref):
    @pl.when(pl.program_id(1) == 0)
    def _():
        xn_ref[...] = (_rms(x_ref[...]) * g_ref[...]).astype(BF16)

    o_ref[...] = _dot(xn_ref[...], w_ref[...])


def _inproj(x, g, w, l):
    m = x.shape[0]
    tm = _pick(m, (1024, 512, 256, 128))
    tn = 768
    return pl.pallas_call(
        _inproj_kernel,
        grid=(m // tm, H_WIDTH // tn),
        in_specs=[pl.BlockSpec((tm, D_MODEL), lambda i, j: (i, 0)),
                  pl.BlockSpec((None, 1, D_MODEL), lambda i, j: (l, 0, 0)),
                  pl.BlockSpec((None, D_MODEL, tn), lambda i, j: (l, 0, j))],
        out_specs=pl.BlockSpec((tm, tn), lambda i, j: (i, j)),
        out_shape=jax.ShapeDtypeStruct((m, H_WIDTH), F32),
        scratch_shapes=[pltpu.VMEM((tm, D_MODEL), BF16)],
        compiler_params=_cparams(("arbitrary", "arbitrary")),
        name="inproj",
    )(x, g, w)


def _gmlp_kernel(h_ref, gv_ref, w_ref, b_ref, ya_ref, vn_ref, *, n_prompt_blocks):
    i = pl.program_id(0)
    hv = h_ref[...]
    u = jax.nn.gelu(hv[:, :WIDTH_A])
    v = jax.nn.gelu(hv[:, WIDTH_A:])
    r = lax.broadcasted_iota(I32, (GROUP_A, GROUP_A), 0)
    c = lax.broadcasted_iota(I32, (GROUP_A, GROUP_A), 1)
    samp = (i >= n_prompt_blocks).astype(I32)
    keep = (c <= r) & (((r >> 6) * samp) == ((c >> 6) * samp))
    for g in range(N_GROUPS_A):
        sl = slice(g * GROUP_A, (g + 1) * GROUP_A)
        vn = _rms(v[:, sl]) * gv_ref[:, sl]
        vn_ref[:, sl] = vn
        wm = jnp.where(keep, w_ref[0, g], 0.0).astype(BF16)
        z = _dot(wm, vn.astype(BF16)) + b_ref[0][:, g:g + 1]
        ya_ref[:, sl] = (u[:, sl] * z).astype(BF16)


def _gmlp(h, gv, w2, b2, n_prompt_rows, l):
    m = h.shape[0]
    npb = n_prompt_rows // GROUP_A
    nsb = (m - n_prompt_rows) // GROUP_A
    return pl.pallas_call(
        functools.partial(_gmlp_kernel, n_prompt_blocks=npb),
        grid=(npb + nsb,),
        in_specs=[pl.BlockSpec((GROUP_A, 2 * WIDTH_A), lambda i: (i, C_UA // (2 * WIDTH_A))),
                  pl.BlockSpec((None, 1, WIDTH_A), lambda i: (l, 0, 0)),
                  pl.BlockSpec((None, 1, N_GROUPS_A, GROUP_A, GROUP_A),
                               lambda i: (l, jnp.where(i >= npb, 1, 0), 0, 0, 0)),
                  pl.BlockSpec((None, 1, GROUP_A, N_GROUPS_A),
                               lambda i: (l, jnp.where(i >= npb, 1, 0), 0, 0))],
        out_specs=[pl.BlockSpec((GROUP_A, WIDTH_A), lambda i: (i, 0)),
                   pl.BlockSpec((GROUP_A, WIDTH_A), lambda i: (jnp.maximum(i - npb, 0), 0))],
        out_shape=[jax.ShapeDtypeStruct((m, WIDTH_A), BF16),
                   jax.ShapeDtypeStruct((nsb * GROUP_A, WIDTH_A), F32)],
        compiler_params=_cparams(("arbitrary",)),
        name="gmlp",
    )(h, gv, w2, b2)


def _gla_kernel(q_ref, k_ref, v_ref, og_ref, misc_ref, wg_ref, bg_ref, gbo_ref, s0_ref,
                yb_ref, sout_ref, st_ref, *, n_prompt_chunks):
    c = pl.program_id(0)

    @pl.when((c == 0) | (c >= n_prompt_chunks))
    def _():
        for h in range(N_HEADS_B):
            st_ref[h] = s0_ref[0, h].T

    lrb = misc_ref[:, MISC_LRB:MISC_LRB + GATE_RANK_B].astype(BF16)
    x = _dot(lrb, wg_ref[...]) + bg_ref[...]
    g = (jnp.minimum(x, 0.0) - jnp.log1p(jnp.exp(-jnp.abs(x)))) * (1.0 / GATE_TEMP_B)
    rr = lax.broadcasted_iota(I32, (CHUNK, CHUNK), 0)
    cc = lax.broadcasted_iota(I32, (CHUNK, CHUNK), 1)
    tril = cc <= rr
    tri = jnp.where(tril, 1.0, 0.0).astype(BF16)
    g1 = g.astype(BF16)
    r1 = g - g1.astype(F32)
    g2 = r1.astype(BF16)
    g3 = (r1 - g2.astype(F32)).astype(BF16)
    b = _dot(tri, g1) + _dot(tri, g2) + _dot(tri, g3)
    b_last = b[CHUNK - 1:CHUNK, :]
    b_mid = b[CHUNK // 2 - 1:CHUNK // 2, :]
    qs = q_ref[...] * (DK_B ** -0.5)
    kk = k_ref[...]
    q_inter = (qs * jnp.exp(b)).astype(BF16)
    q_intra = (qs * jnp.exp(b - b_mid)).astype(BF16)
    k_intra = (kk * jnp.exp(b_mid - b)).astype(BF16)
    k_state = (kk * jnp.exp(b_last - b)).astype(BF16)
    dec = jnp.exp(b_last)
    for h in range(N_HEADS_B):
        sk = slice(h * DK_B, (h + 1) * DK_B)
        sv = slice(h * DV_B, (h + 1) * DV_B)
        st = st_ref[h]
        vh = v_ref[:, sv].astype(BF16)
        o = _dot_nt(q_inter[:, sk], st.astype(BF16))
        att = jnp.where(tril, _dot_nt(q_intra[:, sk], k_intra[:, sk]), 0.0)
        o = o + _dot(att.astype(BF16), vh)
        st_new = st * dec[:, sk] + _dot_tn(vh, k_state[:, sk])
        st_ref[h] = st_new
        sout_ref[0, h] = st_new.T
        on = _rms(o) * gbo_ref[:, sv]
        yb_ref[:, sv] = (on * jax.nn.silu(og_ref[:, sv])).astype(BF16)


def _gla(h, wg, bg, gbo, s_all, n_prompt_rows, l):
    m = h.shape[0]
    npc = n_prompt_rows // CHUNK
    nch = m // CHUNK
    nseq = s_all.shape[1]

    def seq(c):
        return jnp.where(c < npc, 0, c - npc + 1)

    return pl.pallas_call(
        functools.partial(_gla_kernel, n_prompt_chunks=npc),
        grid=(nch,),
        in_specs=[pl.BlockSpec((CHUNK, QK_B), lambda c: (c, C_QB // QK_B)),
                  pl.BlockSpec((CHUNK, QK_B), lambda c: (c, C_KB // QK_B)),
                  pl.BlockSpec((CHUNK, WIDTH_B), lambda c: (c, C_VB // WIDTH_B)),
                  pl.BlockSpec((CHUNK, WIDTH_B), lambda c: (c, C_OG // WIDTH_B)),
                  pl.BlockSpec((CHUNK, LANES), lambda c: (c, C_MISC // LANES)),
                  pl.BlockSpec((None, GATE_RANK_B, QK_B), lambda c: (l, 0, 0)),
                  pl.BlockSpec((None, 1, QK_B), lambda c: (l, 0, 0)),
                  pl.BlockSpec((None, 1, WIDTH_B), lambda c: (l, 0, 0)),
                  pl.BlockSpec((None, 1, N_HEADS_B, DK_B, DV_B), lambda c: (l, seq(c), 0, 0, 0))],
        out_specs=[pl.BlockSpec((CHUNK, WIDTH_B), lambda c: (c, 0)),
                   pl.BlockSpec((1, N_HEADS_B, DK_B, DV_B), lambda c: (seq(c), 0, 0, 0))],
        out_shape=[jax.ShapeDtypeStruct((m, WIDTH_B), BF16),
                   jax.ShapeDtypeStruct((nseq, N_HEADS_B, DK_B, DV_B), F32)],
        scratch_shapes=[pltpu.VMEM((N_HEADS_B, DV_B, DK_B), F32)],
        compiler_params=_cparams(("arbitrary",)),
        name="gla",
    )(h, h, h, h, h, wg, bg, gbo, s_all)


def _fold_rows(x, op):
    tile = SUBLANES * (4 // x.dtype.itemsize)
    parts = [x[j * tile:(j + 1) * tile] for j in range(x.shape[0] // tile)]
    while len(parts) > 1:
        nxt = [op(parts[j], parts[j + 1]) for j in range(0, len(parts) - 1, 2)]
        parts = nxt + parts[len(parts) - len(parts) % 2:]
    return parts[0]


def _loop(n, body, init):
    if isinstance(n, int):
        carry = init
        for c in range(n):
            carry = body(c, carry)
        return carry
    return lax.fori_loop(0, n, body, init)


def _dsa_core(q, qi, misc_t, far_ki, far_k, far_vt, nf, far_adm, near_ki, near_k, near_vt, near_adm,
              bias_near_ref, out_ref, scratch, *, tq, kw, nw, topk):
    fkeys_ref, nkeys_ref, m_ref, acc_ref, ss_ref = scratch
    rep = N_HEADS_C // N_KV_C
    groups = range(N_KV_C)
    sls = [slice(g * HD_C, (g + 1) * HD_C) for g in groups]
    qi_s = (qi * (D_IDX ** -0.5)).astype(BF16)
    qi_all = jnp.concatenate([qi_s[:, h * D_IDX:(h + 1) * D_IDX] for h in range(N_IDX_HEADS)], axis=0)
    wi_t = misc_t[MISC_WIC:MISC_WIC + N_IDX_HEADS, :] * (N_IDX_HEADS ** -0.5)
    qs = q * (HD_C ** -0.5 * LOG2E)
    q_g = [jnp.concatenate([qs[:, (g * rep + r) * HD_C:(g * rep + r + 1) * HD_C] for r in range(rep)],
                           axis=0).astype(BF16) for g in groups]
    kf = float(topk)

    def keys_of(ki_blk, adm):
        d = _dot_nt(ki_blk, qi_all)
        acc = None
        for h in range(N_IDX_HEADS):
            t = wi_t[h:h + 1, :] * jnp.maximum(d[:, h * tq:(h + 1) * tq], 0.0)
            acc = t if acc is None else acc + t
        bits = lax.bitcast_convert_type(acc, I32)
        key = bits ^ ((bits >> 31) & 0x7FFFFFFF)
        key = jnp.where(acc == 0.0, 0, key)
        if adm is not None:
            key = jnp.where(adm, key, INT_MIN)
        return key

    def extremes(key, carry):
        kmax, kmin = carry
        kmax = jnp.maximum(kmax, _fold_rows(key, jnp.maximum))
        kmin = jnp.minimum(kmin, _fold_rows(jnp.where(key == INT_MIN, INT_MAX, key), jnp.minimum))
        return kmax, kmin

    def fill(c, carry):
        key = keys_of(far_ki(c), None if far_adm is None else far_adm(c))
        fkeys_ref[c] = key
        return extremes(key, carry)

    carry = _loop(nf, fill, (jnp.full((SUBLANES, tq), INT_MIN, I32), jnp.full((SUBLANES, tq), INT_MAX, I32)))
    key = keys_of(near_ki, near_adm)
    nkeys_ref[...] = key
    kmax, kmin = extremes(key, carry)
    kmax = jnp.max(kmax, axis=0, keepdims=True)
    kmin = jnp.min(kmin, axis=0, keepdims=True)

    def count(thr, strict):
        def cmp(x):
            hit = (x > thr) if strict else (x >= thr)
            return _fold_rows(jnp.where(hit, 1.0, 0.0), jnp.add)

        part = _loop(nf, lambda c, a: a + cmp(fkeys_ref[c]), jnp.zeros((SUBLANES, tq), F32))
        part = part + cmp(nkeys_ref[...])
        return jnp.sum(part, axis=0, keepdims=True)

    def bits_flip(x):
        return x ^ ((x >> 31) & 0x7FFFFFFF)

    def any_lane(flag):
        return jnp.max(jnp.where(flag, 1.0, 0.0))

    def snap(lo, hi):
        def pull(x, carry):
            kin, kax = carry
            kin = jnp.minimum(kin, _fold_rows(jnp.where(x >= lo, x, INT_MAX), jnp.minimum))
            kax = jnp.maximum(kax, _fold_rows(jnp.where(x < hi, x, INT_MIN), jnp.maximum))
            return kin, kax

        init = (jnp.full((SUBLANES, tq), INT_MAX, I32), jnp.full((SUBLANES, tq), INT_MIN, I32))
        kin, kax = pull(nkeys_ref[...], _loop(nf, lambda c, carry: pull(fkeys_ref[c], carry), init))
        return jnp.min(kin, axis=0, keepdims=True), jnp.max(kax, axis=0, keepdims=True)

    n_adm = count(jnp.full((1, tq), INT_MIN, I32), True)
    enough = n_adm >= kf
    log_k = math.log2(topk)

    def open_lanes(lo, hi, flo):
        return enough & (flo > kf) & (hi > lo + 1)

    def log_excess(c):
        return jnp.log2(jnp.maximum(c, 0.5)) - log_k

    def step(state):
        it, _, lo, hi, flo, fhi, glo, ghi, side = state
        active = open_lanes(lo, hi, flo)

        def pull_in(_):
            kin, kax = snap(lo, hi)
            return jnp.where(active, kin, lo), jnp.where(active, kax + 1, hi), flo, fhi, glo, ghi, side

        def probe(_):
            v_lo = lax.bitcast_convert_type(bits_flip(lo), F32)
            v_hi = lax.bitcast_convert_type(bits_flip(hi), F32)
            frac = jnp.minimum(jnp.maximum(glo / (glo - ghi), 0.02), 0.98)
            t_int = bits_flip(lax.bitcast_convert_type(v_lo + (v_hi - v_lo) * frac, I32))
            t_mid = (lo & hi) + ((lo ^ hi) >> 1)
            t = jnp.where(it >= BISECT_AFTER, t_mid, t_int)
            t = jnp.minimum(jnp.maximum(t, lo + 1), hi - 1)
            c = count(t, False)
            g = log_excess(c)
            up = active & (c >= kf)
            dn = active & (c < kf)
            ghi2 = jnp.where(up & (side > 0.0), ghi * 0.5, ghi)
            glo2 = jnp.where(dn & (side < 0.0), glo * 0.5, glo)
            return (jnp.where(up, t, lo), jnp.where(dn, t, hi), jnp.where(up, c, flo), jnp.where(dn, c, fhi),
                    jnp.where(up, g, glo2), jnp.where(dn, g, ghi2), jnp.where(up, 1.0, jnp.where(dn, -1.0, side)))

        is_snap = (it % SNAP_EVERY == SNAP_EVERY - 1) & (it < BISECT_AFTER)
        lo, hi, flo, fhi, glo, ghi, side = lax.cond(is_snap, pull_in, probe, 0)
        return it + 1, any_lane(open_lanes(lo, hi, flo)), lo, hi, flo, fhi, glo, ghi, side

    hi0 = kmax + 1
    zero = jnp.zeros((1, tq), F32)
    state = (jnp.int32(0), any_lane(open_lanes(kmin, hi0, n_adm)), kmin, hi0, n_adm, zero,
             log_excess(n_adm), log_excess(zero), zero)
    state = lax.while_loop(lambda st: (st[1] > 0.0) & (st[0] < MAX_SEARCH_STEPS), step, state)
    thr = jnp.where(enough, state[2], INT_MIN)
    has_tie = any_lane(enough & (state[4] > kf)) > 0.0

    @pl.when(has_tie)
    def _():
        need = kf - count(thr, True)

        def lower(n):
            a = lax.broadcasted_iota(I32, (n, n), 0)
            b = lax.broadcasted_iota(I32, (n, n), 1)
            return jnp.where(b < a, 1.0, 0.0).astype(BF16)

        def demote(keys, run, lt):
            eq = (keys == thr) & (keys > INT_MIN)
            eqf = jnp.where(eq, 1.0, 0.0)
            before = _dot(lt, eqf.astype(BF16)) + run
            keys = jnp.where(eq & (before >= need), INT_MIN, keys)
            return keys, run + jnp.sum(eqf, axis=0, keepdims=True)

        lt_far = lower(kw)

        def step(c, run):
            keys, run = demote(fkeys_ref[c], run, lt_far)
            fkeys_ref[c] = keys
            return run

        run = _loop(nf, step, jnp.zeros((1, tq), F32))
        keys, _ = demote(nkeys_ref[...], run, lower(nw))
        nkeys_ref[...] = keys

    thr_sel = jnp.maximum(thr, INT_MIN + 1)
    m_ref[...] = jnp.full(m_ref.shape, MASKED, F32)
    acc_ref[...] = jnp.zeros(acc_ref.shape, F32)

    def logits(k_blk):
        return [_dot_nt(k_blk[:, sls[g]], q_g[g]) for g in groups]

    def attend(ss, keys, vt_blk, bias_ref, n):
        selb = jnp.where(keys >= thr_sel, 0.0, MASKED)
        selb = jnp.concatenate([selb] * rep, axis=1)
        ones = jnp.ones((ONES_ROWS, n), BF16)
        ps = []
        for g in groups:
            s = ss[g] + selb
            if bias_ref is not None:
                s = s + bias_ref[g]
            m_old = m_ref[g]
            m_new = jnp.maximum(m_old, jnp.max(_fold_rows(s, jnp.maximum), axis=0, keepdims=True))
            ps.append(jnp.exp2(s - m_new).astype(BF16))
            acc_ref[g] = jnp.exp2(m_old - m_new) * acc_ref[g]
            m_ref[g] = m_new
        for g in groups:
            acc_ref[g] += _dot(jnp.concatenate([vt_blk[sls[g], :], ones], axis=0), ps[g])

    last = fkeys_ref.shape[0] - 1

    def put_logits(slot, c):
        for g, s in enumerate(logits(far_k(c))):
            ss_ref[slot, g] = s

    def far_body(c, cur):
        put_logits(1 - cur, min(c + 1, last) if isinstance(c, int) else jnp.minimum(c + 1, last))
        attend([ss_ref[cur, g] for g in groups], fkeys_ref[c], far_vt(c), None, kw)

    def far_step(c, carry):
        if isinstance(c, int):
            far_body(c, c % 2)
        else:
            for cur in range(2):
                pl.when(c % 2 == cur)(functools.partial(far_body, c, cur))
        return carry

    put_logits(0, 0)
    _loop(nf, far_step, 0)
    attend(logits(near_k), nkeys_ref[...], near_vt, bias_near_ref, nw)

    for g in groups:
        a = acc_ref[g]
        o = a[:HD_C] / a[HD_C:HD_C + 1]
        for r in range(rep):
            hh = g * rep + r
            out_ref[:, hh * HD_C:(hh + 1) * HD_C] = o[:, r * tq:(r + 1) * tq].T.astype(BF16)


def _dsa_scratch(nf, tq, kw, nw):
    rep = N_HEADS_C // N_KV_C
    return [pltpu.VMEM((nf, kw, tq), I32),
            pltpu.VMEM((nw, tq), I32),
            pltpu.VMEM((N_KV_C, 1, rep * tq), F32),
            pltpu.VMEM((N_KV_C, HD_C + ONES_ROWS, rep * tq), F32),
            pltpu.VMEM((2, N_KV_C, kw, rep * tq), F32)]


def _dsa_prompt_kernel(q_ref, qi_ref, misc_ref, k_ref, vt_far_ref, vt_near_ref, ki_ref, bias_near_ref,
                       out_ref, *scratch, tq, topk):
    i = pl.program_id(0)
    kw = FAR_KW
    nw = 2 * tq
    far_limit = tq * (i - 1)
    nf = (i + 2) // (kw // tq)

    def rows(c):
        return pl.ds(pl.multiple_of(c * kw, kw), kw)

    def far_adm(c):
        return (c * kw + lax.broadcasted_iota(I32, (kw, tq), 0)) < far_limit

    lb = jnp.maximum(i - 1, 0)
    left = pl.ds(pl.multiple_of(lb * tq, tq), tq)
    right = pl.ds(pl.multiple_of(i * tq, tq), tq)
    near_ki = jnp.concatenate([ki_ref[left, :], ki_ref[right, :]], axis=0)
    near_k = jnp.concatenate([k_ref[left, :], k_ref[right, :]], axis=0)
    near_vt = jnp.concatenate([vt_near_ref[lb], vt_near_ref[i]], axis=1)
    key = lax.broadcasted_iota(I32, (nw, tq), 0)
    qry = lax.broadcasted_iota(I32, (nw, tq), 1)
    first_key = jnp.where(i > 0, 0, tq)
    near_adm = (key >= first_key) & (((key - tq) >> 6) <= (qry >> 6))

    _dsa_core(q_ref[...], qi_ref[...], misc_ref[...].T,
              lambda c: ki_ref[rows(c), :], lambda c: k_ref[rows(c), :], lambda c: vt_far_ref[c],
              nf, far_adm, near_ki, near_k, near_vt, near_adm,
              bias_near_ref, out_ref, scratch,
              tq=tq, kw=kw, nw=nw, topk=topk)


def _dsa_prompt(h, kb, vt_far, vt_near, kib, bias_near, t):
    tq = NEAR_SPAN
    topk = min(TOPK_MAX, t // 4)
    return pl.pallas_call(
        functools.partial(_dsa_prompt_kernel, tq=tq, topk=topk),
        grid=(t // tq,),
        in_specs=[pl.BlockSpec((tq, WIDTH_C), lambda i: (i, C_QC // WIDTH_C)),
                  pl.BlockSpec((tq, QI_C), lambda i: (i, C_QIC // QI_C)),
                  pl.BlockSpec((tq, LANES), lambda i: (i, C_MISC // LANES)),
                  pl.BlockSpec((t, KV_C), lambda i: (0, 0)),
                  pl.BlockSpec(vt_far.shape, lambda i: (0, 0, 0)),
                  pl.BlockSpec(vt_near.shape, lambda i: (0, 0, 0)),
                  pl.BlockSpec((t, D_IDX), lambda i: (0, 0)),
                  pl.BlockSpec(bias_near.shape, lambda i: (0, 0, 0))],
        out_specs=pl.BlockSpec((tq, WIDTH_C), lambda i: (i, 0)),
        out_shape=jax.ShapeDtypeStruct((t, WIDTH_C), BF16),
        scratch_shapes=_dsa_scratch(t // FAR_KW, tq, FAR_KW, 2 * tq),
        compiler_params=_cparams(("arbitrary",)),
        name="dsa_prompt",
    )(h, h, h, kb, vt_far, vt_near, kib, bias_near)


def _dsa_sample_kernel(q_ref, qi_ref, misc_ref, kvn_ref, pk_ref, pv_ref, pki_ref, bias_near_ref,
                       out_ref, *scratch, kw, nf, topk):
    tq = CHUNK
    far_len = kw * nf
    misc = misc_ref[...]
    kvn = kvn_ref[...]

    def group_rows(ref, start, n, g):
        return ref[0, pl.ds(2 * start + g, n, stride=2), :]

    def past_k(start, n):
        return jnp.concatenate([group_rows(pk_ref, start, n, g) for g in range(N_KV_C)], axis=1)

    def past_vt(start, n):
        return jnp.concatenate([group_rows(pv_ref, start, n, g).T for g in range(N_KV_C)], axis=0)

    near_ki = jnp.concatenate([pki_ref[0, pl.ds(far_len, NEAR_SPAN), :], misc[:, :D_IDX]], axis=0).astype(BF16)
    near_k = jnp.concatenate([past_k(far_len, NEAR_SPAN), kvn[:, :KV_C]], axis=0).astype(BF16)
    near_vt = jnp.concatenate([past_vt(far_len, NEAR_SPAN), kvn[:, KV_C:].T], axis=1).astype(BF16)

    _dsa_core(q_ref[...], qi_ref[...], misc.T,
              lambda c: pki_ref[0, pl.ds(c * kw, kw), :].astype(BF16),
              lambda c: past_k(c * kw, kw).astype(BF16), lambda c: past_vt(c * kw, kw).astype(BF16),
              nf, None, near_ki, near_k, near_vt, None,
              bias_near_ref, out_ref, scratch,
              tq=tq, kw=kw, nw=NEAR_SPAN + CHUNK, topk=topk)


def _dsa_sample(h, pk, pv, pki, bias_near, t, l):
    nb, p = pki.shape[1], pki.shape[2]
    tq = CHUNK
    far_len = p - NEAR_SPAN
    kw = _pick(far_len, (640, 512, 384, 256, 128))
    nf = far_len // kw
    nw = NEAR_SPAN + CHUNK
    topk = min(TOPK_MAX, (p + CHUNK) // 4)
    row0 = t // tq
    return pl.pallas_call(
        functools.partial(_dsa_sample_kernel, kw=kw, nf=nf, topk=topk),
        grid=(nb,),
        in_specs=[pl.BlockSpec((tq, WIDTH_C), lambda b: (row0 + b, C_QC // WIDTH_C)),
                  pl.BlockSpec((tq, QI_C), lambda b: (row0 + b, C_QIC // QI_C)),
                  pl.BlockSpec((tq, LANES), lambda b: (row0 + b, C_MISC // LANES)),
                  pl.BlockSpec((tq, 2 * KV_C), lambda b: (row0 + b, C_KC // (2 * KV_C))),
                  pl.BlockSpec((None, 1, N_KV_C * p, HD_C), lambda b: (l, b, 0, 0)),
                  pl.BlockSpec((None, 1, N_KV_C * p, HD_C), lambda b: (l, b, 0, 0)),
                  pl.BlockSpec((None, 1, p, D_IDX), lambda b: (l, b, 0, 0)),
                  pl.BlockSpec(bias_near.shape, lambda b: (0, 0, 0))],
        out_specs=pl.BlockSpec((tq, WIDTH_C), lambda b: (b, 0)),
        out_shape=jax.ShapeDtypeStruct((nb * tq, WIDTH_C), BF16),
        scratch_shapes=_dsa_scratch(nf, tq, kw, nw),
        compiler_params=_cparams(("arbitrary",)),
        name="dsa_sample",
    )(h, h, h, h, pk, pv, pki, bias_near)


def _outproj_kernel(x_ref, ya_ref, yb_ref, yc_ref, wa_ref, wb_ref, wc_ref, o_ref):
    o_ref[...] = (x_ref[...] + _dot(ya_ref[...], wa_ref[...]) + _dot(yb_ref[...], wb_ref[...])
                  + _dot(yc_ref[...], wc_ref[...]))


def _outproj(x, ya, yb, yc, wa, wb, wc, l):
    m = x.shape[0]
    tm = _pick(m, (1024, 512, 256, 128))
    tn = 512
    return pl.pallas_call(
        _outproj_kernel,
        grid=(m // tm, D_MODEL // tn),
        in_specs=[pl.BlockSpec((tm, tn), lambda i, j: (i, j)),
                  pl.BlockSpec((tm, WIDTH_A), lambda i, j: (i, 0)),
                  pl.BlockSpec((tm, WIDTH_B), lambda i, j: (i, 0)),
                  pl.BlockSpec((tm, WIDTH_C), lambda i, j: (i, 0)),
                  pl.BlockSpec((None, WIDTH_A, tn), lambda i, j: (l, 0, j)),
                  pl.BlockSpec((None, WIDTH_B, tn), lambda i, j: (l, 0, j)),
                  pl.BlockSpec((None, WIDTH_C, tn), lambda i, j: (l, 0, j))],
        out_specs=pl.BlockSpec((tm, tn), lambda i, j: (i, j)),
        out_shape=jax.ShapeDtypeStruct((m, D_MODEL), F32),
        compiler_params=_cparams(("arbitrary", "arbitrary")),
        name="outproj",
    )(x, ya, yb, yc, wa, wb, wc)


def _ffn_kernel(x_ref, g_ref, wu_ref, wd_ref, o_ref, xn_ref, acc_ref):
    f = pl.program_id(1)

    @pl.when(f == 0)
    def _():
        xn_ref[...] = (_rms(x_ref[...]) * g_ref[...]).astype(BF16)
        acc_ref[...] = jnp.zeros(acc_ref.shape, F32)

    a = jnp.maximum(_dot(xn_ref[...], wu_ref[...]), 0.0)
    acc_ref[...] += _dot((a * a).astype(BF16), wd_ref[...])

    @pl.when(f == pl.num_programs(1) - 1)
    def _():
        o_ref[...] = x_ref[...] + acc_ref[...]


def _ffn(x, g, wu, wd, l):
    m = x.shape[0]
    tm = _pick(m, (512, 256, 128))
    tf = 512
    return pl.pallas_call(
        _ffn_kernel,
        grid=(m // tm, D_FF // tf),
        in_specs=[pl.BlockSpec((tm, D_MODEL), lambda i, f: (i, 0)),
                  pl.BlockSpec((None, 1, D_MODEL), lambda i, f: (l, 0, 0)),
                  pl.BlockSpec((None, D_MODEL, tf), lambda i, f: (l, 0, f)),
                  pl.BlockSpec((None, tf, D_MODEL), lambda i, f: (l, f, 0))],
        out_specs=pl.BlockSpec((tm, D_MODEL), lambda i, f: (i, 0)),
        out_shape=jax.ShapeDtypeStruct((m, D_MODEL), F32),
        scratch_shapes=[pltpu.VMEM((tm, D_MODEL), BF16), pltpu.VMEM((tm, D_MODEL), F32)],
        compiler_params=_cparams(("arbitrary", "arbitrary")),
        name="ffn",
    )(x, g, wu, wd)


def _norm_kernel(x_ref, g_ref, o_ref):
    o_ref[...] = _rms(x_ref[...]) * g_ref[...]


def _final_norm(x, g):
    m = x.shape[0]
    tm = _pick(m, (1024, 512, 256, 128))
    return pl.pallas_call(
        _norm_kernel,
        grid=(m // tm,),
        in_specs=[pl.BlockSpec((tm, D_MODEL), lambda i: (i, 0)),
                  pl.BlockSpec((1, D_MODEL), lambda i: (0, 0))],
        out_specs=pl.BlockSpec((tm, D_MODEL), lambda i: (i, 0)),
        out_shape=jax.ShapeDtypeStruct((m, D_MODEL), F32),
        compiler_params=_cparams(("arbitrary",)),
        name="final_norm",
    )(x, g)


def _t5_bucket(rel):
    half = N_BUCKETS // 2
    max_exact = half // 2
    n = jnp.abs(rel)
    nf = jnp.maximum(n, 1).astype(F32)
    large = max_exact + (jnp.log(nf / max_exact) / jnp.log(MAX_DISTANCE / max_exact)
                         * (half - max_exact)).astype(I32)
    large = jnp.minimum(large, half - 1)
    return jnp.where(rel > 0, half, 0) + jnp.where(n < max_exact, n, large)


def _bias_table(rel_bias, tq, nw):
    j = jnp.arange(nw, dtype=I32)[:, None]
    t = jnp.arange(tq, dtype=I32)[None, :]
    hot = jax.nn.one_hot(_t5_bucket(j - NEAR_SPAN - t), N_BUCKETS, dtype=F32)
    near = jnp.einsum("jtb,bh->jth", hot, rel_bias, precision=lax.Precision.HIGHEST)
    far = rel_bias[_t5_bucket(jnp.int32(-(NEAR_SPAN + 1)))]
    rep = N_HEADS_C // N_KV_C
    near = ((near - far) * LOG2E).astype(F32).reshape(nw, tq, N_KV_C, rep)
    return jnp.transpose(near, (2, 0, 3, 1)).reshape(N_KV_C, nw, rep * tq)


def _permute_w_in(w):
    sizes = (WIDTH_A, WIDTH_A, QK_B, QK_B, WIDTH_B, GATE_RANK_B, WIDTH_B, WIDTH_C, KV_C, KV_C, QI_C,
             D_IDX, N_IDX_HEADS)
    parts, off = [], 0
    for n in sizes:
        parts.append(w[..., off:off + n])
        off += n
    ua, va, qb, kb, vb, lrb, og, qc, kc, vc, qic, kic, wic = parts
    pad = jnp.zeros(w.shape[:-1] + (H_WIDTH - (C_MISC + D_IDX + GATE_RANK_B + N_IDX_HEADS),), w.dtype)
    return jnp.concatenate([qb, kb, vb, og, qc, kc, vc, qic, ua, va, kic, lrb, wic, pad], axis=-1).astype(BF16)


def kernel(x_prompt, x_sample, cache_c_k, cache_c_v, cache_c_kidx, state_b_s, norm_mix, w_in, norm_a_v,
           w_s_a, b_s_a, w_gate_b, b_gate_b, norm_b_o, rel_bias, w_o, norm_ffn, w_up, w_down, norm_final):
    depth = w_in.shape[0]
    nbp, t, _ = x_prompt.shape
    nbs, ts, _ = x_sample.shape
    p = cache_c_k.shape[2]
    assert nbp == 1 and ts == CHUNK and t % FAR_KW == 0 and (nbs * ts) % GROUP_A == 0
    assert p % NEAR_SPAN == 0 and p >= 2 * NEAR_SPAN

    x = jnp.concatenate([x_prompt[0], x_sample.reshape(nbs * ts, D_MODEL)], axis=0)
    w_in_p = _permute_w_in(w_in)
    wo_a = w_o[:, :WIDTH_A].astype(BF16)
    wo_b = w_o[:, WIDTH_A:WIDTH_A + WIDTH_B].astype(BF16)
    wo_c = w_o[:, WIDTH_A + WIDTH_B:].astype(BF16)
    w_up_b = w_up.astype(BF16)
    w_down_b = w_down.astype(BF16)
    w_gate = w_gate_b.astype(BF16)
    half = GROUP_A // 2
    w_s2 = jnp.stack([w_s_a, jnp.tile(w_s_a[:, :, :half, :half], (1, 1, 2, 2))], axis=1)
    b_s2 = jnp.stack([b_s_a, jnp.tile(b_s_a[:, :, :half], (1, 1, 2))], axis=1)
    b_s2 = jnp.swapaxes(b_s2, 2, 3)
    s_all = jnp.concatenate([jnp.zeros((depth, 1) + state_b_s.shape[2:], F32), state_b_s], axis=1)
    bias_near = _bias_table(rel_bias, NEAR_SPAN, 2 * NEAR_SPAN)
    bias_near_s = _bias_table(rel_bias, CHUNK, NEAR_SPAN + CHUNK)
    pk_all = cache_c_k.reshape(depth, nbs, N_KV_C * p, HD_C)
    pv_all = cache_c_v.reshape(depth, nbs, N_KV_C * p, HD_C)

    row = lambda a: a[:, None, :]
    outs = [[] for _ in range(9)]
    for l in range(depth):
        h = _inproj(x, row(norm_mix), w_in_p, l)
        hp, hs = h[:t], h[t:]
        ya, vn = _gmlp(h, row(norm_a_v), w_s2, b_s2, t, l)
        yb, s_out = _gla(h, w_gate, row(b_gate_b), row(norm_b_o), s_all, t, l)
        kb = hp[:, C_KC:C_KC + KV_C].astype(BF16)
        vt = hp[:, C_VC:C_VC + KV_C].astype(BF16).T
        vt_far = jnp.swapaxes(vt.reshape(KV_C, t // FAR_KW, FAR_KW), 0, 1)
        vt_near = jnp.swapaxes(vt.reshape(KV_C, t // NEAR_SPAN, NEAR_SPAN), 0, 1)
        kib = hp[:, C_MISC:C_MISC + D_IDX].astype(BF16)
        yc_p = _dsa_prompt(h, kb, vt_far, vt_near, kib, bias_near, t)
        yc_s = _dsa_sample(h, pk_all, pv_all, cache_c_kidx, bias_near_s, t, l)
        yc = jnp.concatenate([yc_p, yc_s], axis=0)
        x = _outproj(x, ya, yb, yc, wo_a, wo_b, wo_c, l)
        x = _ffn(x, row(norm_ffn), w_up_b, w_down_b, l)

        outs[0].append(hp[:, C_KC:C_KC + KV_C].reshape(1, t, N_KV_C, HD_C))
        outs[1].append(hp[:, C_VC:C_VC + KV_C].reshape(1, t, N_KV_C, HD_C))
        outs[2].append(hp[:, C_MISC:C_MISC + D_IDX].reshape(1, t, D_IDX))
        outs[3].append(s_out[:1])
        outs[4].append(hs[:, C_KC:C_KC + KV_C].reshape(nbs, ts, N_KV_C, HD_C))
        outs[5].append(hs[:, C_VC:C_VC + KV_C].reshape(nbs, ts, N_KV_C, HD_C))
        outs[6].append(hs[:, C_MISC:C_MISC + D_IDX].reshape(nbs, ts, D_IDX))
        outs[7].append(s_out[1:])
        outs[8].append(vn.reshape(nbs, ts, WIDTH_A))

    y = _final_norm(x, norm_final[None])
    return (y[:t][None], y[t:].reshape(nbs, ts, D_MODEL)) + tuple(jnp.stack(o) for o in outs)
```

```python
import functools
import math

import jax
import jax.numpy as jnp
from jax import lax
from jax.experimental import pallas as pl
from jax.experimental.pallas import tpu as pltpu

BF16 = jnp.bfloat16
F32 = jnp.float32
I32 = jnp.int32

D_MODEL = 2048
EPS = 1e-6
CHUNK = 64
GROUP_A = 128
N_GROUPS_A = 4
WIDTH_A = 512
N_HEADS_B = 6
DK_B = 64
DV_B = 128
GATE_RANK_B = 16
GATE_TEMP_B = 16.0
WIDTH_B = 768
N_HEADS_C = 6
N_KV_C = 2
HD_C = 128
N_IDX_HEADS = 8
D_IDX = 64
TOPK_MAX = 256
WIDTH_C = 768
N_BUCKETS = 32
MAX_DISTANCE = 128
D_FF = 4 * D_MODEL
QK_B = N_HEADS_B * DK_B
KV_C = N_KV_C * HD_C
QI_C = N_IDX_HEADS * D_IDX

LANES = 128
SUBLANES = 8
ONES_ROWS = 16
VMEM_LIMIT = 56 * 1024 * 1024

C_QB = 0
C_KB = 384
C_VB = 768
C_OG = 1536
C_QC = 2304
C_KC = 3072
C_VC = 3328
C_QIC = 3584
C_UA = 4096
C_VA = 4608
C_MISC = 5120
MISC_LRB = D_IDX
MISC_WIC = D_IDX + GATE_RANK_B
H_WIDTH = 5376

INT_MIN = -2147483648
INT_MAX = 2147483647
SNAP_EVERY = 8
BISECT_AFTER = 40
MAX_SEARCH_STEPS = 80
MASKED = -1e30
LOG2E = 1.4426950408889634
NEAR_SPAN = 128
FAR_KW = 512
PROMPT_TQ = 256
SOFTMAX_ROWS = 64


def _pick(n, cands):
    for c in cands:
        if n % c == 0:
            return c
    raise ValueError(f"no tile for {n}")


def _cparams(sem):
    return pltpu.CompilerParams(dimension_semantics=sem, vmem_limit_bytes=VMEM_LIMIT)


def _rms(x):
    return x * lax.rsqrt(jnp.mean(x * x, axis=-1, keepdims=True) + EPS)


def _dot(a, b):
    return jnp.dot(a, b, preferred_element_type=F32)


def _dot_nt(a, b):
    return lax.dot_general(a, b, (((1,), (1,)), ((), ())), preferred_element_type=F32)


def _dot_tn(a, b):
    return lax.dot_general(a, b, (((0,), (0,)), ((), ())), preferred_element_type=F32)


def _inproj_kernel(x_ref, g_ref, w_ref, o_ref, xn_ref):
    @pl.when(pl.program_id(1) == 0)
    def _():
        xn_ref[...] = (_rms(x_ref[...]) * g_ref[...]).astype(BF16)

    o_ref[...] = _dot(xn_ref[...], w_ref[...])


def _inproj(x, g, w, l):
    m = x.shape[0]
    tm = _pick(m, (1024, 512, 256, 128))
    tn = 768
    return pl.pallas_call(
        _inproj_kernel,
        grid=(m // tm, H_WIDTH // tn),
        in_specs=[pl.BlockSpec((tm, D_MODEL), lambda i, j: (i, 0)),
                  pl.BlockSpec((None, 1, D_MODEL), lambda i, j: (l, 0, 0)),
                  pl.BlockSpec((None, D_MODEL, tn), lambda i, j: (l, 0, j))],
        out_specs=pl.BlockSpec((tm, tn), lambda i, j: (i, j)),
        out_shape=jax.ShapeDtypeStruct((m, H_WIDTH), F32),
        scratch_shapes=[pltpu.VMEM((tm, D_MODEL), BF16)],
        compiler_params=_cparams(("arbitrary", "arbitrary")),
        name="inproj",
    )(x, g, w)


def _gmlp_kernel(h_ref, gv_ref, w_ref, b_ref, ya_ref, vn_ref, *, n_prompt_blocks):
    i = pl.program_id(0)
    hv = h_ref[...]
    u = jax.nn.gelu(hv[:, :WIDTH_A])
    v = jax.nn.gelu(hv[:, WIDTH_A:])
    r = lax.broadcasted_iota(I32, (GROUP_A, GROUP_A), 0)
    c = lax.broadcasted_iota(I32, (GROUP_A, GROUP_A), 1)
    samp = (i >= n_prompt_blocks).astype(I32)
    keep = (c <= r) & (((r >> 6) * samp) == ((c >> 6) * samp))
    for g in range(N_GROUPS_A):
        sl = slice(g * GROUP_A, (g + 1) * GROUP_A)
        vn = _rms(v[:, sl]) * gv_ref[:, sl]
        vn_ref[:, sl] = vn
        wm = jnp.where(keep, w_ref[0, g], 0.0).astype(BF16)
        z = _dot(wm, vn.astype(BF16)) + b_ref[0][:, g:g + 1]
        ya_ref[:, sl] = (u[:, sl] * z).astype(BF16)


def _gmlp(h, gv, w2, b2, n_prompt_rows, l):
    m = h.shape[0]
    npb = n_prompt_rows // GROUP_A
    nsb = (m - n_prompt_rows) // GROUP_A
    return pl.pallas_call(
        functools.partial(_gmlp_kernel, n_prompt_blocks=npb),
        grid=(npb + nsb,),
        in_specs=[pl.BlockSpec((GROUP_A, 2 * WIDTH_A), lambda i: (i, C_UA // (2 * WIDTH_A))),
                  pl.BlockSpec((None, 1, WIDTH_A), lambda i: (l, 0, 0)),
                  pl.BlockSpec((None, 1, N_GROUPS_A, GROUP_A, GROUP_A),
                               lambda i: (l, jnp.where(i >= npb, 1, 0), 0, 0, 0)),
                  pl.BlockSpec((None, 1, GROUP_A, N_GROUPS_A),
                               lambda i: (l, jnp.where(i >= npb, 1, 0), 0, 0))],
        out_specs=[pl.BlockSpec((GROUP_A, WIDTH_A), lambda i: (i, 0)),
                   pl.BlockSpec((GROUP_A, WIDTH_A), lambda i: (jnp.maximum(i - npb, 0), 0))],
        out_shape=[jax.ShapeDtypeStruct((m, WIDTH_A), BF16),
                   jax.ShapeDtypeStruct((nsb * GROUP_A, WIDTH_A), F32)],
        compiler_params=_cparams(("arbitrary",)),
        name="gmlp",
    )(h, gv, w2, b2)


def _gla_kernel(q_ref, k_ref, v_ref, og_ref, misc_ref, wg_ref, bg_ref, gbo_ref, s0_ref,
                yb_ref, sout_ref, st_ref, *, n_prompt_chunks):
    c = pl.program_id(0)

    @pl.when((c == 0) | (c >= n_prompt_chunks))
    def _():
        for h in range(N_HEADS_B):
            st_ref[h] = s0_ref[0, h].T

    lrb = misc_ref[:, MISC_LRB:MISC_LRB + GATE_RANK_B].astype(BF16)
    x = _dot(lrb, wg_ref[...]) + bg_ref[...]
    g = (jnp.minimum(x, 0.0) - jnp.log1p(jnp.exp(-jnp.abs(x)))) * (1.0 / GATE_TEMP_B)
    rr = lax.broadcasted_iota(I32, (CHUNK, CHUNK), 0)
    cc = lax.broadcasted_iota(I32, (CHUNK, CHUNK), 1)
    tril = cc <= rr
    tri = jnp.where(tril, 1.0, 0.0).astype(BF16)
    g1 = g.astype(BF16)
    r1 = g - g1.astype(F32)
    g2 = r1.astype(BF16)
    g3 = (r1 - g2.astype(F32)).astype(BF16)
    b = _dot(tri, g1) + _dot(tri, g2) + _dot(tri, g3)
    b_last = b[CHUNK - 1:CHUNK, :]
    b_mid = b[CHUNK // 2 - 1:CHUNK // 2, :]
    qs = q_ref[...] * (DK_B ** -0.5)
    kk = k_ref[...]
    q_inter = (qs * jnp.exp(b)).astype(BF16)
    q_intra = (qs * jnp.exp(b - b_mid)).astype(BF16)
    k_intra = (kk * jnp.exp(b_mid - b)).astype(BF16)
    k_state = (kk * jnp.exp(b_last - b)).astype(BF16)
    dec = jnp.exp(b_last)
    for h in range(N_HEADS_B):
        sk = slice(h * DK_B, (h + 1) * DK_B)
        sv = slice(h * DV_B, (h + 1) * DV_B)
        st = st_ref[h]
        vh = v_ref[:, sv].astype(BF16)
        o = _dot_nt(q_inter[:, sk], st.astype(BF16))
        att = jnp.where(tril, _dot_nt(q_intra[:, sk], k_intra[:, sk]), 0.0)
        o = o + _dot(att.astype(BF16), vh)
        st_new = st * dec[:, sk] + _dot_tn(vh, k_state[:, sk])
        st_ref[h] = st_new
        sout_ref[0, h] = st_new.T
        on = _rms(o) * gbo_ref[:, sv]
        yb_ref[:, sv] = (on * jax.nn.silu(og_ref[:, sv])).astype(BF16)


def _gla(h, wg, bg, gbo, s_all, n_prompt_rows, l):
    m = h.shape[0]
    npc = n_prompt_rows // CHUNK
    nch = m // CHUNK
    nseq = s_all.shape[1]

    def seq(c):
        return jnp.where(c < npc, 0, c - npc + 1)

    return pl.pallas_call(
        functools.partial(_gla_kernel, n_prompt_chunks=npc),
        grid=(nch,),
        in_specs=[pl.BlockSpec((CHUNK, QK_B), lambda c: (c, C_QB // QK_B)),
                  pl.BlockSpec((CHUNK, QK_B), lambda c: (c, C_KB // QK_B)),
                  pl.BlockSpec((CHUNK, WIDTH_B), lambda c: (c, C_VB // WIDTH_B)),
                  pl.BlockSpec((CHUNK, WIDTH_B), lambda c: (c, C_OG // WIDTH_B)),
                  pl.BlockSpec((CHUNK, LANES), lambda c: (c, C_MISC // LANES)),
                  pl.BlockSpec((None, GATE_RANK_B, QK_B), lambda c: (l, 0, 0)),
                  pl.BlockSpec((None, 1, QK_B), lambda c: (l, 0, 0)),
                  pl.BlockSpec((None, 1, WIDTH_B), lambda c: (l, 0, 0)),
                  pl.BlockSpec((None, 1, N_HEADS_B, DK_B, DV_B), lambda c: (l, seq(c), 0, 0, 0))],
        out_specs=[pl.BlockSpec((CHUNK, WIDTH_B), lambda c: (c, 0)),
                   pl.BlockSpec((1, N_HEADS_B, DK_B, DV_B), lambda c: (seq(c), 0, 0, 0))],
        out_shape=[jax.ShapeDtypeStruct((m, WIDTH_B), BF16),
                   jax.ShapeDtypeStruct((nseq, N_HEADS_B, DK_B, DV_B), F32)],
        scratch_shapes=[pltpu.VMEM((N_HEADS_B, DV_B, DK_B), F32)],
        compiler_params=_cparams(("arbitrary",)),
        name="gla",
    )(h, h, h, h, h, wg, bg, gbo, s_all)


def _fold_rows(x, op):
    tile = SUBLANES * (4 // x.dtype.itemsize)
    parts = [x[j * tile:(j + 1) * tile] for j in range(x.shape[0] // tile)]
    while len(parts) > 1:
        nxt = [op(parts[j], parts[j + 1]) for j in range(0, len(parts) - 1, 2)]
        parts = nxt + parts[len(parts) - len(parts) % 2:]
    return parts[0]


def _loop(n, body, init):
    if isinstance(n, int):
        carry = init
        for c in range(n):
            carry = body(c, carry)
        return carry
    return lax.fori_loop(0, n, body, init)


def _dsa_core(q, qi, misc_t, far_ki, far_k, far_vt, nf, far_adm, near_ki, near_k, near_vt, near_adm, n_adm,
              bias_near_ref, out_ref, scratch, *, tq, kw, nw, topk):
    fkeys_ref, nkeys_ref, m_ref, acc_ref, ss_ref, p_ref = scratch
    rep = N_HEADS_C // N_KV_C
    groups = range(N_KV_C)
    sls = [slice(g * HD_C, (g + 1) * HD_C) for g in groups]
    qi_s = (qi * (D_IDX ** -0.5)).astype(BF16)
    qi_all = jnp.concatenate([qi_s[:, h * D_IDX:(h + 1) * D_IDX] for h in range(N_IDX_HEADS)], axis=0)
    wi_t = misc_t[MISC_WIC:MISC_WIC + N_IDX_HEADS, :] * (N_IDX_HEADS ** -0.5)
    qs = q * (HD_C ** -0.5 * LOG2E)
    q_g = [jnp.concatenate([qs[:, (g * rep + r) * HD_C:(g * rep + r + 1) * HD_C] for r in range(rep)],
                           axis=0).astype(BF16) for g in groups]
    kf = float(topk)

    def keys_of(ki_blk, adm):
        d = _dot_nt(ki_blk, qi_all)
        acc = None
        for h in range(N_IDX_HEADS):
            t = wi_t[h:h + 1, :] * jnp.maximum(d[:, h * tq:(h + 1) * tq], 0.0)
            acc = t if acc is None else acc + t
        bits = lax.bitcast_convert_type(acc, I32)
        key = bits ^ ((bits >> 31) & 0x7FFFFFFF)
        key = jnp.where(acc == 0.0, 0, key)
        if adm is not None:
            key = jnp.where(adm, key, INT_MIN)
        return key

    def extremes(key, carry):
        kmax, kmin = carry
        kmax = jnp.maximum(kmax, _fold_rows(key, jnp.maximum))
        kmin = jnp.minimum(kmin, _fold_rows(jnp.where(key == INT_MIN, INT_MAX, key), jnp.minimum))
        return kmax, kmin

    def fill(c, carry):
        key = keys_of(far_ki(c), None if far_adm is None else far_adm(c))
        fkeys_ref[c] = key
        return extremes(key, carry)

    carry = _loop(nf, fill, (jnp.full((SUBLANES, tq), INT_MIN, I32), jnp.full((SUBLANES, tq), INT_MAX, I32)))
    key = keys_of(near_ki, near_adm)
    nkeys_ref[...] = key
    kmax, kmin = extremes(key, carry)
    kmax = jnp.max(kmax, axis=0, keepdims=True)
    kmin = jnp.min(kmin, axis=0, keepdims=True)

    def count(thr, strict):
        def cmp(x):
            hit = (x > thr) if strict else (x >= thr)
            return _fold_rows(jnp.where(hit, 1.0, 0.0), jnp.add)

        part = _loop(nf, lambda c, a: a + cmp(fkeys_ref[c]), jnp.zeros((SUBLANES, tq), F32))
        part = part + cmp(nkeys_ref[...])
        return jnp.sum(part, axis=0, keepdims=True)

    def bits_flip(x):
        return x ^ ((x >> 31) & 0x7FFFFFFF)

    def any_lane(flag):
        return jnp.max(jnp.where(flag, 1.0, 0.0))

    def snap(lo, hi):
        def pull(x, carry):
            kin, kax = carry
            kin = jnp.minimum(kin, _fold_rows(jnp.where(x >= lo, x, INT_MAX), jnp.minimum))
            kax = jnp.maximum(kax, _fold_rows(jnp.where(x < hi, x, INT_MIN), jnp.maximum))
            return kin, kax

        init = (jnp.full((SUBLANES, tq), INT_MAX, I32), jnp.full((SUBLANES, tq), INT_MIN, I32))
        kin, kax = pull(nkeys_ref[...], _loop(nf, lambda c, carry: pull(fkeys_ref[c], carry), init))
        return jnp.min(kin, axis=0, keepdims=True), jnp.max(kax, axis=0, keepdims=True)

    enough = n_adm >= kf
    log_k = math.log2(topk)

    def open_lanes(lo, hi, flo):
        return enough & (flo > kf) & (hi > lo + 1)

    def log_excess(c):
        return jnp.log2(jnp.maximum(c, 0.5)) - log_k

    def step(state):
        it, _, lo, hi, flo, fhi, glo, ghi, side = state
        active = open_lanes(lo, hi, flo)

        def pull_in(_):
            kin, kax = snap(lo, hi)
            return jnp.where(active, kin, lo), jnp.where(active, kax + 1, hi), flo, fhi, glo, ghi, side

        def probe(_):
            v_lo = lax.bitcast_convert_type(bits_flip(lo), F32)
            v_hi = lax.bitcast_convert_type(bits_flip(hi), F32)
            frac = jnp.minimum(jnp.maximum(glo / (glo - ghi), 0.02), 0.98)
            t_int = bits_flip(lax.bitcast_convert_type(v_lo + (v_hi - v_lo) * frac, I32))
            t_mid = (lo & hi) + ((lo ^ hi) >> 1)
            t = jnp.where(it >= BISECT_AFTER, t_mid, t_int)
            t = jnp.minimum(jnp.maximum(t, lo + 1), hi - 1)
            c = count(t, False)
            g = log_excess(c)
            up = active & (c >= kf)
            dn = active & (c < kf)
            ghi2 = jnp.where(up & (side > 0.0), ghi * 0.5, ghi)
            glo2 = jnp.where(dn & (side < 0.0), glo * 0.5, glo)
            return (jnp.where(up, t, lo), jnp.where(dn, t, hi), jnp.where(up, c, flo), jnp.where(dn, c, fhi),
                    jnp.where(up, g, glo2), jnp.where(dn, g, ghi2), jnp.where(up, 1.0, jnp.where(dn, -1.0, side)))

        is_snap = (it % SNAP_EVERY == SNAP_EVERY - 1) & (it < BISECT_AFTER)
        lo, hi, flo, fhi, glo, ghi, side = lax.cond(is_snap, pull_in, probe, 0)
        return it + 1, any_lane(open_lanes(lo, hi, flo)), lo, hi, flo, fhi, glo, ghi, side

    hi0 = kmax + 1
    zero = jnp.zeros((1, tq), F32)
    state = (jnp.int32(0), any_lane(open_lanes(kmin, hi0, n_adm)), kmin, hi0, n_adm, zero,
             log_excess(n_adm), log_excess(zero), zero)
    state = lax.while_loop(lambda st: (st[1] > 0.0) & (st[0] < MAX_SEARCH_STEPS), step, state)
    thr = jnp.where(enough, state[2], INT_MIN)
    has_tie = any_lane(enough & (state[4] > kf)) > 0.0

    @pl.when(has_tie)
    def _():
        need = kf - count(thr, True)

        def lower(n):
            a = lax.broadcasted_iota(I32, (n, n), 0)
            b = lax.broadcasted_iota(I32, (n, n), 1)
            return jnp.where(b < a, 1.0, 0.0).astype(BF16)

        def demote(keys, run, lt):
            eq = (keys == thr) & (keys > INT_MIN)
            eqf = jnp.where(eq, 1.0, 0.0)
            before = _dot(lt, eqf.astype(BF16)) + run
            keys = jnp.where(eq & (before >= need), INT_MIN, keys)
            return keys, run + jnp.sum(eqf, axis=0, keepdims=True)

        lt_far = lower(kw)

        def step(c, run):
            keys, run = demote(fkeys_ref[c], run, lt_far)
            fkeys_ref[c] = keys
            return run

        run = _loop(nf, step, jnp.zeros((1, tq), F32))
        keys, _ = demote(nkeys_ref[...], run, lower(nw))
        nkeys_ref[...] = keys

    thr_sel = jnp.maximum(thr, INT_MIN + 1)
    m_ref[...] = jnp.full(m_ref.shape, MASKED, F32)
    acc_ref[...] = jnp.zeros(acc_ref.shape, F32)

    def logits(k_blk):
        return [_dot_nt(k_blk[:, sls[g]], q_g[g]) for g in groups]

    def attend(slot, keys_rows, vt_blk, bias_ref, n):
        tiles = [slice(r0, r0 + SOFTMAX_ROWS) for r0 in range(0, n, SOFTMAX_ROWS)]
        heads = [slice(r * tq, (r + 1) * tq) for r in range(rep)]

        def masked_logits(g, rows, selb):
            s = ss_ref[slot, g, rows, :]
            if bias_ref is not None:
                s = s + bias_ref[g, rows, :]
            return [s[:, hd] + selb for hd in heads]

        run = [None] * N_KV_C
        for rows in tiles:
            selb = jnp.where(keys_rows(rows) >= thr_sel, 0.0, MASKED)
            for g in groups:
                top = jnp.concatenate([_fold_rows(s, jnp.maximum) for s in masked_logits(g, rows, selb)], axis=1)
                run[g] = top if run[g] is None else jnp.maximum(run[g], top)
        m_new = []
        for g in groups:
            m_old = m_ref[g]
            m_new.append(jnp.maximum(m_old, jnp.max(run[g], axis=0, keepdims=True)))
            acc_ref[g] = jnp.exp2(m_old - m_new[g]) * acc_ref[g]
            m_ref[g] = m_new[g]
        for rows in tiles:
            selb = jnp.where(keys_rows(rows) >= thr_sel, 0.0, MASKED)
            for g in groups:
                for hd, s in zip(heads, masked_logits(g, rows, selb)):
                    p_ref[g, rows, hd] = jnp.exp2(s - m_new[g][:, hd]).astype(BF16)
        ones = jnp.ones((ONES_ROWS, n), BF16)
        for g in groups:
            acc_ref[g] += _dot(jnp.concatenate([vt_blk[sls[g], :], ones], axis=0), p_ref[g, :n, :])

    last = fkeys_ref.shape[0] - 1

    def put_logits(slot, k_blk, n):
        for g, s in enumerate(logits(k_blk)):
            ss_ref[slot, g, :n, :] = s

    def far_body(c, cur):
        put_logits(1 - cur, far_k(min(c + 1, last) if isinstance(c, int) else jnp.minimum(c + 1, last)), kw)
        attend(cur, lambda rows: fkeys_ref[c, rows, :], far_vt(c), None, kw)

    def far_step(c, carry):
        if isinstance(c, int):
            far_body(c, c % 2)
        else:
            for cur in range(2):
                pl.when(c % 2 == cur)(functools.partial(far_body, c, cur))
        return carry

    put_logits(0, far_k(0), kw)
    _loop(nf, far_step, 0)
    put_logits(0, near_k, nw)
    attend(0, lambda rows: nkeys_ref[rows, :], near_vt, bias_near_ref, nw)

    for g in groups:
        a = acc_ref[g]
        o = a[:HD_C] / a[HD_C:HD_C + 1]
        for r in range(rep):
            hh = g * rep + r
            out_ref[:, hh * HD_C:(hh + 1) * HD_C] = o[:, r * tq:(r + 1) * tq].T.astype(BF16)


def _dsa_scratch(nf, tq, kw, nw):
    rep = N_HEADS_C // N_KV_C
    return [pltpu.VMEM((nf, kw, tq), I32),
            pltpu.VMEM((nw, tq), I32),
            pltpu.VMEM((N_KV_C, 1, rep * tq), F32),
            pltpu.VMEM((N_KV_C, HD_C + ONES_ROWS, rep * tq), F32),
            pltpu.VMEM((2, N_KV_C, max(kw, nw), rep * tq), F32),
            pltpu.VMEM((N_KV_C, max(kw, nw), rep * tq), BF16)]


def _dsa_prompt_kernel(q_ref, qi_ref, misc_ref, k_ref, vt_far_ref, vt_near_ref, ki_ref, bias_near_ref,
                       out_ref, *scratch, tq, topk):
    i = pl.program_id(0)
    kw = FAR_KW
    nw = NEAR_SPAN + tq
    q0 = i * tq
    far_limit = q0 - NEAR_SPAN
    near0 = jnp.maximum(far_limit, 0)
    nf = (near0 + kw - 1) // kw

    def rows(c):
        return pl.ds(pl.multiple_of(c * kw, kw), kw)

    def far_adm(c):
        return (c * kw + lax.broadcasted_iota(I32, (kw, tq), 0)) < far_limit

    left = pl.ds(pl.multiple_of(near0, NEAR_SPAN), NEAR_SPAN)
    right = pl.ds(pl.multiple_of(q0, tq), tq)
    near_ki = jnp.concatenate([ki_ref[left, :], ki_ref[right, :]], axis=0)
    near_k = jnp.concatenate([k_ref[left, :], k_ref[right, :]], axis=0)
    per = tq // NEAR_SPAN
    near_vt = jnp.concatenate([vt_near_ref[near0 // NEAR_SPAN]] + [vt_near_ref[i * per + j] for j in range(per)],
                              axis=1)
    key = lax.broadcasted_iota(I32, (nw, tq), 0)
    qry = lax.broadcasted_iota(I32, (nw, tq), 1)
    first_key = jnp.where(i > 0, 0, NEAR_SPAN)
    near_adm = (key >= first_key) & (((key - NEAR_SPAN) >> 6) <= (qry >> 6))
    n_adm = ((((q0 + lax.broadcasted_iota(I32, (1, tq), 1)) >> 6) + 1) * CHUNK).astype(F32)

    _dsa_core(q_ref[...], qi_ref[...], misc_ref[...].T,
              lambda c: ki_ref[rows(c), :], lambda c: k_ref[rows(c), :], lambda c: vt_far_ref[c],
              nf, far_adm, near_ki, near_k, near_vt, near_adm, n_adm,
              bias_near_ref, out_ref, scratch,
              tq=tq, kw=kw, nw=nw, topk=topk)


def _dsa_prompt(h, kb, vt_far, vt_near, kib, bias_near, t):
    tq = PROMPT_TQ
    topk = min(TOPK_MAX, t // 4)
    return pl.pallas_call(
        functools.partial(_dsa_prompt_kernel, tq=tq, topk=topk),
        grid=(t // tq,),
        in_specs=[pl.BlockSpec((tq, WIDTH_C), lambda i: (i, C_QC // WIDTH_C)),
                  pl.BlockSpec((tq, QI_C), lambda i: (i, C_QIC // QI_C)),
                  pl.BlockSpec((tq, LANES), lambda i: (i, C_MISC // LANES)),
                  pl.BlockSpec((t, KV_C), lambda i: (0, 0)),
                  pl.BlockSpec(vt_far.shape, lambda i: (0, 0, 0)),
                  pl.BlockSpec(vt_near.shape, lambda i: (0, 0, 0)),
                  pl.BlockSpec((t, D_IDX), lambda i: (0, 0)),
                  pl.BlockSpec(bias_near.shape, lambda i: (0, 0, 0))],
        out_specs=pl.BlockSpec((tq, WIDTH_C), lambda i: (i, 0)),
        out_shape=jax.ShapeDtypeStruct((t, WIDTH_C), BF16),
        scratch_shapes=_dsa_scratch(t // FAR_KW, tq, FAR_KW, NEAR_SPAN + tq),
        compiler_params=_cparams(("arbitrary",)),
        name="dsa_prompt",
    )(h, h, h, kb, vt_far, vt_near, kib, bias_near)


def _dsa_sample_kernel(q_ref, qi_ref, misc_ref, kvn_ref, pk_ref, pv_ref, pki_ref, bias_near_ref,
                       out_ref, *scratch, kw, nf, topk):
    tq = CHUNK
    far_len = kw * nf
    misc = misc_ref[...]
    kvn = kvn_ref[...]

    def group_rows(ref, start, n, g):
        return ref[0, pl.ds(2 * start + g, n, stride=2), :]

    def past_k(start, n):
        return jnp.concatenate([group_rows(pk_ref, start, n, g) for g in range(N_KV_C)], axis=1)

    def past_vt(start, n):
        return jnp.concatenate([group_rows(pv_ref, start, n, g).T for g in range(N_KV_C)], axis=0)

    near_ki = jnp.concatenate([pki_ref[0, pl.ds(far_len, NEAR_SPAN), :], misc[:, :D_IDX]], axis=0).astype(BF16)
    near_k = jnp.concatenate([past_k(far_len, NEAR_SPAN), kvn[:, :KV_C]], axis=0).astype(BF16)
    near_vt = jnp.concatenate([past_vt(far_len, NEAR_SPAN), kvn[:, KV_C:].T], axis=1).astype(BF16)

    _dsa_core(q_ref[...], qi_ref[...], misc.T,
              lambda c: pki_ref[0, pl.ds(c * kw, kw), :].astype(BF16),
              lambda c: past_k(c * kw, kw).astype(BF16), lambda c: past_vt(c * kw, kw).astype(BF16),
              nf, None, near_ki, near_k, near_vt, None, jnp.full((1, tq), float(far_len + NEAR_SPAN + CHUNK), F32),
              bias_near_ref, out_ref, scratch,
              tq=tq, kw=kw, nw=NEAR_SPAN + CHUNK, topk=topk)


def _dsa_sample(h, pk, pv, pki, bias_near, t, l):
    nb, p = pki.shape[1], pki.shape[2]
    tq = CHUNK
    far_len = p - NEAR_SPAN
    kw = _pick(far_len, (640, 512, 384, 256, 128))
    nf = far_len // kw
    nw = NEAR_SPAN + CHUNK
    topk = min(TOPK_MAX, (p + CHUNK) // 4)
    row0 = t // tq
    return pl.pallas_call(
        functools.partial(_dsa_sample_kernel, kw=kw, nf=nf, topk=topk),
        grid=(nb,),
        in_specs=[pl.BlockSpec((tq, WIDTH_C), lambda b: (row0 + b, C_QC // WIDTH_C)),
                  pl.BlockSpec((tq, QI_C), lambda b: (row0 + b, C_QIC // QI_C)),
                  pl.BlockSpec((tq, LANES), lambda b: (row0 + b, C_MISC // LANES)),
                  pl.BlockSpec((tq, 2 * KV_C), lambda b: (row0 + b, C_KC // (2 * KV_C))),
                  pl.BlockSpec((None, 1, N_KV_C * p, HD_C), lambda b: (l, b, 0, 0)),
                  pl.BlockSpec((None, 1, N_KV_C * p, HD_C), lambda b: (l, b, 0, 0)),
                  pl.BlockSpec((None, 1, p, D_IDX), lambda b: (l, b, 0, 0)),
                  pl.BlockSpec(bias_near.shape, lambda b: (0, 0, 0))],
        out_specs=pl.BlockSpec((tq, WIDTH_C), lambda b: (b, 0)),
        out_shape=jax.ShapeDtypeStruct((nb * tq, WIDTH_C), BF16),
        scratch_shapes=_dsa_scratch(nf, tq, kw, nw),
        compiler_params=_cparams(("arbitrary",)),
        name="dsa_sample",
    )(h, h, h, h, pk, pv, pki, bias_near)


def _outproj_kernel(x_ref, ya_ref, yb_ref, yc_ref, wa_ref, wb_ref, wc_ref, o_ref):
    o_ref[...] = (x_ref[...] + _dot(ya_ref[...], wa_ref[...]) + _dot(yb_ref[...], wb_ref[...])
                  + _dot(yc_ref[...], wc_ref[...]))


def _outproj(x, ya, yb, yc, wa, wb, wc, l):
    m = x.shape[0]
    tm = _pick(m, (1024, 512, 256, 128))
    tn = 512
    return pl.pallas_call(
        _outproj_kernel,
        grid=(m // tm, D_MODEL // tn),
        in_specs=[pl.BlockSpec((tm, tn), lambda i, j: (i, j)),
                  pl.BlockSpec((tm, WIDTH_A), lambda i, j: (i, 0)),
                  pl.BlockSpec((tm, WIDTH_B), lambda i, j: (i, 0)),
                  pl.BlockSpec((tm, WIDTH_C), lambda i, j: (i, 0)),
                  pl.BlockSpec((None, WIDTH_A, tn), lambda i, j: (l, 0, j)),
                  pl.BlockSpec((None, WIDTH_B, tn), lambda i, j: (l, 0, j)),
                  pl.BlockSpec((None, WIDTH_C, tn), lambda i, j: (l, 0, j))],
        out_specs=pl.BlockSpec((tm, tn), lambda i, j: (i, j)),
        out_shape=jax.ShapeDtypeStruct((m, D_MODEL), F32),
        compiler_params=_cparams(("arbitrary", "arbitrary")),
        name="outproj",
    )(x, ya, yb, yc, wa, wb, wc)


def _ffn_kernel(x_ref, g_ref, wu_ref, wd_ref, o_ref, xn_ref, acc_ref):
    f = pl.program_id(1)

    @pl.when(f == 0)
    def _():
        xn_ref[...] = (_rms(x_ref[...]) * g_ref[...]).astype(BF16)
        acc_ref[...] = jnp.zeros(acc_ref.shape, F32)

    a = jnp.maximum(_dot(xn_ref[...], wu_ref[...]), 0.0)
    acc_ref[...] += _dot((a * a).astype(BF16), wd_ref[...])

    @pl.when(f == pl.num_programs(1) - 1)
    def _():
        o_ref[...] = x_ref[...] + acc_ref[...]


def _ffn(x, g, wu, wd, l):
    m = x.shape[0]
    tm = _pick(m, (512, 256, 128))
    tf = 512
    return pl.pallas_call(
        _ffn_kernel,
        grid=(m // tm, D_FF // tf),
        in_specs=[pl.BlockSpec((tm, D_MODEL), lambda i, f: (i, 0)),
                  pl.BlockSpec((None, 1, D_MODEL), lambda i, f: (l, 0, 0)),
                  pl.BlockSpec((None, D_MODEL, tf), lambda i, f: (l, 0, f)),
                  pl.BlockSpec((None, tf, D_MODEL), lambda i, f: (l, f, 0))],
        out_specs=pl.BlockSpec((tm, D_MODEL), lambda i, f: (i, 0)),
        out_shape=jax.ShapeDtypeStruct((m, D_MODEL), F32),
        scratch_shapes=[pltpu.VMEM((tm, D_MODEL), BF16), pltpu.VMEM((tm, D_MODEL), F32)],
        compiler_params=_cparams(("arbitrary", "arbitrary")),
        name="ffn",
    )(x, g, wu, wd)


def _norm_kernel(x_ref, g_ref, o_ref):
    o_ref[...] = _rms(x_ref[...]) * g_ref[...]


def _final_norm(x, g):
    m = x.shape[0]
    tm = _pick(m, (1024, 512, 256, 128))
    return pl.pallas_call(
        _norm_kernel,
        grid=(m // tm,),
        in_specs=[pl.BlockSpec((tm, D_MODEL), lambda i: (i, 0)),
                  pl.BlockSpec((1, D_MODEL), lambda i: (0, 0))],
        out_specs=pl.BlockSpec((tm, D_MODEL), lambda i: (i, 0)),
        out_shape=jax.ShapeDtypeStruct((m, D_MODEL), F32),
        compiler_params=_cparams(("arbitrary",)),
        name="final_norm",
    )(x, g)


def _t5_bucket(rel):
    half = N_BUCKETS // 2
    max_exact = half // 2
    n = jnp.abs(rel)
    nf = jnp.maximum(n, 1).astype(F32)
    large = max_exact + (jnp.log(nf / max_exact) / jnp.log(MAX_DISTANCE / max_exact)
                         * (half - max_exact)).astype(I32)
    large = jnp.minimum(large, half - 1)
    return jnp.where(rel > 0, half, 0) + jnp.where(n < max_exact, n, large)


def _bias_table(rel_bias, tq, nw):
    j = jnp.arange(nw, dtype=I32)[:, None]
    t = jnp.arange(tq, dtype=I32)[None, :]
    hot = jax.nn.one_hot(_t5_bucket(j - NEAR_SPAN - t), N_BUCKETS, dtype=F32)
    near = jnp.einsum("jtb,bh->jth", hot, rel_bias, precision=lax.Precision.HIGHEST)
    far = rel_bias[_t5_bucket(jnp.int32(-(NEAR_SPAN + 1)))]
    rep = N_HEADS_C // N_KV_C
    near = ((near - far) * LOG2E).astype(F32).reshape(nw, tq, N_KV_C, rep)
    return jnp.transpose(near, (2, 0, 3, 1)).reshape(N_KV_C, nw, rep * tq)


def _permute_w_in(w):
    sizes = (WIDTH_A, WIDTH_A, QK_B, QK_B, WIDTH_B, GATE_RANK_B, WIDTH_B, WIDTH_C, KV_C, KV_C, QI_C,
             D_IDX, N_IDX_HEADS)
    parts, off = [], 0
    for n in sizes:
        parts.append(w[..., off:off + n])
        off += n
    ua, va, qb, kb, vb, lrb, og, qc, kc, vc, qic, kic, wic = parts
    pad = jnp.zeros(w.shape[:-1] + (H_WIDTH - (C_MISC + D_IDX + GATE_RANK_B + N_IDX_HEADS),), w.dtype)
    return jnp.concatenate([qb, kb, vb, og, qc, kc, vc, qic, ua, va, kic, lrb, wic, pad], axis=-1).astype(BF16)


def kernel(x_prompt, x_sample, cache_c_k, cache_c_v, cache_c_kidx, state_b_s, norm_mix, w_in, norm_a_v,
           w_s_a, b_s_a, w_gate_b, b_gate_b, norm_b_o, rel_bias, w_o, norm_ffn, w_up, w_down, norm_final):
    depth = w_in.shape[0]
    nbp, t, _ = x_prompt.shape
    nbs, ts, _ = x_sample.shape
    p = cache_c_k.shape[2]
    assert nbp == 1 and ts == CHUNK and t % FAR_KW == 0 and (nbs * ts) % GROUP_A == 0
    assert p % NEAR_SPAN == 0 and p >= 2 * NEAR_SPAN

    x = jnp.concatenate([x_prompt[0], x_sample.reshape(nbs * ts, D_MODEL)], axis=0)
    w_in_p = _permute_w_in(w_in)
    wo_a = w_o[:, :WIDTH_A].astype(BF16)
    wo_b = w_o[:, WIDTH_A:WIDTH_A + WIDTH_B].astype(BF16)
    wo_c = w_o[:, WIDTH_A + WIDTH_B:].astype(BF16)
    w_up_b = w_up.astype(BF16)
    w_down_b = w_down.astype(BF16)
    w_gate = w_gate_b.astype(BF16)
    half = GROUP_A // 2
    w_s2 = jnp.stack([w_s_a, jnp.tile(w_s_a[:, :, :half, :half], (1, 1, 2, 2))], axis=1)
    b_s2 = jnp.stack([b_s_a, jnp.tile(b_s_a[:, :, :half], (1, 1, 2))], axis=1)
    b_s2 = jnp.swapaxes(b_s2, 2, 3)
    s_all = jnp.concatenate([jnp.zeros((depth, 1) + state_b_s.shape[2:], F32), state_b_s], axis=1)
    bias_near = _bias_table(rel_bias, PROMPT_TQ, NEAR_SPAN + PROMPT_TQ)
    bias_near_s = _bias_table(rel_bias, CHUNK, NEAR_SPAN + CHUNK)
    pk_all = cache_c_k.reshape(depth, nbs, N_KV_C * p, HD_C)
    pv_all = cache_c_v.reshape(depth, nbs, N_KV_C * p, HD_C)

    row = lambda a: a[:, None, :]
    outs = [[] for _ in range(9)]
    for l in range(depth):
        h = _inproj(x, row(norm_mix), w_in_p, l)
        hp, hs = h[:t], h[t:]
        ya, vn = _gmlp(h, row(norm_a_v), w_s2, b_s2, t, l)
        yb, s_out = _gla(h, w_gate, row(b_gate_b), row(norm_b_o), s_all, t, l)
        kb = hp[:, C_KC:C_KC + KV_C].astype(BF16)
        vt = hp[:, C_VC:C_VC + KV_C].astype(BF16).T
        vt_far = jnp.swapaxes(vt.reshape(KV_C, t // FAR_KW, FAR_KW), 0, 1)
        vt_near = jnp.swapaxes(vt.reshape(KV_C, t // NEAR_SPAN, NEAR_SPAN), 0, 1)
        kib = hp[:, C_MISC:C_MISC + D_IDX].astype(BF16)
        yc_p = _dsa_prompt(h, kb, vt_far, vt_near, kib, bias_near, t)
        yc_s = _dsa_sample(h, pk_all, pv_all, cache_c_kidx, bias_near_s, t, l)
        yc = jnp.concatenate([yc_p, yc_s], axis=0)
        x = _outproj(x, ya, yb, yc, wo_a, wo_b, wo_c, l)
        x = _ffn(x, row(norm_ffn), w_up_b, w_down_b, l)

        outs[0].append(hp[:, C_KC:C_KC + KV_C].reshape(1, t, N_KV_C, HD_C))
        outs[1].append(hp[:, C_VC:C_VC + KV_C].reshape(1, t, N_KV_C, HD_C))
        outs[2].append(hp[:, C_MISC:C_MISC + D_IDX].reshape(1, t, D_IDX))
        outs[3].append(s_out[:1])
        outs[4].append(hs[:, C_KC:C_KC + KV_C].reshape(nbs, ts, N_KV_C, HD_C))
        outs[5].append(hs[:, C_VC:C_VC + KV_C].reshape(nbs, ts, N_KV_C, HD_C))
        outs[6].append(hs[:, C_MISC:C_MISC + D_IDX].reshape(nbs, ts, D_IDX))
        outs[7].append(s_out[1:])
        outs[8].append(vn.reshape(nbs, ts, WIDTH_A))

    y = _final_norm(x, norm_final[None])
    return (y[:t][None], y[t:].reshape(nbs, ts, D_MODEL)) + tuple(jnp.stack(o) for o in outs)
```

```python
import functools
import math

import jax
import jax.numpy as jnp
from jax import lax
from jax.experimental import pallas as pl
from jax.experimental.pallas import tpu as pltpu

BF16 = jnp.bfloat16
F32 = jnp.float32
I32 = jnp.int32

D_MODEL = 2048
EPS = 1e-6
CHUNK = 64
GROUP_A = 128
N_GROUPS_A = 4
WIDTH_A = 512
N_HEADS_B = 6
DK_B = 64
DV_B = 128
GATE_RANK_B = 16
GATE_TEMP_B = 16.0
WIDTH_B = 768
N_HEADS_C = 6
N_KV_C = 2
HD_C = 128
N_IDX_HEADS = 8
D_IDX = 64
TOPK_MAX = 256
WIDTH_C = 768
N_BUCKETS = 32
MAX_DISTANCE = 128
D_FF = 4 * D_MODEL
QK_B = N_HEADS_B * DK_B
KV_C = N_KV_C * HD_C
QI_C = N_IDX_HEADS * D_IDX

LANES = 128
SUBLANES = 8
ONES_ROWS = 16
VMEM_LIMIT = 56 * 1024 * 1024

C_QB = 0
C_KB = 384
C_VB = 768
C_OG = 1536
C_QC = 2304
C_KC = 3072
C_VC = 3328
C_QIC = 3584
C_UA = 4096
C_VA = 4608
C_MISC = 5120
MISC_LRB = D_IDX
MISC_WIC = D_IDX + GATE_RANK_B
H_WIDTH = 5376

INT_MIN = -2147483648
INT_MAX = 2147483647
SNAP_EVERY = 8
BISECT_AFTER = 40
MAX_SEARCH_STEPS = 80
MASKED = -1e30
LOG2E = 1.4426950408889634
NEAR_SPAN = 128
FAR_KW = 512
PROMPT_TQ = 256
SOFTMAX_ROWS = 64
GLA_SUB = 4


def _pick(n, cands):
    for c in cands:
        if n % c == 0:
            return c
    raise ValueError(f"no tile for {n}")


def _cparams(sem):
    return pltpu.CompilerParams(dimension_semantics=sem, vmem_limit_bytes=VMEM_LIMIT)


def _rms(x):
    return x * lax.rsqrt(jnp.mean(x * x, axis=-1, keepdims=True) + EPS)


def _dot(a, b):
    return jnp.dot(a, b, preferred_element_type=F32)


def _dot_nt(a, b):
    return lax.dot_general(a, b, (((1,), (1,)), ((), ())), preferred_element_type=F32)


def _dot_tn(a, b):
    return lax.dot_general(a, b, (((0,), (0,)), ((), ())), preferred_element_type=F32)


def _inproj_kernel(x_ref, g_ref, w_ref, o_ref, xn_ref):
    @pl.when(pl.program_id(1) == 0)
    def _():
        xn_ref[...] = (_rms(x_ref[...]) * g_ref[...]).astype(BF16)

    o_ref[...] = _dot(xn_ref[...], w_ref[...])


def _inproj(x, g, w, l):
    m = x.shape[0]
    tm = _pick(m, (1024, 512, 256, 128))
    tn = 768
    return pl.pallas_call(
        _inproj_kernel,
        grid=(m // tm, H_WIDTH // tn),
        in_specs=[pl.BlockSpec((tm, D_MODEL), lambda i, j: (i, 0)),
                  pl.BlockSpec((None, 1, D_MODEL), lambda i, j: (l, 0, 0)),
                  pl.BlockSpec((None, D_MODEL, tn), lambda i, j: (l, 0, j))],
        out_specs=pl.BlockSpec((tm, tn), lambda i, j: (i, j)),
        out_shape=jax.ShapeDtypeStruct((m, H_WIDTH), F32),
        scratch_shapes=[pltpu.VMEM((tm, D_MODEL), BF16)],
        compiler_params=_cparams(("arbitrary", "arbitrary")),
        name="inproj",
    )(x, g, w)


def _gmlp_kernel(h_ref, gv_ref, w_ref, b_ref, ya_ref, vn_ref, *, n_prompt_blocks):
    i = pl.program_id(0)
    hv = h_ref[...]
    u = jax.nn.gelu(hv[:, :WIDTH_A])
    v = jax.nn.gelu(hv[:, WIDTH_A:])
    r = lax.broadcasted_iota(I32, (GROUP_A, GROUP_A), 0)
    c = lax.broadcasted_iota(I32, (GROUP_A, GROUP_A), 1)
    samp = (i >= n_prompt_blocks).astype(I32)
    keep = (c <= r) & (((r >> 6) * samp) == ((c >> 6) * samp))
    for g in range(N_GROUPS_A):
        sl = slice(g * GROUP_A, (g + 1) * GROUP_A)
        vn = _rms(v[:, sl]) * gv_ref[:, sl]
        vn_ref[:, sl] = vn
        wm = jnp.where(keep, w_ref[0, g], 0.0).astype(BF16)
        z = _dot(wm, vn.astype(BF16)) + b_ref[0][:, g:g + 1]
        ya_ref[:, sl] = (u[:, sl] * z).astype(BF16)


def _gmlp(h, gv, w2, b2, n_prompt_rows, l):
    m = h.shape[0]
    npb = n_prompt_rows // GROUP_A
    nsb = (m - n_prompt_rows) // GROUP_A
    return pl.pallas_call(
        functools.partial(_gmlp_kernel, n_prompt_blocks=npb),
        grid=(npb + nsb,),
        in_specs=[pl.BlockSpec((GROUP_A, 2 * WIDTH_A), lambda i: (i, C_UA // (2 * WIDTH_A))),
                  pl.BlockSpec((None, 1, WIDTH_A), lambda i: (l, 0, 0)),
                  pl.BlockSpec((None, 1, N_GROUPS_A, GROUP_A, GROUP_A),
                               lambda i: (l, jnp.where(i >= npb, 1, 0), 0, 0, 0)),
                  pl.BlockSpec((None, 1, GROUP_A, N_GROUPS_A),
                               lambda i: (l, jnp.where(i >= npb, 1, 0), 0, 0))],
        out_specs=[pl.BlockSpec((GROUP_A, WIDTH_A), lambda i: (i, 0)),
                   pl.BlockSpec((GROUP_A, WIDTH_A), lambda i: (jnp.maximum(i - npb, 0), 0))],
        out_shape=[jax.ShapeDtypeStruct((m, WIDTH_A), BF16),
                   jax.ShapeDtypeStruct((nsb * GROUP_A, WIDTH_A), F32)],
        compiler_params=_cparams(("arbitrary",)),
        name="gmlp",
    )(h, gv, w2, b2)


def _gla_kernel(q_ref, k_ref, v_ref, og_ref, misc_ref, wg_ref, bg_ref, gbo_ref, s0_ref,
                yb_ref, sout_ref, st_ref, *, n_prompt_steps):
    j = pl.program_id(0)
    rows = GLA_SUB * CHUNK
    fresh = j >= n_prompt_steps

    lrb = misc_ref[:, MISC_LRB:MISC_LRB + GATE_RANK_B].astype(BF16)
    x = _dot(lrb, wg_ref[...]) + bg_ref[...]
    g = (jnp.minimum(x, 0.0) - jnp.log1p(jnp.exp(-jnp.abs(x)))) * (1.0 / GATE_TEMP_B)
    rr = lax.broadcasted_iota(I32, (rows, rows), 0)
    cc = lax.broadcasted_iota(I32, (rows, rows), 1)
    same = (rr >> 6) == (cc >> 6)
    tril = same & (cc <= rr)
    g1 = g.astype(BF16)
    r1 = g - g1.astype(F32)
    g2 = r1.astype(BF16)
    g3 = (r1 - g2.astype(F32)).astype(BF16)

    def chunk_sums(mask):
        m = jnp.where(mask, 1.0, 0.0).astype(BF16)
        return _dot(m, g1) + _dot(m, g2) + _dot(m, g3)

    b = chunk_sums(tril)
    b_last = chunk_sums(same)
    b_mid = chunk_sums(same & ((cc & (CHUNK - 1)) < CHUNK // 2))
    qs = q_ref[...] * (DK_B ** -0.5)
    kk = k_ref[...]
    q_inter = (qs * jnp.exp(b)).astype(BF16)
    q_intra = (qs * jnp.exp(b - b_mid)).astype(BF16)
    k_intra = (kk * jnp.exp(b_mid - b)).astype(BF16)
    k_state = (kk * jnp.exp(b_last - b)).astype(BF16)
    dec = jnp.exp(b_last)
    for h in range(N_HEADS_B):
        sk = slice(h * DK_B, (h + 1) * DK_B)
        sv = slice(h * DV_B, (h + 1) * DV_B)
        vh = v_ref[:, sv].astype(BF16)
        att = jnp.where(tril, _dot_nt(q_intra[:, sk], k_intra[:, sk]), 0.0)
        o_intra = _dot(att.astype(BF16), vh)
        st = st_ref[h]
        o_inter = []
        for c in range(GLA_SUB):
            rc = slice(c * CHUNK, (c + 1) * CHUNK)
            start = fresh | ((j == 0) & (c == 0))
            st = jnp.where(start, s0_ref[c, h].T, st)
            o_inter.append(_dot_nt(q_inter[rc, sk], st.astype(BF16)))
            st = st * dec[c * CHUNK:c * CHUNK + 1, sk] + _dot_tn(vh[rc], k_state[rc, sk])
            sout_ref[c, h] = st.T
        st_ref[h] = st
        o = jnp.concatenate(o_inter, axis=0) + o_intra
        on = _rms(o) * gbo_ref[:, sv]
        yb_ref[:, sv] = (on * jax.nn.silu(og_ref[:, sv])).astype(BF16)


def _gla(h, wg, bg, gbo, s_in, n_prompt_rows, l):
    m = h.shape[0]
    rows = GLA_SUB * CHUNK
    nps = n_prompt_rows // rows
    nslots = s_in.shape[1]

    def blk(j):
        return jnp.where(j < nps, 0, j - nps + 1)

    return pl.pallas_call(
        functools.partial(_gla_kernel, n_prompt_steps=nps),
        grid=(m // rows,),
        in_specs=[pl.BlockSpec((rows, QK_B), lambda j: (j, C_QB // QK_B)),
                  pl.BlockSpec((rows, QK_B), lambda j: (j, C_KB // QK_B)),
                  pl.BlockSpec((rows, WIDTH_B), lambda j: (j, C_VB // WIDTH_B)),
                  pl.BlockSpec((rows, WIDTH_B), lambda j: (j, C_OG // WIDTH_B)),
                  pl.BlockSpec((rows, LANES), lambda j: (j, C_MISC // LANES)),
                  pl.BlockSpec((None, GATE_RANK_B, QK_B), lambda j: (l, 0, 0)),
                  pl.BlockSpec((None, 1, QK_B), lambda j: (l, 0, 0)),
                  pl.BlockSpec((None, 1, WIDTH_B), lambda j: (l, 0, 0)),
                  pl.BlockSpec((None, GLA_SUB, N_HEADS_B, DK_B, DV_B), lambda j: (l, blk(j), 0, 0, 0))],
        out_specs=[pl.BlockSpec((rows, WIDTH_B), lambda j: (j, 0)),
                   pl.BlockSpec((GLA_SUB, N_HEADS_B, DK_B, DV_B), lambda j: (blk(j), 0, 0, 0))],
        out_shape=[jax.ShapeDtypeStruct((m, WIDTH_B), BF16),
                   jax.ShapeDtypeStruct((nslots, N_HEADS_B, DK_B, DV_B), F32)],
        scratch_shapes=[pltpu.VMEM((N_HEADS_B, DV_B, DK_B), F32)],
        compiler_params=_cparams(("arbitrary",)),
        name="gla",
    )(h, h, h, h, h, wg, bg, gbo, s_in)


def _fold_rows(x, op):
    tile = SUBLANES * (4 // x.dtype.itemsize)
    parts = [x[j * tile:(j + 1) * tile] for j in range(x.shape[0] // tile)]
    while len(parts) > 1:
        nxt = [op(parts[j], parts[j + 1]) for j in range(0, len(parts) - 1, 2)]
        parts = nxt + parts[len(parts) - len(parts) % 2:]
    return parts[0]


def _loop(n, body, init):
    if isinstance(n, int):
        carry = init
        for c in range(n):
            carry = body(c, carry)
        return carry
    return lax.fori_loop(0, n, body, init)


def _dsa_core(q, qi, misc_t, far_ki, far_k, far_vt, nf, far_adm, near_ki, near_k, near_vt, near_adm, n_adm,
              bias_near_ref, out_ref, scratch, *, tq, kw, nw, topk):
    fkeys_ref, nkeys_ref, m_ref, acc_ref, ss_ref, p_ref = scratch
    rep = N_HEADS_C // N_KV_C
    groups = range(N_KV_C)
    sls = [slice(g * HD_C, (g + 1) * HD_C) for g in groups]
    qi_s = (qi * (D_IDX ** -0.5)).astype(BF16)
    qi_all = jnp.concatenate([qi_s[:, h * D_IDX:(h + 1) * D_IDX] for h in range(N_IDX_HEADS)], axis=0)
    wi_t = misc_t[MISC_WIC:MISC_WIC + N_IDX_HEADS, :] * (N_IDX_HEADS ** -0.5)
    qs = q * (HD_C ** -0.5 * LOG2E)
    q_g = [jnp.concatenate([qs[:, (g * rep + r) * HD_C:(g * rep + r + 1) * HD_C] for r in range(rep)],
                           axis=0).astype(BF16) for g in groups]
    kf = float(topk)

    def keys_of(ki_blk, adm):
        d = _dot_nt(ki_blk, qi_all)
        acc = None
        for h in range(N_IDX_HEADS):
            t = wi_t[h:h + 1, :] * jnp.maximum(d[:, h * tq:(h + 1) * tq], 0.0)
            acc = t if acc is None else acc + t
        bits = lax.bitcast_convert_type(acc, I32)
        key = bits ^ ((bits >> 31) & 0x7FFFFFFF)
        key = jnp.where(acc == 0.0, 0, key)
        if adm is not None:
            key = jnp.where(adm, key, INT_MIN)
        return key

    def extremes(key, carry):
        kmax, kmin = carry
        kmax = jnp.maximum(kmax, _fold_rows(key, jnp.maximum))
        kmin = jnp.minimum(kmin, _fold_rows(jnp.where(key == INT_MIN, INT_MAX, key), jnp.minimum))
        return kmax, kmin

    def fill(c, carry):
        key = keys_of(far_ki(c), None if far_adm is None else far_adm(c))
        fkeys_ref[c] = key
        return extremes(key, carry)

    carry = _loop(nf, fill, (jnp.full((SUBLANES, tq), INT_MIN, I32), jnp.full((SUBLANES, tq), INT_MAX, I32)))
    key = keys_of(near_ki, near_adm)
    nkeys_ref[...] = key
    kmax, kmin = extremes(key, carry)
    kmax = jnp.max(kmax, axis=0, keepdims=True)
    kmin = jnp.min(kmin, axis=0, keepdims=True)

    def count(thr, strict):
        def cmp(x):
            hit = (x > thr) if strict else (x >= thr)
            return _fold_rows(jnp.where(hit, 1.0, 0.0), jnp.add)

        part = _loop(nf, lambda c, a: a + cmp(fkeys_ref[c]), jnp.zeros((SUBLANES, tq), F32))
        part = part + cmp(nkeys_ref[...])
        return jnp.sum(part, axis=0, keepdims=True)

    def bits_flip(x):
        return x ^ ((x >> 31) & 0x7FFFFFFF)

    def any_lane(flag):
        return jnp.max(jnp.where(flag, 1.0, 0.0))

    def snap(lo, hi):
        def pull(x, carry):
            kin, kax = carry
            kin = jnp.minimum(kin, _fold_rows(jnp.where(x >= lo, x, INT_MAX), jnp.minimum))
            kax = jnp.maximum(kax, _fold_rows(jnp.where(x < hi, x, INT_MIN), jnp.maximum))
            return kin, kax

        init = (jnp.full((SUBLANES, tq), INT_MAX, I32), jnp.full((SUBLANES, tq), INT_MIN, I32))
        kin, kax = pull(nkeys_ref[...], _loop(nf, lambda c, carry: pull(fkeys_ref[c], carry), init))
        return jnp.min(kin, axis=0, keepdims=True), jnp.max(kax, axis=0, keepdims=True)

    enough = n_adm >= kf
    log_k = math.log2(topk)

    def open_lanes(lo, hi, flo):
        return enough & (flo > kf) & (hi > lo + 1)

    def log_excess(c):
        return jnp.log2(jnp.maximum(c, 0.5)) - log_k

    def step(state):
        it, _, lo, hi, flo, fhi, glo, ghi, side = state
        active = open_lanes(lo, hi, flo)

        def pull_in(_):
            kin, kax = snap(lo, hi)
            return jnp.where(active, kin, lo), jnp.where(active, kax + 1, hi), flo, fhi, glo, ghi, side

        def probe(_):
            v_lo = lax.bitcast_convert_type(bits_flip(lo), F32)
            v_hi = lax.bitcast_convert_type(bits_flip(hi), F32)
            frac = jnp.minimum(jnp.maximum(glo / (glo - ghi), 0.02), 0.98)
            t_int = bits_flip(lax.bitcast_convert_type(v_lo + (v_hi - v_lo) * frac, I32))
            t_mid = (lo & hi) + ((lo ^ hi) >> 1)
            t = jnp.where(it >= BISECT_AFTER, t_mid, t_int)
            t = jnp.minimum(jnp.maximum(t, lo + 1), hi - 1)
            c = count(t, False)
            g = log_excess(c)
            up = active & (c >= kf)
            dn = active & (c < kf)
            ghi2 = jnp.where(up & (side > 0.0), ghi * 0.5, ghi)
            glo2 = jnp.where(dn & (side < 0.0), glo * 0.5, glo)
            return (jnp.where(up, t, lo), jnp.where(dn, t, hi), jnp.where(up, c, flo), jnp.where(dn, c, fhi),
                    jnp.where(up, g, glo2), jnp.where(dn, g, ghi2), jnp.where(up, 1.0, jnp.where(dn, -1.0, side)))

        is_snap = (it % SNAP_EVERY == SNAP_EVERY - 1) & (it < BISECT_AFTER)
        lo, hi, flo, fhi, glo, ghi, side = lax.cond(is_snap, pull_in, probe, 0)
        return it + 1, any_lane(open_lanes(lo, hi, flo)), lo, hi, flo, fhi, glo, ghi, side

    hi0 = kmax + 1
    zero = jnp.zeros((1, tq), F32)
    state = (jnp.int32(0), any_lane(open_lanes(kmin, hi0, n_adm)), kmin, hi0, n_adm, zero,
             log_excess(n_adm), log_excess(zero), zero)
    state = lax.while_loop(lambda st: (st[1] > 0.0) & (st[0] < MAX_SEARCH_STEPS), step, state)
    thr = jnp.where(enough, state[2], INT_MIN)
    has_tie = any_lane(enough & (state[4] > kf)) > 0.0

    @pl.when(has_tie)
    def _():
        need = kf - count(thr, True)

        def lower(n):
            a = lax.broadcasted_iota(I32, (n, n), 0)
            b = lax.broadcasted_iota(I32, (n, n), 1)
            return jnp.where(b < a, 1.0, 0.0).astype(BF16)

        def demote(keys, run, lt):
            eq = (keys == thr) & (keys > INT_MIN)
            eqf = jnp.where(eq, 1.0, 0.0)
            before = _dot(lt, eqf.astype(BF16)) + run
            keys = jnp.where(eq & (before >= need), INT_MIN, keys)
            return keys, run + jnp.sum(eqf, axis=0, keepdims=True)

        lt_far = lower(kw)

        def step(c, run):
            keys, run = demote(fkeys_ref[c], run, lt_far)
            fkeys_ref[c] = keys
            return run

        run = _loop(nf, step, jnp.zeros((1, tq), F32))
        keys, _ = demote(nkeys_ref[...], run, lower(nw))
        nkeys_ref[...] = keys

    thr_sel = jnp.maximum(thr, INT_MIN + 1)
    m_ref[...] = jnp.full(m_ref.shape, MASKED, F32)
    acc_ref[...] = jnp.zeros(acc_ref.shape, F32)

    def logits(k_blk):
        return [_dot_nt(k_blk[:, sls[g]], q_g[g]) for g in groups]

    def attend(slot, keys_rows, vt_blk, bias_ref, n):
        tiles = [slice(r0, r0 + SOFTMAX_ROWS) for r0 in range(0, n, SOFTMAX_ROWS)]
        heads = [slice(r * tq, (r + 1) * tq) for r in range(rep)]

        def masked_logits(g, rows, selb):
            s = ss_ref[slot, g, rows, :]
            if bias_ref is not None:
                s = s + bias_ref[g, rows, :]
            return [s[:, hd] + selb for hd in heads]

        run = [None] * N_KV_C
        for rows in tiles:
            selb = jnp.where(keys_rows(rows) >= thr_sel, 0.0, MASKED)
            for g in groups:
                top = jnp.concatenate([_fold_rows(s, jnp.maximum) for s in masked_logits(g, rows, selb)], axis=1)
                run[g] = top if run[g] is None else jnp.maximum(run[g], top)
        m_new = []
        for g in groups:
            m_old = m_ref[g]
            m_new.append(jnp.maximum(m_old, jnp.max(run[g], axis=0, keepdims=True)))
            acc_ref[g] = jnp.exp2(m_old - m_new[g]) * acc_ref[g]
            m_ref[g] = m_new[g]
        for rows in tiles:
            selb = jnp.where(keys_rows(rows) >= thr_sel, 0.0, MASKED)
            for g in groups:
                for hd, s in zip(heads, masked_logits(g, rows, selb)):
                    p_ref[g, rows, hd] = jnp.exp2(s - m_new[g][:, hd]).astype(BF16)
        ones = jnp.ones((ONES_ROWS, n), BF16)
        for g in groups:
            acc_ref[g] += _dot(jnp.concatenate([vt_blk[sls[g], :], ones], axis=0), p_ref[g, :n, :])

    last = fkeys_ref.shape[0] - 1

    def put_logits(slot, k_blk, n):
        for g, s in enumerate(logits(k_blk)):
            ss_ref[slot, g, :n, :] = s

    def far_body(c, cur):
        put_logits(1 - cur, far_k(min(c + 1, last) if isinstance(c, int) else jnp.minimum(c + 1, last)), kw)
        attend(cur, lambda rows: fkeys_ref[c, rows, :], far_vt(c), None, kw)

    def far_step(c, carry):
        if isinstance(c, int):
            far_body(c, c % 2)
        else:
            for cur in range(2):
                pl.when(c % 2 == cur)(functools.partial(far_body, c, cur))
        return carry

    put_logits(0, far_k(0), kw)
    _loop(nf, far_step, 0)
    put_logits(0, near_k, nw)
    attend(0, lambda rows: nkeys_ref[rows, :], near_vt, bias_near_ref, nw)

    for g in groups:
        a = acc_ref[g]
        o = a[:HD_C] / a[HD_C:HD_C + 1]
        for r in range(rep):
            hh = g * rep + r
            out_ref[:, hh * HD_C:(hh + 1) * HD_C] = o[:, r * tq:(r + 1) * tq].T.astype(BF16)


def _dsa_scratch(nf, tq, kw, nw):
    rep = N_HEADS_C // N_KV_C
    return [pltpu.VMEM((nf, kw, tq), I32),
            pltpu.VMEM((nw, tq), I32),
            pltpu.VMEM((N_KV_C, 1, rep * tq), F32),
            pltpu.VMEM((N_KV_C, HD_C + ONES_ROWS, rep * tq), F32),
            pltpu.VMEM((2, N_KV_C, max(kw, nw), rep * tq), F32),
            pltpu.VMEM((N_KV_C, max(kw, nw), rep * tq), BF16)]


def _dsa_prompt_kernel(q_ref, qi_ref, misc_ref, k_ref, vt_far_ref, vt_near_ref, ki_ref, bias_near_ref,
                       out_ref, *scratch, tq, topk):
    i = pl.program_id(0)
    kw = FAR_KW
    nw = NEAR_SPAN + tq
    q0 = i * tq
    far_limit = q0 - NEAR_SPAN
    near0 = jnp.maximum(far_limit, 0)
    nf = (near0 + kw - 1) // kw

    def rows(c):
        return pl.ds(pl.multiple_of(c * kw, kw), kw)

    def far_adm(c):
        return (c * kw + lax.broadcasted_iota(I32, (kw, tq), 0)) < far_limit

    left = pl.ds(pl.multiple_of(near0, NEAR_SPAN), NEAR_SPAN)
    right = pl.ds(pl.multiple_of(q0, tq), tq)
    near_ki = jnp.concatenate([ki_ref[left, :], ki_ref[right, :]], axis=0)
    near_k = jnp.concatenate([k_ref[left, :], k_ref[right, :]], axis=0)
    per = tq // NEAR_SPAN
    near_vt = jnp.concatenate([vt_near_ref[near0 // NEAR_SPAN]] + [vt_near_ref[i * per + j] for j in range(per)],
                              axis=1)
    key = lax.broadcasted_iota(I32, (nw, tq), 0)
    qry = lax.broadcasted_iota(I32, (nw, tq), 1)
    first_key = jnp.where(i > 0, 0, NEAR_SPAN)
    near_adm = (key >= first_key) & (((key - NEAR_SPAN) >> 6) <= (qry >> 6))
    n_adm = ((((q0 + lax.broadcasted_iota(I32, (1, tq), 1)) >> 6) + 1) * CHUNK).astype(F32)

    _dsa_core(q_ref[...], qi_ref[...], misc_ref[...].T,
              lambda c: ki_ref[rows(c), :], lambda c: k_ref[rows(c), :], lambda c: vt_far_ref[c],
              nf, far_adm, near_ki, near_k, near_vt, near_adm, n_adm,
              bias_near_ref, out_ref, scratch,
              tq=tq, kw=kw, nw=nw, topk=topk)


def _dsa_prompt(h, kb, vt_far, vt_near, kib, bias_near, t):
    tq = PROMPT_TQ
    topk = min(TOPK_MAX, t // 4)
    return pl.pallas_call(
        functools.partial(_dsa_prompt_kernel, tq=tq, topk=topk),
        grid=(t // tq,),
        in_specs=[pl.BlockSpec((tq, WIDTH_C), lambda i: (i, C_QC // WIDTH_C)),
                  pl.BlockSpec((tq, QI_C), lambda i: (i, C_QIC // QI_C)),
                  pl.BlockSpec((tq, LANES), lambda i: (i, C_MISC // LANES)),
                  pl.BlockSpec((t, KV_C), lambda i: (0, 0)),
                  pl.BlockSpec(vt_far.shape, lambda i: (0, 0, 0)),
                  pl.BlockSpec(vt_near.shape, lambda i: (0, 0, 0)),
                  pl.BlockSpec((t, D_IDX), lambda i: (0, 0)),
                  pl.BlockSpec(bias_near.shape, lambda i: (0, 0, 0))],
        out_specs=pl.BlockSpec((tq, WIDTH_C), lambda i: (i, 0)),
        out_shape=jax.ShapeDtypeStruct((t, WIDTH_C), BF16),
        scratch_shapes=_dsa_scratch(t // FAR_KW, tq, FAR_KW, NEAR_SPAN + tq),
        compiler_params=_cparams(("arbitrary",)),
        name="dsa_prompt",
    )(h, h, h, kb, vt_far, vt_near, kib, bias_near)


def _dsa_sample_kernel(q_ref, qi_ref, misc_ref, kvn_ref, pk_ref, pv_ref, pki_ref, bias_near_ref,
                       out_ref, *scratch, kw, nf, topk):
    tq = CHUNK
    far_len = kw * nf
    misc = misc_ref[...]
    kvn = kvn_ref[...]

    def group_rows(ref, start, n, g):
        return ref[0, pl.ds(2 * start + g, n, stride=2), :]

    def past_k(start, n):
        return jnp.concatenate([group_rows(pk_ref, start, n, g) for g in range(N_KV_C)], axis=1)

    def past_vt(start, n):
        return jnp.concatenate([group_rows(pv_ref, start, n, g).T for g in range(N_KV_C)], axis=0)

    near_ki = jnp.concatenate([pki_ref[0, pl.ds(far_len, NEAR_SPAN), :], misc[:, :D_IDX]], axis=0).astype(BF16)
    near_k = jnp.concatenate([past_k(far_len, NEAR_SPAN), kvn[:, :KV_C]], axis=0).astype(BF16)
    near_vt = jnp.concatenate([past_vt(far_len, NEAR_SPAN), kvn[:, KV_C:].T], axis=1).astype(BF16)

    _dsa_core(q_ref[...], qi_ref[...], misc.T,
              lambda c: pki_ref[0, pl.ds(c * kw, kw), :].astype(BF16),
              lambda c: past_k(c * kw, kw).astype(BF16), lambda c: past_vt(c * kw, kw).astype(BF16),
              nf, None, near_ki, near_k, near_vt, None, jnp.full((1, tq), float(far_len + NEAR_SPAN + CHUNK), F32),
              bias_near_ref, out_ref, scratch,
              tq=tq, kw=kw, nw=NEAR_SPAN + CHUNK, topk=topk)


def _dsa_sample(h, pk, pv, pki, bias_near, t, l):
    nb, p = pki.shape[1], pki.shape[2]
    tq = CHUNK
    far_len = p - NEAR_SPAN
    kw = _pick(far_len, (640, 512, 384, 256, 128))
    nf = far_len // kw
    nw = NEAR_SPAN + CHUNK
    topk = min(TOPK_MAX, (p + CHUNK) // 4)
    row0 = t // tq
    return pl.pallas_call(
        functools.partial(_dsa_sample_kernel, kw=kw, nf=nf, topk=topk),
        grid=(nb,),
        in_specs=[pl.BlockSpec((tq, WIDTH_C), lambda b: (row0 + b, C_QC // WIDTH_C)),
                  pl.BlockSpec((tq, QI_C), lambda b: (row0 + b, C_QIC // QI_C)),
                  pl.BlockSpec((tq, LANES), lambda b: (row0 + b, C_MISC // LANES)),
                  pl.BlockSpec((tq, 2 * KV_C), lambda b: (row0 + b, C_KC // (2 * KV_C))),
                  pl.BlockSpec((None, 1, N_KV_C * p, HD_C), lambda b: (l, b, 0, 0)),
                  pl.BlockSpec((None, 1, N_KV_C * p, HD_C), lambda b: (l, b, 0, 0)),
                  pl.BlockSpec((None, 1, p, D_IDX), lambda b: (l, b, 0, 0)),
                  pl.BlockSpec(bias_near.shape, lambda b: (0, 0, 0))],
        out_specs=pl.BlockSpec((tq, WIDTH_C), lambda b: (b, 0)),
        out_shape=jax.ShapeDtypeStruct((nb * tq, WIDTH_C), BF16),
        scratch_shapes=_dsa_scratch(nf, tq, kw, nw),
        compiler_params=_cparams(("arbitrary",)),
        name="dsa_sample",
    )(h, h, h, h, pk, pv, pki, bias_near)


def _outproj_kernel(x_ref, ya_ref, yb_ref, yc_ref, wa_ref, wb_ref, wc_ref, o_ref):
    o_ref[...] = (x_ref[...] + _dot(ya_ref[...], wa_ref[...]) + _dot(yb_ref[...], wb_ref[...])
                  + _dot(yc_ref[...], wc_ref[...]))


def _outproj(x, ya, yb, yc, wa, wb, wc, l):
    m = x.shape[0]
    tm = _pick(m, (1024, 512, 256, 128))
    tn = 512
    return pl.pallas_call(
        _outproj_kernel,
        grid=(m // tm, D_MODEL // tn),
        in_specs=[pl.BlockSpec((tm, tn), lambda i, j: (i, j)),
                  pl.BlockSpec((tm, WIDTH_A), lambda i, j: (i, 0)),
                  pl.BlockSpec((tm, WIDTH_B), lambda i, j: (i, 0)),
                  pl.BlockSpec((tm, WIDTH_C), lambda i, j: (i, 0)),
                  pl.BlockSpec((None, WIDTH_A, tn), lambda i, j: (l, 0, j)),
                  pl.BlockSpec((None, WIDTH_B, tn), lambda i, j: (l, 0, j)),
                  pl.BlockSpec((None, WIDTH_C, tn), lambda i, j: (l, 0, j))],
        out_specs=pl.BlockSpec((tm, tn), lambda i, j: (i, j)),
        out_shape=jax.ShapeDtypeStruct((m, D_MODEL), F32),
        compiler_params=_cparams(("arbitrary", "arbitrary")),
        name="outproj",
    )(x, ya, yb, yc, wa, wb, wc)


def _ffn_kernel(x_ref, g_ref, wu_ref, wd_ref, o_ref, xn_ref, acc_ref):
    f = pl.program_id(1)

    @pl.when(f == 0)
    def _():
        xn_ref[...] = (_rms(x_ref[...]) * g_ref[...]).astype(BF16)
        acc_ref[...] = jnp.zeros(acc_ref.shape, F32)

    a = jnp.maximum(_dot(xn_ref[...], wu_ref[...]), 0.0)
    acc_ref[...] += _dot((a * a).astype(BF16), wd_ref[...])

    @pl.when(f == pl.num_programs(1) - 1)
    def _():
        o_ref[...] = x_ref[...] + acc_ref[...]


def _ffn(x, g, wu, wd, l):
    m = x.shape[0]
    tm = _pick(m, (512, 256, 128))
    tf = 512
    return pl.pallas_call(
        _ffn_kernel,
        grid=(m // tm, D_FF // tf),
        in_specs=[pl.BlockSpec((tm, D_MODEL), lambda i, f: (i, 0)),
                  pl.BlockSpec((None, 1, D_MODEL), lambda i, f: (l, 0, 0)),
                  pl.BlockSpec((None, D_MODEL, tf), lambda i, f: (l, 0, f)),
                  pl.BlockSpec((None, tf, D_MODEL), lambda i, f: (l, f, 0))],
        out_specs=pl.BlockSpec((tm, D_MODEL), lambda i, f: (i, 0)),
        out_shape=jax.ShapeDtypeStruct((m, D_MODEL), F32),
        scratch_shapes=[pltpu.VMEM((tm, D_MODEL), BF16), pltpu.VMEM((tm, D_MODEL), F32)],
        compiler_params=_cparams(("arbitrary", "arbitrary")),
        name="ffn",
    )(x, g, wu, wd)


def _norm_kernel(x_ref, g_ref, o_ref):
    o_ref[...] = _rms(x_ref[...]) * g_ref[...]


def _final_norm(x, g):
    m = x.shape[0]
    tm = _pick(m, (1024, 512, 256, 128))
    return pl.pallas_call(
        _norm_kernel,
        grid=(m // tm,),
        in_specs=[pl.BlockSpec((tm, D_MODEL), lambda i: (i, 0)),
                  pl.BlockSpec((1, D_MODEL), lambda i: (0, 0))],
        out_specs=pl.BlockSpec((tm, D_MODEL), lambda i: (i, 0)),
        out_shape=jax.ShapeDtypeStruct((m, D_MODEL), F32),
        compiler_params=_cparams(("arbitrary",)),
        name="final_norm",
    )(x, g)


def _t5_bucket(rel):
    half = N_BUCKETS // 2
    max_exact = half // 2
    n = jnp.abs(rel)
    nf = jnp.maximum(n, 1).astype(F32)
    large = max_exact + (jnp.log(nf / max_exact) / jnp.log(MAX_DISTANCE / max_exact)
                         * (half - max_exact)).astype(I32)
    large = jnp.minimum(large, half - 1)
    return jnp.where(rel > 0, half, 0) + jnp.where(n < max_exact, n, large)


def _bias_table(rel_bias, tq, nw):
    j = jnp.arange(nw, dtype=I32)[:, None]
    t = jnp.arange(tq, dtype=I32)[None, :]
    hot = jax.nn.one_hot(_t5_bucket(j - NEAR_SPAN - t), N_BUCKETS, dtype=F32)
    near = jnp.einsum("jtb,bh->jth", hot, rel_bias, precision=lax.Precision.HIGHEST)
    far = rel_bias[_t5_bucket(jnp.int32(-(NEAR_SPAN + 1)))]
    rep = N_HEADS_C // N_KV_C
    near = ((near - far) * LOG2E).astype(F32).reshape(nw, tq, N_KV_C, rep)
    return jnp.transpose(near, (2, 0, 3, 1)).reshape(N_KV_C, nw, rep * tq)


def _permute_w_in(w):
    sizes = (WIDTH_A, WIDTH_A, QK_B, QK_B, WIDTH_B, GATE_RANK_B, WIDTH_B, WIDTH_C, KV_C, KV_C, QI_C,
             D_IDX, N_IDX_HEADS)
    parts, off = [], 0
    for n in sizes:
        parts.append(w[..., off:off + n])
        off += n
    ua, va, qb, kb, vb, lrb, og, qc, kc, vc, qic, kic, wic = parts
    pad = jnp.zeros(w.shape[:-1] + (H_WIDTH - (C_MISC + D_IDX + GATE_RANK_B + N_IDX_HEADS),), w.dtype)
    return jnp.concatenate([qb, kb, vb, og, qc, kc, vc, qic, ua, va, kic, lrb, wic, pad], axis=-1).astype(BF16)


def kernel(x_prompt, x_sample, cache_c_k, cache_c_v, cache_c_kidx, state_b_s, norm_mix, w_in, norm_a_v,
           w_s_a, b_s_a, w_gate_b, b_gate_b, norm_b_o, rel_bias, w_o, norm_ffn, w_up, w_down, norm_final):
    depth = w_in.shape[0]
    nbp, t, _ = x_prompt.shape
    nbs, ts, _ = x_sample.shape
    p = cache_c_k.shape[2]
    assert nbp == 1 and ts == CHUNK and t % FAR_KW == 0 and (nbs * ts) % GROUP_A == 0
    assert p % NEAR_SPAN == 0 and p >= 2 * NEAR_SPAN
    assert t % PROMPT_TQ == 0 and t % (GLA_SUB * CHUNK) == 0 and nbs % GLA_SUB == 0

    x = jnp.concatenate([x_prompt[0], x_sample.reshape(nbs * ts, D_MODEL)], axis=0)
    w_in_p = _permute_w_in(w_in)
    wo_a = w_o[:, :WIDTH_A].astype(BF16)
    wo_b = w_o[:, WIDTH_A:WIDTH_A + WIDTH_B].astype(BF16)
    wo_c = w_o[:, WIDTH_A + WIDTH_B:].astype(BF16)
    w_up_b = w_up.astype(BF16)
    w_down_b = w_down.astype(BF16)
    w_gate = w_gate_b.astype(BF16)
    half = GROUP_A // 2
    w_s2 = jnp.stack([w_s_a, jnp.tile(w_s_a[:, :, :half, :half], (1, 1, 2, 2))], axis=1)
    b_s2 = jnp.stack([b_s_a, jnp.tile(b_s_a[:, :, :half], (1, 1, 2))], axis=1)
    b_s2 = jnp.swapaxes(b_s2, 2, 3)
    s_all = jnp.concatenate([jnp.zeros((depth, GLA_SUB) + state_b_s.shape[2:], F32), state_b_s], axis=1)
    bias_near = _bias_table(rel_bias, PROMPT_TQ, NEAR_SPAN + PROMPT_TQ)
    bias_near_s = _bias_table(rel_bias, CHUNK, NEAR_SPAN + CHUNK)
    pk_all = cache_c_k.reshape(depth, nbs, N_KV_C * p, HD_C)
    pv_all = cache_c_v.reshape(depth, nbs, N_KV_C * p, HD_C)

    row = lambda a: a[:, None, :]
    outs = [[] for _ in range(9)]
    for l in range(depth):
        h = _inproj(x, row(norm_mix), w_in_p, l)
        hp, hs = h[:t], h[t:]
        ya, vn = _gmlp(h, row(norm_a_v), w_s2, b_s2, t, l)
        yb, s_out = _gla(h, w_gate, row(b_gate_b), row(norm_b_o), s_all, t, l)
        kb = hp[:, C_KC:C_KC + KV_C].astype(BF16)
        vt = hp[:, C_VC:C_VC + KV_C].astype(BF16).T
        vt_far = jnp.swapaxes(vt.reshape(KV_C, t // FAR_KW, FAR_KW), 0, 1)
        vt_near = jnp.swapaxes(vt.reshape(KV_C, t // NEAR_SPAN, NEAR_SPAN), 0, 1)
        kib = hp[:, C_MISC:C_MISC + D_IDX].astype(BF16)
        yc_p = _dsa_prompt(h, kb, vt_far, vt_near, kib, bias_near, t)
        yc_s = _dsa_sample(h, pk_all, pv_all, cache_c_kidx, bias_near_s, t, l)
        yc = jnp.concatenate([yc_p, yc_s], axis=0)
        x = _outproj(x, ya, yb, yc, wo_a, wo_b, wo_c, l)
        x = _ffn(x, row(norm_ffn), w_up_b, w_down_b, l)

        outs[0].append(hp[:, C_KC:C_KC + KV_C].reshape(1, t, N_KV_C, HD_C))
        outs[1].append(hp[:, C_VC:C_VC + KV_C].reshape(1, t, N_KV_C, HD_C))
        outs[2].append(hp[:, C_MISC:C_MISC + D_IDX].reshape(1, t, D_IDX))
        outs[3].append(s_out[GLA_SUB - 1:GLA_SUB])
        outs[4].append(hs[:, C_KC:C_KC + KV_C].reshape(nbs, ts, N_KV_C, HD_C))
        outs[5].append(hs[:, C_VC:C_VC + KV_C].reshape(nbs, ts, N_KV_C, HD_C))
        outs[6].append(hs[:, C_MISC:C_MISC + D_IDX].reshape(nbs, ts, D_IDX))
        outs[7].append(s_out[GLA_SUB:])
        outs[8].append(vn.reshape(nbs, ts, WIDTH_A))

    y = _final_norm(x, norm_final[None])
    return (y[:t][None], y[t:].reshape(nbs, ts, D_MODEL)) + tuple(jnp.stack(o) for o in outs)
```

```python
import functools
import math

import jax
import jax.numpy as jnp
from jax import lax
from jax.experimental import pallas as pl
from jax.experimental.pallas import tpu as pltpu

BF16 = jnp.bfloat16
F32 = jnp.float32
I32 = jnp.int32

D_MODEL = 2048
EPS = 1e-6
CHUNK = 64
GROUP_A = 128
N_GROUPS_A = 4
WIDTH_A = 512
N_HEADS_B = 6
DK_B = 64
DV_B = 128
GATE_RANK_B = 16
GATE_TEMP_B = 16.0
WIDTH_B = 768
N_HEADS_C = 6
N_KV_C = 2
HD_C = 128
N_IDX_HEADS = 8
D_IDX = 64
TOPK_MAX = 256
WIDTH_C = 768
N_BUCKETS = 32
MAX_DISTANCE = 128
D_FF = 4 * D_MODEL
QK_B = N_HEADS_B * DK_B
KV_C = N_KV_C * HD_C
QI_C = N_IDX_HEADS * D_IDX

LANES = 128
SUBLANES = 8
ONES_ROWS = 16
VMEM_LIMIT = 56 * 1024 * 1024

C_QB = 0
C_KB = 384
C_VB = 768
C_OG = 1536
C_QC = 2304
C_KC = 3072
C_VC = 3328
C_QIC = 3584
C_UA = 4096
C_VA = 4608
C_MISC = 5120
MISC_LRB = D_IDX
MISC_WIC = D_IDX + GATE_RANK_B
H_WIDTH = 5376

INT_MIN = -2147483648
INT_MAX = 2147483647
SNAP_EVERY = 8
BISECT_AFTER = 40
MAX_SEARCH_STEPS = 80
MASKED = -1e30
LOG2E = 1.4426950408889634
NEAR_SPAN = 128
FAR_KW = 512
PROMPT_TQ = 256
SOFTMAX_ROWS = 64
GLA_SUB = 4


def _pick(n, cands):
    for c in cands:
        if n % c == 0:
            return c
    raise ValueError(f"no tile for {n}")


def _cparams(sem):
    return pltpu.CompilerParams(dimension_semantics=sem, vmem_limit_bytes=VMEM_LIMIT)


def _rms(x):
    return x * lax.rsqrt(jnp.mean(x * x, axis=-1, keepdims=True) + EPS)


def _dot(a, b):
    return jnp.dot(a, b, preferred_element_type=F32)


def _dot_nt(a, b):
    return lax.dot_general(a, b, (((1,), (1,)), ((), ())), preferred_element_type=F32)


def _dot_tn(a, b):
    return lax.dot_general(a, b, (((0,), (0,)), ((), ())), preferred_element_type=F32)


def _inproj_kernel(x_ref, g_ref, w_ref, o_ref, xn_ref):
    @pl.when(pl.program_id(1) == 0)
    def _():
        xn_ref[...] = (_rms(x_ref[...]) * g_ref[...]).astype(BF16)

    o_ref[...] = _dot(xn_ref[...], w_ref[...])


def _inproj(x, g, w, l):
    m = x.shape[0]
    tm = _pick(m, (1024, 512, 256, 128))
    tn = 768
    return pl.pallas_call(
        _inproj_kernel,
        grid=(m // tm, H_WIDTH // tn),
        in_specs=[pl.BlockSpec((tm, D_MODEL), lambda i, j: (i, 0)),
                  pl.BlockSpec((None, 1, D_MODEL), lambda i, j: (l, 0, 0)),
                  pl.BlockSpec((None, D_MODEL, tn), lambda i, j: (l, 0, j))],
        out_specs=pl.BlockSpec((tm, tn), lambda i, j: (i, j)),
        out_shape=jax.ShapeDtypeStruct((m, H_WIDTH), F32),
        scratch_shapes=[pltpu.VMEM((tm, D_MODEL), BF16)],
        compiler_params=_cparams(("arbitrary", "arbitrary")),
        name="inproj",
    )(x, g, w)


def _gmlp_kernel(h_ref, gv_ref, w_ref, b_ref, ya_ref, vn_ref, *, n_prompt_blocks):
    i = pl.program_id(0)
    hv = h_ref[...]
    u = jax.nn.gelu(hv[:, :WIDTH_A])
    v = jax.nn.gelu(hv[:, WIDTH_A:])
    r = lax.broadcasted_iota(I32, (GROUP_A, GROUP_A), 0)
    c = lax.broadcasted_iota(I32, (GROUP_A, GROUP_A), 1)
    samp = (i >= n_prompt_blocks).astype(I32)
    keep = (c <= r) & (((r >> 6) * samp) == ((c >> 6) * samp))
    for g in range(N_GROUPS_A):
        sl = slice(g * GROUP_A, (g + 1) * GROUP_A)
        vn = _rms(v[:, sl]) * gv_ref[:, sl]
        vn_ref[:, sl] = vn
        wm = jnp.where(keep, w_ref[0, g], 0.0).astype(BF16)
        z = _dot(wm, vn.astype(BF16)) + b_ref[0][:, g:g + 1]
        ya_ref[:, sl] = (u[:, sl] * z).astype(BF16)


def _gmlp(h, gv, w2, b2, n_prompt_rows, l):
    m = h.shape[0]
    npb = n_prompt_rows // GROUP_A
    nsb = (m - n_prompt_rows) // GROUP_A
    return pl.pallas_call(
        functools.partial(_gmlp_kernel, n_prompt_blocks=npb),
        grid=(npb + nsb,),
        in_specs=[pl.BlockSpec((GROUP_A, 2 * WIDTH_A), lambda i: (i, C_UA // (2 * WIDTH_A))),
                  pl.BlockSpec((None, 1, WIDTH_A), lambda i: (l, 0, 0)),
                  pl.BlockSpec((None, 1, N_GROUPS_A, GROUP_A, GROUP_A),
                               lambda i: (l, jnp.where(i >= npb, 1, 0), 0, 0, 0)),
                  pl.BlockSpec((None, 1, GROUP_A, N_GROUPS_A),
                               lambda i: (l, jnp.where(i >= npb, 1, 0), 0, 0))],
        out_specs=[pl.BlockSpec((GROUP_A, WIDTH_A), lambda i: (i, 0)),
                   pl.BlockSpec((GROUP_A, WIDTH_A), lambda i: (jnp.maximum(i - npb, 0), 0))],
        out_shape=[jax.ShapeDtypeStruct((m, WIDTH_A), BF16),
                   jax.ShapeDtypeStruct((nsb * GROUP_A, WIDTH_A), F32)],
        compiler_params=_cparams(("arbitrary",)),
        name="gmlp",
    )(h, gv, w2, b2)


def _gla_kernel(q_ref, k_ref, v_ref, og_ref, misc_ref, wg_ref, bg_ref, gbo_ref, s0_ref,
                yb_ref, sout_ref, st_ref, *, n_prompt_steps):
    j = pl.program_id(0)
    rows = GLA_SUB * CHUNK
    fresh = j >= n_prompt_steps

    lrb = misc_ref[:, MISC_LRB:MISC_LRB + GATE_RANK_B].astype(BF16)
    x = _dot(lrb, wg_ref[...]) + bg_ref[...]
    g = (jnp.minimum(x, 0.0) - jnp.log1p(jnp.exp(-jnp.abs(x)))) * (1.0 / GATE_TEMP_B)
    rr = lax.broadcasted_iota(I32, (rows, rows), 0)
    cc = lax.broadcasted_iota(I32, (rows, rows), 1)
    same = (rr >> 6) == (cc >> 6)
    tril = same & (cc <= rr)
    g1 = g.astype(BF16)
    r1 = g - g1.astype(F32)
    g2 = r1.astype(BF16)
    g3 = (r1 - g2.astype(F32)).astype(BF16)

    def chunk_sums(mask):
        m = jnp.where(mask, 1.0, 0.0).astype(BF16)
        return _dot(m, g1) + _dot(m, g2) + _dot(m, g3)

    b = chunk_sums(tril)
    b_last = chunk_sums(same)
    b_mid = chunk_sums(same & ((cc & (CHUNK - 1)) < CHUNK // 2))
    qs = q_ref[...] * (DK_B ** -0.5)
    kk = k_ref[...]
    q_inter = (qs * jnp.exp(b)).astype(BF16)
    q_intra = (qs * jnp.exp(b - b_mid)).astype(BF16)
    k_intra = (kk * jnp.exp(b_mid - b)).astype(BF16)
    k_state = (kk * jnp.exp(b_last - b)).astype(BF16)
    dec = jnp.exp(b_last)
    for h in range(N_HEADS_B):
        sk = slice(h * DK_B, (h + 1) * DK_B)
        sv = slice(h * DV_B, (h + 1) * DV_B)
        vh = v_ref[:, sv].astype(BF16)
        att = jnp.where(tril, _dot_nt(q_intra[:, sk], k_intra[:, sk]), 0.0)
        o_intra = _dot(att.astype(BF16), vh)
        st = st_ref[h]
        o_inter = []
        for c in range(GLA_SUB):
            rc = slice(c * CHUNK, (c + 1) * CHUNK)
            start = fresh | ((j == 0) & (c == 0))
            st = jnp.where(start, s0_ref[c, h].T, st)
            o_inter.append(_dot_nt(q_inter[rc, sk], st.astype(BF16)))
            st = st * dec[c * CHUNK:c * CHUNK + 1, sk] + _dot_tn(vh[rc], k_state[rc, sk])
            sout_ref[c, h] = st.T
        st_ref[h] = st
        o = jnp.concatenate(o_inter, axis=0) + o_intra
        on = _rms(o) * gbo_ref[:, sv]
        yb_ref[:, sv] = (on * jax.nn.silu(og_ref[:, sv])).astype(BF16)


def _gla(h, wg, bg, gbo, s_in, n_prompt_rows, l):
    m = h.shape[0]
    rows = GLA_SUB * CHUNK
    nps = n_prompt_rows // rows
    nslots = s_in.shape[1]

    def blk(j):
        return jnp.where(j < nps, 0, j - nps + 1)

    return pl.pallas_call(
        functools.partial(_gla_kernel, n_prompt_steps=nps),
        grid=(m // rows,),
        in_specs=[pl.BlockSpec((rows, QK_B), lambda j: (j, C_QB // QK_B)),
                  pl.BlockSpec((rows, QK_B), lambda j: (j, C_KB // QK_B)),
                  pl.BlockSpec((rows, WIDTH_B), lambda j: (j, C_VB // WIDTH_B)),
                  pl.BlockSpec((rows, WIDTH_B), lambda j: (j, C_OG // WIDTH_B)),
                  pl.BlockSpec((rows, LANES), lambda j: (j, C_MISC // LANES)),
                  pl.BlockSpec((None, GATE_RANK_B, QK_B), lambda j: (l, 0, 0)),
                  pl.BlockSpec((None, 1, QK_B), lambda j: (l, 0, 0)),
                  pl.BlockSpec((None, 1, WIDTH_B), lambda j: (l, 0, 0)),
                  pl.BlockSpec((None, GLA_SUB, N_HEADS_B, DK_B, DV_B), lambda j: (l, blk(j), 0, 0, 0))],
        out_specs=[pl.BlockSpec((rows, WIDTH_B), lambda j: (j, 0)),
                   pl.BlockSpec((GLA_SUB, N_HEADS_B, DK_B, DV_B), lambda j: (blk(j), 0, 0, 0))],
        out_shape=[jax.ShapeDtypeStruct((m, WIDTH_B), BF16),
                   jax.ShapeDtypeStruct((nslots, N_HEADS_B, DK_B, DV_B), F32)],
        scratch_shapes=[pltpu.VMEM((N_HEADS_B, DV_B, DK_B), F32)],
        compiler_params=_cparams(("arbitrary",)),
        name="gla",
    )(h, h, h, h, h, wg, bg, gbo, s_in)


def _fold_rows(x, op):
    tile = SUBLANES * (4 // x.dtype.itemsize)
    parts = [x[j * tile:(j + 1) * tile] for j in range(x.shape[0] // tile)]
    while len(parts) > 1:
        nxt = [op(parts[j], parts[j + 1]) for j in range(0, len(parts) - 1, 2)]
        parts = nxt + parts[len(parts) - len(parts) % 2:]
    return parts[0]


def _loop(n, body, init):
    if isinstance(n, int):
        carry = init
        for c in range(n):
            carry = body(c, carry)
        return carry
    return lax.fori_loop(0, n, body, init)


def _dsa_core(q, qi, misc_t, far_ki, far_k, far_vt, nf, far_adm, near_ki, near_k, near_vt, near_adm, n_adm,
              bias_near_ref, out_ref, scratch, *, tq, kw, nw, topk):
    fkeys_ref, nkeys_ref, m_ref, acc_ref, ss_ref, p_ref = scratch
    rep = N_HEADS_C // N_KV_C
    groups = range(N_KV_C)
    sls = [slice(g * HD_C, (g + 1) * HD_C) for g in groups]
    qi_s = (qi * (D_IDX ** -0.5)).astype(BF16)
    qi_all = jnp.concatenate([qi_s[:, h * D_IDX:(h + 1) * D_IDX] for h in range(N_IDX_HEADS)], axis=0)
    wi_t = misc_t[MISC_WIC:MISC_WIC + N_IDX_HEADS, :] * (N_IDX_HEADS ** -0.5)
    qs = q * (HD_C ** -0.5 * LOG2E)
    q_g = [jnp.concatenate([qs[:, (g * rep + r) * HD_C:(g * rep + r + 1) * HD_C] for r in range(rep)],
                           axis=0).astype(BF16) for g in groups]
    kf = float(topk)

    def keys_of(ki_blk, adm):
        d = _dot_nt(ki_blk, qi_all)
        acc = None
        for h in range(N_IDX_HEADS):
            t = wi_t[h:h + 1, :] * jnp.maximum(d[:, h * tq:(h + 1) * tq], 0.0)
            acc = t if acc is None else acc + t
        bits = lax.bitcast_convert_type(acc, I32)
        key = bits ^ ((bits >> 31) & 0x7FFFFFFF)
        key = jnp.where(acc == 0.0, 0, key)
        if adm is not None:
            key = jnp.where(adm, key, INT_MIN)
        return key

    def extremes(key, carry):
        kmax, kmin = carry
        kmax = jnp.maximum(kmax, _fold_rows(key, jnp.maximum))
        kmin = jnp.minimum(kmin, _fold_rows(jnp.where(key == INT_MIN, INT_MAX, key), jnp.minimum))
        return kmax, kmin

    def fill(c, carry):
        key = keys_of(far_ki(c), None if far_adm is None else far_adm(c))
        fkeys_ref[c] = key
        return extremes(key, carry)

    carry = _loop(nf, fill, (jnp.full((SUBLANES, tq), INT_MIN, I32), jnp.full((SUBLANES, tq), INT_MAX, I32)))
    key = keys_of(near_ki, near_adm)
    nkeys_ref[...] = key
    kmax, kmin = extremes(key, carry)
    kmax = jnp.max(kmax, axis=0, keepdims=True)
    kmin = jnp.min(kmin, axis=0, keepdims=True)

    def count(thr, strict):
        def cmp(x):
            hit = (x > thr) if strict else (x >= thr)
            return _fold_rows(jnp.where(hit, 1.0, 0.0), jnp.add)

        part = _loop(nf, lambda c, a: a + cmp(fkeys_ref[c]), jnp.zeros((SUBLANES, tq), F32))
        part = part + cmp(nkeys_ref[...])
        return jnp.sum(part, axis=0, keepdims=True)

    def bits_flip(x):
        return x ^ ((x >> 31) & 0x7FFFFFFF)

    def any_lane(flag):
        return jnp.max(jnp.where(flag, 1.0, 0.0))

    def snap(lo, hi):
        def pull(x, carry):
            kin, kax = carry
            kin = jnp.minimum(kin, _fold_rows(jnp.where(x >= lo, x, INT_MAX), jnp.minimum))
            kax = jnp.maximum(kax, _fold_rows(jnp.where(x < hi, x, INT_MIN), jnp.maximum))
            return kin, kax

        init = (jnp.full((SUBLANES, tq), INT_MAX, I32), jnp.full((SUBLANES, tq), INT_MIN, I32))
        kin, kax = pull(nkeys_ref[...], _loop(nf, lambda c, carry: pull(fkeys_ref[c], carry), init))
        return jnp.min(kin, axis=0, keepdims=True), jnp.max(kax, axis=0, keepdims=True)

    enough = n_adm >= kf
    log_k = math.log2(topk)

    def open_lanes(lo, hi, flo):
        return enough & (flo > kf) & (hi > lo + 1)

    def log_excess(c):
        return jnp.log2(jnp.maximum(c, 0.5)) - log_k

    def step(state):
        it, _, lo, hi, flo, fhi, glo, ghi, side = state
        active = open_lanes(lo, hi, flo)

        def pull_in(_):
            kin, kax = snap(lo, hi)
            return jnp.where(active, kin, lo), jnp.where(active, kax + 1, hi), flo, fhi, glo, ghi, side

        def probe(_):
            v_lo = lax.bitcast_convert_type(bits_flip(lo), F32)
            v_hi = lax.bitcast_convert_type(bits_flip(hi), F32)
            frac = jnp.minimum(jnp.maximum(glo / (glo - ghi), 0.02), 0.98)
            t_int = bits_flip(lax.bitcast_convert_type(v_lo + (v_hi - v_lo) * frac, I32))
            t_mid = (lo & hi) + ((lo ^ hi) >> 1)
            t = jnp.where(it >= BISECT_AFTER, t_mid, t_int)
            t = jnp.minimum(jnp.maximum(t, lo + 1), hi - 1)
            c = count(t, False)
            g = log_excess(c)
            up = active & (c >= kf)
            dn = active & (c < kf)
            ghi2 = jnp.where(up & (side > 0.0), ghi * 0.5, ghi)
            glo2 = jnp.where(dn & (side < 0.0), glo * 0.5, glo)
            return (jnp.where(up, t, lo), jnp.where(dn, t, hi), jnp.where(up, c, flo), jnp.where(dn, c, fhi),
                    jnp.where(up, g, glo2), jnp.where(dn, g, ghi2), jnp.where(up, 1.0, jnp.where(dn, -1.0, side)))

        is_snap = (it % SNAP_EVERY == SNAP_EVERY - 1) & (it < BISECT_AFTER)
        lo, hi, flo, fhi, glo, ghi, side = lax.cond(is_snap, pull_in, probe, 0)
        return it + 1, any_lane(open_lanes(lo, hi, flo)), lo, hi, flo, fhi, glo, ghi, side

    hi0 = kmax + 1
    zero = jnp.zeros((1, tq), F32)
    state = (jnp.int32(0), any_lane(open_lanes(kmin, hi0, n_adm)), kmin, hi0, n_adm, zero,
             log_excess(n_adm), log_excess(zero), zero)
    state = lax.while_loop(lambda st: (st[1] > 0.0) & (st[0] < MAX_SEARCH_STEPS), step, state)
    thr = jnp.where(enough, state[2], INT_MIN)
    has_tie = any_lane(enough & (state[4] > kf)) > 0.0

    @pl.when(has_tie)
    def _():
        need = kf - count(thr, True)

        def lower(n):
            a = lax.broadcasted_iota(I32, (n, n), 0)
            b = lax.broadcasted_iota(I32, (n, n), 1)
            return jnp.where(b < a, 1.0, 0.0).astype(BF16)

        def demote(keys, run, lt):
            eq = (keys == thr) & (keys > INT_MIN)
            eqf = jnp.where(eq, 1.0, 0.0)
            before = _dot(lt, eqf.astype(BF16)) + run
            keys = jnp.where(eq & (before >= need), INT_MIN, keys)
            return keys, run + jnp.sum(eqf, axis=0, keepdims=True)

        lt_far = lower(kw)

        def step(c, run):
            keys, run = demote(fkeys_ref[c], run, lt_far)
            fkeys_ref[c] = keys
            return run

        run = _loop(nf, step, jnp.zeros((1, tq), F32))
        keys, _ = demote(nkeys_ref[...], run, lower(nw))
        nkeys_ref[...] = keys

    thr_sel = jnp.maximum(thr, INT_MIN + 1)
    m_ref[...] = jnp.full(m_ref.shape, MASKED, F32)
    acc_ref[...] = jnp.zeros(acc_ref.shape, F32)

    def logits(k_blk):
        return [_dot_nt(k_blk[:, sls[g]], q_g[g]) for g in groups]

    def attend(slot, keys_rows, vt_blk, bias_ref, n):
        tiles = [slice(r0, r0 + SOFTMAX_ROWS) for r0 in range(0, n, SOFTMAX_ROWS)]
        heads = [slice(r * tq, (r + 1) * tq) for r in range(rep)]

        def masked_logits(g, rows, selb):
            s = ss_ref[slot, g, rows, :]
            if bias_ref is not None:
                s = s + bias_ref[g, rows, :]
            return [s[:, hd] + selb for hd in heads]

        run = [None] * N_KV_C
        for rows in tiles:
            selb = jnp.where(keys_rows(rows) >= thr_sel, 0.0, MASKED)
            for g in groups:
                top = jnp.concatenate([_fold_rows(s, jnp.maximum) for s in masked_logits(g, rows, selb)], axis=1)
                run[g] = top if run[g] is None else jnp.maximum(run[g], top)
        m_new = []
        for g in groups:
            m_old = m_ref[g]
            m_new.append(jnp.maximum(m_old, jnp.max(run[g], axis=0, keepdims=True)))
            acc_ref[g] = jnp.exp2(m_old - m_new[g]) * acc_ref[g]
            m_ref[g] = m_new[g]
        for rows in tiles:
            selb = jnp.where(keys_rows(rows) >= thr_sel, 0.0, MASKED)
            for g in groups:
                for hd, s in zip(heads, masked_logits(g, rows, selb)):
                    p_ref[g, rows, hd] = jnp.exp2(s - m_new[g][:, hd]).astype(BF16)
        ones = jnp.ones((ONES_ROWS, n), BF16)
        for g in groups:
            acc_ref[g] += _dot(jnp.concatenate([vt_blk[sls[g], :], ones], axis=0), p_ref[g, :n, :])

    last = fkeys_ref.shape[0] - 1

    def put_logits(slot, k_blk, n):
        for g, s in enumerate(logits(k_blk)):
            ss_ref[slot, g, :n, :] = s

    def far_body(c, cur):
        put_logits(1 - cur, far_k(min(c + 1, last) if isinstance(c, int) else jnp.minimum(c + 1, last)), kw)
        attend(cur, lambda rows: fkeys_ref[c, rows, :], far_vt(c), None, kw)

    def far_step(c, carry):
        if isinstance(c, int):
            far_body(c, c % 2)
        else:
            for cur in range(2):
                pl.when(c % 2 == cur)(functools.partial(far_body, c, cur))
        return carry

    put_logits(0, far_k(0), kw)
    _loop(nf, far_step, 0)
    put_logits(0, near_k, nw)
    attend(0, lambda rows: nkeys_ref[rows, :], near_vt, bias_near_ref, nw)

    for g in groups:
        a = acc_ref[g]
        o = a[:HD_C] / a[HD_C:HD_C + 1]
        for r in range(rep):
            hh = g * rep + r
            out_ref[:, hh * HD_C:(hh + 1) * HD_C] = o[:, r * tq:(r + 1) * tq].T.astype(BF16)


def _dsa_scratch(nf, tq, kw, nw):
    rep = N_HEADS_C // N_KV_C
    return [pltpu.VMEM((nf, kw, tq), I32),
            pltpu.VMEM((nw, tq), I32),
            pltpu.VMEM((N_KV_C, 1, rep * tq), F32),
            pltpu.VMEM((N_KV_C, HD_C + ONES_ROWS, rep * tq), F32),
            pltpu.VMEM((2, N_KV_C, max(kw, nw), rep * tq), F32),
            pltpu.VMEM((N_KV_C, max(kw, nw), rep * tq), BF16)]


def _dsa_prompt_kernel(q_ref, qi_ref, misc_ref, k_ref, vt_far_ref, vt_near_ref, ki_ref, bias_near_ref,
                       out_ref, *scratch, tq, topk):
    i = pl.program_id(0)
    kw = FAR_KW
    nw = NEAR_SPAN + tq
    q0 = i * tq
    far_limit = q0 - NEAR_SPAN
    near0 = jnp.maximum(far_limit, 0)
    nf = (near0 + kw - 1) // kw

    def rows(c):
        return pl.ds(pl.multiple_of(c * kw, kw), kw)

    def far_adm(c):
        return (c * kw + lax.broadcasted_iota(I32, (kw, tq), 0)) < far_limit

    left = pl.ds(pl.multiple_of(near0, NEAR_SPAN), NEAR_SPAN)
    right = pl.ds(pl.multiple_of(q0, tq), tq)
    near_ki = jnp.concatenate([ki_ref[left, :], ki_ref[right, :]], axis=0)
    near_k = jnp.concatenate([k_ref[left, :], k_ref[right, :]], axis=0)
    per = tq // NEAR_SPAN
    near_vt = jnp.concatenate([vt_near_ref[near0 // NEAR_SPAN]] + [vt_near_ref[i * per + j] for j in range(per)],
                              axis=1)
    key = lax.broadcasted_iota(I32, (nw, tq), 0)
    qry = lax.broadcasted_iota(I32, (nw, tq), 1)
    first_key = jnp.where(i > 0, 0, NEAR_SPAN)
    near_adm = (key >= first_key) & (((key - NEAR_SPAN) >> 6) <= (qry >> 6))
    n_adm = ((((q0 + lax.broadcasted_iota(I32, (1, tq), 1)) >> 6) + 1) * CHUNK).astype(F32)

    _dsa_core(q_ref[...], qi_ref[...], misc_ref[...].T,
              lambda c: ki_ref[rows(c), :], lambda c: k_ref[rows(c), :], lambda c: vt_far_ref[c],
              nf, far_adm, near_ki, near_k, near_vt, near_adm, n_adm,
              bias_near_ref, out_ref, scratch,
              tq=tq, kw=kw, nw=nw, topk=topk)


def _dsa_prompt(h, kb, vt_far, vt_near, kib, bias_near, t):
    tq = PROMPT_TQ
    topk = min(TOPK_MAX, t // 4)
    return pl.pallas_call(
        functools.partial(_dsa_prompt_kernel, tq=tq, topk=topk),
        grid=(t // tq,),
        in_specs=[pl.BlockSpec((tq, WIDTH_C), lambda i: (i, C_QC // WIDTH_C)),
                  pl.BlockSpec((tq, QI_C), lambda i: (i, C_QIC // QI_C)),
                  pl.BlockSpec((tq, LANES), lambda i: (i, C_MISC // LANES)),
                  pl.BlockSpec((t, KV_C), lambda i: (0, 0)),
                  pl.BlockSpec(vt_far.shape, lambda i: (0, 0, 0)),
                  pl.BlockSpec(vt_near.shape, lambda i: (0, 0, 0)),
                  pl.BlockSpec((t, D_IDX), lambda i: (0, 0)),
                  pl.BlockSpec(bias_near.shape, lambda i: (0, 0, 0))],
        out_specs=pl.BlockSpec((tq, WIDTH_C), lambda i: (i, 0)),
        out_shape=jax.ShapeDtypeStruct((t, WIDTH_C), BF16),
        scratch_shapes=_dsa_scratch(t // FAR_KW, tq, FAR_KW, NEAR_SPAN + tq),
        compiler_params=_cparams(("arbitrary",)),
        name="dsa_prompt",
    )(h, h, h, kb, vt_far, vt_near, kib, bias_near)


def _dsa_sample_kernel(q_ref, qi_ref, misc_ref, kvn_ref, pk_ref, pv_ref, pki_ref, bias_near_ref,
                       out_ref, *scratch, kw, nf, topk):
    tq = CHUNK
    far_len = kw * nf
    misc = misc_ref[...]
    kvn = kvn_ref[...]

    def group_rows(ref, start, n, g):
        return ref[0, pl.ds(2 * start + g, n, stride=2), :]

    def past_k(start, n):
        return jnp.concatenate([group_rows(pk_ref, start, n, g) for g in range(N_KV_C)], axis=1)

    def past_vt(start, n):
        return jnp.concatenate([group_rows(pv_ref, start, n, g).T for g in range(N_KV_C)], axis=0)

    near_ki = jnp.concatenate([pki_ref[0, pl.ds(far_len, NEAR_SPAN), :], misc[:, :D_IDX]], axis=0).astype(BF16)
    near_k = jnp.concatenate([past_k(far_len, NEAR_SPAN), kvn[:, :KV_C]], axis=0).astype(BF16)
    near_vt = jnp.concatenate([past_vt(far_len, NEAR_SPAN), kvn[:, KV_C:].T], axis=1).astype(BF16)

    _dsa_core(q_ref[...], qi_ref[...], misc.T,
              lambda c: pki_ref[0, pl.ds(c * kw, kw), :].astype(BF16),
              lambda c: past_k(c * kw, kw).astype(BF16), lambda c: past_vt(c * kw, kw).astype(BF16),
              nf, None, near_ki, near_k, near_vt, None, jnp.full((1, tq), float(far_len + NEAR_SPAN + CHUNK), F32),
              bias_near_ref, out_ref, scratch,
              tq=tq, kw=kw, nw=NEAR_SPAN + CHUNK, topk=topk)


def _dsa_sample(h, pk, pv, pki, bias_near, t, l):
    nb, p = pki.shape[1], pki.shape[2]
    tq = CHUNK
    far_len = p - NEAR_SPAN
    kw = _pick(far_len, (640, 512, 384, 256, 128))
    nf = far_len // kw
    nw = NEAR_SPAN + CHUNK
    topk = min(TOPK_MAX, (p + CHUNK) // 4)
    row0 = t // tq
    return pl.pallas_call(
        functools.partial(_dsa_sample_kernel, kw=kw, nf=nf, topk=topk),
        grid=(nb,),
        in_specs=[pl.BlockSpec((tq, WIDTH_C), lambda b: (row0 + b, C_QC // WIDTH_C)),
                  pl.BlockSpec((tq, QI_C), lambda b: (row0 + b, C_QIC // QI_C)),
                  pl.BlockSpec((tq, LANES), lambda b: (row0 + b, C_MISC // LANES)),
                  pl.BlockSpec((tq, 2 * KV_C), lambda b: (row0 + b, C_KC // (2 * KV_C))),
                  pl.BlockSpec((None, 1, N_KV_C * p, HD_C), lambda b: (l, b, 0, 0)),
                  pl.BlockSpec((None, 1, N_KV_C * p, HD_C), lambda b: (l, b, 0, 0)),
                  pl.BlockSpec((None, 1, p, D_IDX), lambda b: (l, b, 0, 0)),
                  pl.BlockSpec(bias_near.shape, lambda b: (0, 0, 0))],
        out_specs=pl.BlockSpec((tq, WIDTH_C), lambda b: (b, 0)),
        out_shape=jax.ShapeDtypeStruct((nb * tq, WIDTH_C), BF16),
        scratch_shapes=_dsa_scratch(nf, tq, kw, nw),
        compiler_params=_cparams(("arbitrary",)),
        name="dsa_sample",
    )(h, h, h, h, pk, pv, pki, bias_near)


def _outproj_kernel(x_ref, ya_ref, yb_ref, yc_ref, wa_ref, wb_ref, wc_ref, o_ref):
    o_ref[...] = (x_ref[...] + _dot(ya_ref[...], wa_ref[...]) + _dot(yb_ref[...], wb_ref[...])
                  + _dot(yc_ref[...], wc_ref[...]))


def _outproj(x, ya, yb, yc, wa, wb, wc, l):
    m = x.shape[0]
    tm = _pick(m, (1024, 512, 256, 128))
    tn = 512
    return pl.pallas_call(
        _outproj_kernel,
        grid=(m // tm, D_MODEL // tn),
        in_specs=[pl.BlockSpec((tm, tn), lambda i, j: (i, j)),
                  pl.BlockSpec((tm, WIDTH_A), lambda i, j: (i, 0)),
                  pl.BlockSpec((tm, WIDTH_B), lambda i, j: (i, 0)),
                  pl.BlockSpec((tm, WIDTH_C), lambda i, j: (i, 0)),
                  pl.BlockSpec((None, WIDTH_A, tn), lambda i, j: (l, 0, j)),
                  pl.BlockSpec((None, WIDTH_B, tn), lambda i, j: (l, 0, j)),
                  pl.BlockSpec((None, WIDTH_C, tn), lambda i, j: (l, 0, j))],
        out_specs=pl.BlockSpec((tm, tn), lambda i, j: (i, j)),
        out_shape=jax.ShapeDtypeStruct((m, D_MODEL), F32),
        compiler_params=_cparams(("arbitrary", "arbitrary")),
        name="outproj",
    )(x, ya, yb, yc, wa, wb, wc)


def _ffn_kernel(x_ref, g_ref, wu_ref, wd_ref, o_ref, xn_ref, acc_ref):
    f = pl.program_id(1)

    @pl.when(f == 0)
    def _():
        xn_ref[...] = (_rms(x_ref[...]) * g_ref[...]).astype(BF16)
        acc_ref[...] = jnp.zeros(acc_ref.shape, F32)

    a = jnp.maximum(_dot(xn_ref[...], wu_ref[...]), 0.0)
    acc_ref[...] += _dot((a * a).astype(BF16), wd_ref[...])

    @pl.when(f == pl.num_programs(1) - 1)
    def _():
        o_ref[...] = x_ref[...] + acc_ref[...]


def _ffn(x, g, wu, wd, l):
    m = x.shape[0]
    tm = _pick(m, (512, 256, 128))
    tf = 1024
    return pl.pallas_call(
        _ffn_kernel,
        grid=(m // tm, D_FF // tf),
        in_specs=[pl.BlockSpec((tm, D_MODEL), lambda i, f: (i, 0)),
                  pl.BlockSpec((None, 1, D_MODEL), lambda i, f: (l, 0, 0)),
                  pl.BlockSpec((None, D_MODEL, tf), lambda i, f: (l, 0, f)),
                  pl.BlockSpec((None, tf, D_MODEL), lambda i, f: (l, f, 0))],
        out_specs=pl.BlockSpec((tm, D_MODEL), lambda i, f: (i, 0)),
        out_shape=jax.ShapeDtypeStruct((m, D_MODEL), F32),
        scratch_shapes=[pltpu.VMEM((tm, D_MODEL), BF16), pltpu.VMEM((tm, D_MODEL), F32)],
        compiler_params=_cparams(("arbitrary", "arbitrary")),
        name="ffn",
    )(x, g, wu, wd)


def _norm_kernel(x_ref, g_ref, o_ref):
    o_ref[...] = _rms(x_ref[...]) * g_ref[...]


def _final_norm(x, g):
    m = x.shape[0]
    tm = _pick(m, (1024, 512, 256, 128))
    return pl.pallas_call(
        _norm_kernel,
        grid=(m // tm,),
        in_specs=[pl.BlockSpec((tm, D_MODEL), lambda i: (i, 0)),
                  pl.BlockSpec((1, D_MODEL), lambda i: (0, 0))],
        out_specs=pl.BlockSpec((tm, D_MODEL), lambda i: (i, 0)),
        out_shape=jax.ShapeDtypeStruct((m, D_MODEL), F32),
        compiler_params=_cparams(("arbitrary",)),
        name="final_norm",
    )(x, g)


def _t5_bucket(rel):
    half = N_BUCKETS // 2
    max_exact = half // 2
    n = jnp.abs(rel)
    nf = jnp.maximum(n, 1).astype(F32)
    large = max_exact + (jnp.log(nf / max_exact) / jnp.log(MAX_DISTANCE / max_exact)
                         * (half - max_exact)).astype(I32)
    large = jnp.minimum(large, half - 1)
    return jnp.where(rel > 0, half, 0) + jnp.where(n < max_exact, n, large)


def _bias_table(rel_bias, tq, nw):
    j = jnp.arange(nw, dtype=I32)[:, None]
    t = jnp.arange(tq, dtype=I32)[None, :]
    hot = jax.nn.one_hot(_t5_bucket(j - NEAR_SPAN - t), N_BUCKETS, dtype=F32)
    near = jnp.einsum("jtb,bh->jth", hot, rel_bias, precision=lax.Precision.HIGHEST)
    far = rel_bias[_t5_bucket(jnp.int32(-(NEAR_SPAN + 1)))]
    rep = N_HEADS_C // N_KV_C
    near = ((near - far) * LOG2E).astype(F32).reshape(nw, tq, N_KV_C, rep)
    return jnp.transpose(near, (2, 0, 3, 1)).reshape(N_KV_C, nw, rep * tq)


def _permute_w_in(w):
    sizes = (WIDTH_A, WIDTH_A, QK_B, QK_B, WIDTH_B, GATE_RANK_B, WIDTH_B, WIDTH_C, KV_C, KV_C, QI_C,
             D_IDX, N_IDX_HEADS)
    parts, off = [], 0
    for n in sizes:
        parts.append(w[..., off:off + n])
        off += n
    ua, va, qb, kb, vb, lrb, og, qc, kc, vc, qic, kic, wic = parts
    pad = jnp.zeros(w.shape[:-1] + (H_WIDTH - (C_MISC + D_IDX + GATE_RANK_B + N_IDX_HEADS),), w.dtype)
    return jnp.concatenate([qb, kb, vb, og, qc, kc, vc, qic, ua, va, kic, lrb, wic, pad], axis=-1).astype(BF16)


def kernel(x_prompt, x_sample, cache_c_k, cache_c_v, cache_c_kidx, state_b_s, norm_mix, w_in, norm_a_v,
           w_s_a, b_s_a, w_gate_b, b_gate_b, norm_b_o, rel_bias, w_o, norm_ffn, w_up, w_down, norm_final):
    depth = w_in.shape[0]
    nbp, t, _ = x_prompt.shape
    nbs, ts, _ = x_sample.shape
    p = cache_c_k.shape[2]
    assert nbp == 1 and ts == CHUNK and t % FAR_KW == 0 and (nbs * ts) % GROUP_A == 0
    assert p % NEAR_SPAN == 0 and p >= 2 * NEAR_SPAN
    assert t % PROMPT_TQ == 0 and t % (GLA_SUB * CHUNK) == 0 and nbs % GLA_SUB == 0

    x = jnp.concatenate([x_prompt[0], x_sample.reshape(nbs * ts, D_MODEL)], axis=0)
    w_in_p = _permute_w_in(w_in)
    wo_a = w_o[:, :WIDTH_A].astype(BF16)
    wo_b = w_o[:, WIDTH_A:WIDTH_A + WIDTH_B].astype(BF16)
    wo_c = w_o[:, WIDTH_A + WIDTH_B:].astype(BF16)
    w_up_b = w_up.astype(BF16)
    w_down_b = w_down.astype(BF16)
    w_gate = w_gate_b.astype(BF16)
    half = GROUP_A // 2
    w_s2 = jnp.stack([w_s_a, jnp.tile(w_s_a[:, :, :half, :half], (1, 1, 2, 2))], axis=1)
    b_s2 = jnp.stack([b_s_a, jnp.tile(b_s_a[:, :, :half], (1, 1, 2))], axis=1)
    b_s2 = jnp.swapaxes(b_s2, 2, 3)
    s_all = jnp.concatenate([jnp.zeros((depth, GLA_SUB) + state_b_s.shape[2:], F32), state_b_s], axis=1)
    bias_near = _bias_table(rel_bias, PROMPT_TQ, NEAR_SPAN + PROMPT_TQ)
    bias_near_s = _bias_table(rel_bias, CHUNK, NEAR_SPAN + CHUNK)
    pk_all = cache_c_k.reshape(depth, nbs, N_KV_C * p, HD_C)
    pv_all = cache_c_v.reshape(depth, nbs, N_KV_C * p, HD_C)

    row = lambda a: a[:, None, :]
    outs = [[] for _ in range(9)]
    for l in range(depth):
        h = _inproj(x, row(norm_mix), w_in_p, l)
        hp, hs = h[:t], h[t:]
        ya, vn = _gmlp(h, row(norm_a_v), w_s2, b_s2, t, l)
        yb, s_out = _gla(h, w_gate, row(b_gate_b), row(norm_b_o), s_all, t, l)
        kb = hp[:, C_KC:C_KC + KV_C].astype(BF16)
        vt = hp[:, C_VC:C_VC + KV_C].astype(BF16).T
        vt_far = jnp.swapaxes(vt.reshape(KV_C, t // FAR_KW, FAR_KW), 0, 1)
        vt_near = jnp.swapaxes(vt.reshape(KV_C, t // NEAR_SPAN, NEAR_SPAN), 0, 1)
        kib = hp[:, C_MISC:C_MISC + D_IDX].astype(BF16)
        yc_p = _dsa_prompt(h, kb, vt_far, vt_near, kib, bias_near, t)
        yc_s = _dsa_sample(h, pk_all, pv_all, cache_c_kidx, bias_near_s, t, l)
        yc = jnp.concatenate([yc_p, yc_s], axis=0)
        x = _outproj(x, ya, yb, yc, wo_a, wo_b, wo_c, l)
        x = _ffn(x, row(norm_ffn), w_up_b, w_down_b, l)

        outs[0].append(hp[:, C_KC:C_KC + KV_C].reshape(1, t, N_KV_C, HD_C))
        outs[1].append(hp[:, C_VC:C_VC + KV_C].reshape(1, t, N_KV_C, HD_C))
        outs[2].append(hp[:, C_MISC:C_MISC + D_IDX].reshape(1, t, D_IDX))
        outs[3].append(s_out[GLA_SUB - 1:GLA_SUB])
        outs[4].append(hs[:, C_KC:C_KC + KV_C].reshape(nbs, ts, N_KV_C, HD_C))
        outs[5].append(hs[:, C_VC:C_VC + KV_C].reshape(nbs, ts, N_KV_C, HD_C))
        outs[6].append(hs[:, C_MISC:C_MISC + D_IDX].reshape(nbs, ts, D_IDX))
        outs[7].append(s_out[GLA_SUB:])
        outs[8].append(vn.reshape(nbs, ts, WIDTH_A))

    y = _final_norm(x, norm_final[None])
    return (y[:t][None], y[t:].reshape(nbs, ts, D_MODEL)) + tuple(jnp.stack(o) for o in outs)
```

```python
import functools
import math

import jax
import jax.numpy as jnp
from jax import lax
from jax.experimental import pallas as pl
from jax.experimental.pallas import tpu as pltpu

BF16 = jnp.bfloat16
F32 = jnp.float32
I32 = jnp.int32

D_MODEL = 2048
EPS = 1e-6
CHUNK = 64
GROUP_A = 128
N_GROUPS_A = 4
WIDTH_A = 512
N_HEADS_B = 6
DK_B = 64
DV_B = 128
GATE_RANK_B = 16
GATE_TEMP_B = 16.0
WIDTH_B = 768
N_HEADS_C = 6
N_KV_C = 2
HD_C = 128
N_IDX_HEADS = 8
D_IDX = 64
TOPK_MAX = 256
WIDTH_C = 768
N_BUCKETS = 32
MAX_DISTANCE = 128
D_FF = 4 * D_MODEL
QK_B = N_HEADS_B * DK_B
KV_C = N_KV_C * HD_C
QI_C = N_IDX_HEADS * D_IDX

LANES = 128
SUBLANES = 8
ONES_ROWS = 16
VMEM_LIMIT = 56 * 1024 * 1024

C_QB = 0
C_KB = 384
C_VB = 768
C_OG = 1536
C_QC = 2304
C_KC = 3072
C_VC = 3328
C_QIC = 3584
C_UA = 4096
C_VA = 4608
C_MISC = 5120
MISC_LRB = D_IDX
MISC_WIC = D_IDX + GATE_RANK_B
H_WIDTH = 5376

INT_MIN = -2147483648
INT_MAX = 2147483647
SNAP_EVERY = 8
BISECT_AFTER = 40
MAX_SEARCH_STEPS = 80
MASKED = -1e30
LOG2E = 1.4426950408889634
NEAR_SPAN = 128
FAR_KW = 512
PROMPT_TQ = 256
SOFTMAX_ROWS = 64
GLA_SUB = 4
GMLP_ROWS = 256
SAMPLE_PAIR = 1


def _pick(n, cands):
    for c in cands:
        if n % c == 0:
            return c
    raise ValueError(f"no tile for {n}")


def _cparams(sem):
    return pltpu.CompilerParams(dimension_semantics=sem, vmem_limit_bytes=VMEM_LIMIT)


def _rms(x):
    return x * lax.rsqrt(jnp.mean(x * x, axis=-1, keepdims=True) + EPS)


def _dot(a, b):
    return jnp.dot(a, b, preferred_element_type=F32)


def _dot_nt(a, b):
    return lax.dot_general(a, b, (((1,), (1,)), ((), ())), preferred_element_type=F32)


def _dot_tn(a, b):
    return lax.dot_general(a, b, (((0,), (0,)), ((), ())), preferred_element_type=F32)


def _inproj_kernel(x_ref, g_ref, w_ref, o_ref, xn_ref):
    @pl.when(pl.program_id(1) == 0)
    def _():
        xn_ref[...] = (_rms(x_ref[...]) * g_ref[...]).astype(BF16)

    o_ref[...] = _dot(xn_ref[...], w_ref[...])


def _inproj(x, g, w, l):
    m = x.shape[0]
    tm = _pick(m, (1024, 512, 256, 128))
    tn = 768
    return pl.pallas_call(
        _inproj_kernel,
        grid=(m // tm, H_WIDTH // tn),
        in_specs=[pl.BlockSpec((tm, D_MODEL), lambda i, j: (i, 0)),
                  pl.BlockSpec((None, 1, D_MODEL), lambda i, j: (l, 0, 0)),
                  pl.BlockSpec((None, D_MODEL, tn), lambda i, j: (l, 0, j))],
        out_specs=pl.BlockSpec((tm, tn), lambda i, j: (i, j)),
        out_shape=jax.ShapeDtypeStruct((m, H_WIDTH), F32),
        scratch_shapes=[pltpu.VMEM((tm, D_MODEL), BF16)],
        compiler_params=_cparams(("arbitrary", "arbitrary")),
        name="inproj",
    )(x, g, w)


def _gmlp_kernel(h_ref, gv_ref, w_ref, b_ref, ya_ref, vn_ref, *, n_prompt_blocks):
    i = pl.program_id(0)
    hv = h_ref[...]
    u = jax.nn.gelu(hv[:, :WIDTH_A])
    v = jax.nn.gelu(hv[:, WIDTH_A:])
    r = lax.broadcasted_iota(I32, (GMLP_ROWS, GMLP_ROWS), 0)
    c = lax.broadcasted_iota(I32, (GMLP_ROWS, GMLP_ROWS), 1)
    shift = 7 - (i >= n_prompt_blocks).astype(I32)
    keep = (c <= r) & (jnp.right_shift(r, shift) == jnp.right_shift(c, shift))
    for g in range(N_GROUPS_A):
        sl = slice(g * GROUP_A, (g + 1) * GROUP_A)
        vn = _rms(v[:, sl]) * gv_ref[:, sl]
        vn_ref[:, sl] = vn
        wm = jnp.where(keep, w_ref[0, g], 0.0).astype(BF16)
        z = _dot(wm, vn.astype(BF16)) + b_ref[0][:, g:g + 1]
        ya_ref[:, sl] = (u[:, sl] * z).astype(BF16)


def _gmlp(h, gv, w2, b2, n_prompt_rows, l):
    m = h.shape[0]
    rows = GMLP_ROWS
    npb = n_prompt_rows // rows
    nsb = (m - n_prompt_rows) // rows
    return pl.pallas_call(
        functools.partial(_gmlp_kernel, n_prompt_blocks=npb),
        grid=(npb + nsb,),
        in_specs=[pl.BlockSpec((rows, 2 * WIDTH_A), lambda i: (i, C_UA // (2 * WIDTH_A))),
                  pl.BlockSpec((None, 1, WIDTH_A), lambda i: (l, 0, 0)),
                  pl.BlockSpec((None, 1, N_GROUPS_A, rows, rows),
                               lambda i: (l, jnp.where(i >= npb, 1, 0), 0, 0, 0)),
                  pl.BlockSpec((None, 1, rows, N_GROUPS_A),
                               lambda i: (l, jnp.where(i >= npb, 1, 0), 0, 0))],
        out_specs=[pl.BlockSpec((rows, WIDTH_A), lambda i: (i, 0)),
                   pl.BlockSpec((rows, WIDTH_A), lambda i: (jnp.maximum(i - npb, 0), 0))],
        out_shape=[jax.ShapeDtypeStruct((m, WIDTH_A), BF16),
                   jax.ShapeDtypeStruct((nsb * rows, WIDTH_A), F32)],
        compiler_params=_cparams(("arbitrary",)),
        name="gmlp",
    )(h, gv, w2, b2)


def _gla_kernel(q_ref, k_ref, v_ref, og_ref, misc_ref, wg_ref, bg_ref, gbo_ref, s0_ref,
                yb_ref, sout_ref, st_ref, *, n_prompt_steps):
    j = pl.program_id(0)
    rows = GLA_SUB * CHUNK
    fresh = j >= n_prompt_steps

    lrb = misc_ref[:, MISC_LRB:MISC_LRB + GATE_RANK_B].astype(BF16)
    x = _dot(lrb, wg_ref[...]) + bg_ref[...]
    g = (jnp.minimum(x, 0.0) - jnp.log1p(jnp.exp(-jnp.abs(x)))) * (1.0 / GATE_TEMP_B)
    rr = lax.broadcasted_iota(I32, (rows, rows), 0)
    cc = lax.broadcasted_iota(I32, (rows, rows), 1)
    same = (rr >> 6) == (cc >> 6)
    tril = same & (cc <= rr)
    g1 = g.astype(BF16)
    r1 = g - g1.astype(F32)
    g2 = r1.astype(BF16)
    g3 = (r1 - g2.astype(F32)).astype(BF16)

    def chunk_sums(mask):
        m = jnp.where(mask, 1.0, 0.0).astype(BF16)
        return _dot(m, g1) + _dot(m, g2) + _dot(m, g3)

    b = chunk_sums(tril)
    b_last = chunk_sums(same)
    b_mid = chunk_sums(same & ((cc & (CHUNK - 1)) < CHUNK // 2))
    qs = q_ref[...] * (DK_B ** -0.5)
    kk = k_ref[...]
    q_inter = (qs * jnp.exp(b)).astype(BF16)
    q_intra = (qs * jnp.exp(b - b_mid)).astype(BF16)
    k_intra = (kk * jnp.exp(b_mid - b)).astype(BF16)
    k_state = (kk * jnp.exp(b_last - b)).astype(BF16)
    dec = jnp.exp(b_last)
    for h in range(N_HEADS_B):
        sk = slice(h * DK_B, (h + 1) * DK_B)
        sv = slice(h * DV_B, (h + 1) * DV_B)
        vh = v_ref[:, sv].astype(BF16)
        att = jnp.where(tril, _dot_nt(q_intra[:, sk], k_intra[:, sk]), 0.0)
        o_intra = _dot(att.astype(BF16), vh)
        st = st_ref[h]
        o_inter = []
        for c in range(GLA_SUB):
            rc = slice(c * CHUNK, (c + 1) * CHUNK)
            start = fresh | ((j == 0) & (c == 0))
            st = jnp.where(start, s0_ref[c, h].T, st)
            o_inter.append(_dot_nt(q_inter[rc, sk], st.astype(BF16)))
            st = st * dec[c * CHUNK:c * CHUNK + 1, sk] + _dot_tn(vh[rc], k_state[rc, sk])
            sout_ref[c, h] = st.T
        st_ref[h] = st
        o = jnp.concatenate(o_inter, axis=0) + o_intra
        on = _rms(o) * gbo_ref[:, sv]
        yb_ref[:, sv] = (on * jax.nn.silu(og_ref[:, sv])).astype(BF16)


def _gla(h, wg, bg, gbo, s_in, n_prompt_rows, l):
    m = h.shape[0]
    rows = GLA_SUB * CHUNK
    nps = n_prompt_rows // rows
    nslots = s_in.shape[1]

    def blk(j):
        return jnp.where(j < nps, 0, j - nps + 1)

    return pl.pallas_call(
        functools.partial(_gla_kernel, n_prompt_steps=nps),
        grid=(m // rows,),
        in_specs=[pl.BlockSpec((rows, QK_B), lambda j: (j, C_QB // QK_B)),
                  pl.BlockSpec((rows, QK_B), lambda j: (j, C_KB // QK_B)),
                  pl.BlockSpec((rows, WIDTH_B), lambda j: (j, C_VB // WIDTH_B)),
                  pl.BlockSpec((rows, WIDTH_B), lambda j: (j, C_OG // WIDTH_B)),
                  pl.BlockSpec((rows, LANES), lambda j: (j, C_MISC // LANES)),
                  pl.BlockSpec((None, GATE_RANK_B, QK_B), lambda j: (l, 0, 0)),
                  pl.BlockSpec((None, 1, QK_B), lambda j: (l, 0, 0)),
                  pl.BlockSpec((None, 1, WIDTH_B), lambda j: (l, 0, 0)),
                  pl.BlockSpec((None, GLA_SUB, N_HEADS_B, DK_B, DV_B), lambda j: (l, blk(j), 0, 0, 0))],
        out_specs=[pl.BlockSpec((rows, WIDTH_B), lambda j: (j, 0)),
                   pl.BlockSpec((GLA_SUB, N_HEADS_B, DK_B, DV_B), lambda j: (blk(j), 0, 0, 0))],
        out_shape=[jax.ShapeDtypeStruct((m, WIDTH_B), BF16),
                   jax.ShapeDtypeStruct((nslots, N_HEADS_B, DK_B, DV_B), F32)],
        scratch_shapes=[pltpu.VMEM((N_HEADS_B, DV_B, DK_B), F32)],
        compiler_params=_cparams(("arbitrary",)),
        name="gla",
    )(h, h, h, h, h, wg, bg, gbo, s_in)


def _fold_rows(x, op):
    tile = SUBLANES * (4 // x.dtype.itemsize)
    parts = [x[j * tile:(j + 1) * tile] for j in range(x.shape[0] // tile)]
    while len(parts) > 1:
        nxt = [op(parts[j], parts[j + 1]) for j in range(0, len(parts) - 1, 2)]
        parts = nxt + parts[len(parts) - len(parts) % 2:]
    return parts[0]


def _loop(n, body, init):
    if isinstance(n, int):
        carry = init
        for c in range(n):
            carry = body(c, carry)
        return carry
    return lax.fori_loop(0, n, body, init)


REP_C = N_HEADS_C // N_KV_C
KV_SLICES = [slice(g * HD_C, (g + 1) * HD_C) for g in range(N_KV_C)]


def _scaled_queries(q, qi, misc_t):
    qi_s = (qi * (D_IDX ** -0.5)).astype(BF16)
    wi_t = misc_t[MISC_WIC:MISC_WIC + N_IDX_HEADS, :] * (N_IDX_HEADS ** -0.5)
    qs = (q * (HD_C ** -0.5 * LOG2E)).astype(BF16)
    return ([qi_s[:, h * D_IDX:(h + 1) * D_IDX] for h in range(N_IDX_HEADS)], wi_t,
            [qs[:, hh * HD_C:(hh + 1) * HD_C] for hh in range(N_HEADS_C)])


def _dsa_core(qi_all, wi_t, q_g, far_ki, far_k, far_vx, nf, far_adm, near_ki, near_k, near_vx, near_adm, n_adm,
              bias_near_ref, emit, scratch, *, tq, kw, nw, topk):
    fkeys_ref, nkeys_ref, m_ref, acc_ref, ss_ref, p_ref = scratch
    rep = REP_C
    groups = range(N_KV_C)
    kf = float(topk)

    def keys_of(ki_blk, adm):
        d = _dot_nt(ki_blk[0], qi_all[0])
        for kb, qb in zip(ki_blk[1:], qi_all[1:]):
            d = d + _dot_nt(kb, qb)
        acc = None
        for h in range(N_IDX_HEADS):
            t = wi_t[h:h + 1, :] * jnp.maximum(d[:, h * tq:(h + 1) * tq], 0.0)
            acc = t if acc is None else acc + t
        bits = lax.bitcast_convert_type(acc, I32)
        key = bits ^ ((bits >> 31) & 0x7FFFFFFF)
        key = jnp.where(acc == 0.0, 0, key)
        if adm is not None:
            key = jnp.where(adm, key, INT_MIN)
        return key

    def extremes(key, carry):
        kmax, kmin = carry
        kmax = jnp.maximum(kmax, _fold_rows(key, jnp.maximum))
        kmin = jnp.minimum(kmin, _fold_rows(jnp.where(key == INT_MIN, INT_MAX, key), jnp.minimum))
        return kmax, kmin

    def fill(c, carry):
        key = keys_of(far_ki(c), None if far_adm is None else far_adm(c))
        fkeys_ref[c] = key
        return extremes(key, carry)

    carry = _loop(nf, fill, (jnp.full((SUBLANES, tq), INT_MIN, I32), jnp.full((SUBLANES, tq), INT_MAX, I32)))
    key = keys_of(near_ki, near_adm)
    nkeys_ref[...] = key
    kmax, kmin = extremes(key, carry)
    kmax = jnp.max(kmax, axis=0, keepdims=True)
    kmin = jnp.min(kmin, axis=0, keepdims=True)

    def count(thr, strict):
        def cmp(x):
            hit = (x > thr) if strict else (x >= thr)
            return _fold_rows(jnp.where(hit, 1.0, 0.0), jnp.add)

        part = _loop(nf, lambda c, a: a + cmp(fkeys_ref[c]), jnp.zeros((SUBLANES, tq), F32))
        part = part + cmp(nkeys_ref[...])
        return jnp.sum(part, axis=0, keepdims=True)

    def bits_flip(x):
        return x ^ ((x >> 31) & 0x7FFFFFFF)

    def any_lane(flag):
        return jnp.max(jnp.where(flag, 1.0, 0.0))

    def snap(lo, hi):
        def pull(x, carry):
            kin, kax = carry
            kin = jnp.minimum(kin, _fold_rows(jnp.where(x >= lo, x, INT_MAX), jnp.minimum))
            kax = jnp.maximum(kax, _fold_rows(jnp.where(x < hi, x, INT_MIN), jnp.maximum))
            return kin, kax

        init = (jnp.full((SUBLANES, tq), INT_MAX, I32), jnp.full((SUBLANES, tq), INT_MIN, I32))
        kin, kax = pull(nkeys_ref[...], _loop(nf, lambda c, carry: pull(fkeys_ref[c], carry), init))
        return jnp.min(kin, axis=0, keepdims=True), jnp.max(kax, axis=0, keepdims=True)

    enough = n_adm >= kf
    log_k = math.log2(topk)

    def open_lanes(lo, hi, flo):
        return enough & (flo > kf) & (hi > lo + 1)

    def log_excess(c):
        return jnp.log2(jnp.maximum(c, 0.5)) - log_k

    def step(state):
        it, _, lo, hi, flo, fhi, glo, ghi, side = state
        active = open_lanes(lo, hi, flo)

        def pull_in(_):
            kin, kax = snap(lo, hi)
            return jnp.where(active, kin, lo), jnp.where(active, kax + 1, hi), flo, fhi, glo, ghi, side

        def probe(_):
            v_lo = lax.bitcast_convert_type(bits_flip(lo), F32)
            v_hi = lax.bitcast_convert_type(bits_flip(hi), F32)
            frac = jnp.minimum(jnp.maximum(glo / (glo - ghi), 0.02), 0.98)
            t_int = bits_flip(lax.bitcast_convert_type(v_lo + (v_hi - v_lo) * frac, I32))
            t_mid = (lo & hi) + ((lo ^ hi) >> 1)
            t = jnp.where(it >= BISECT_AFTER, t_mid, t_int)
            t = jnp.minimum(jnp.maximum(t, lo + 1), hi - 1)
            c = count(t, False)
            g = log_excess(c)
            up = active & (c >= kf)
            dn = active & (c < kf)
            ghi2 = jnp.where(up & (side > 0.0), ghi * 0.5, ghi)
            glo2 = jnp.where(dn & (side < 0.0), glo * 0.5, glo)
            return (jnp.where(up, t, lo), jnp.where(dn, t, hi), jnp.where(up, c, flo), jnp.where(dn, c, fhi),
                    jnp.where(up, g, glo2), jnp.where(dn, g, ghi2), jnp.where(up, 1.0, jnp.where(dn, -1.0, side)))

        is_snap = (it % SNAP_EVERY == SNAP_EVERY - 1) & (it < BISECT_AFTER)
        lo, hi, flo, fhi, glo, ghi, side = lax.cond(is_snap, pull_in, probe, 0)
        return it + 1, any_lane(open_lanes(lo, hi, flo)), lo, hi, flo, fhi, glo, ghi, side

    hi0 = kmax + 1
    zero = jnp.zeros((1, tq), F32)
    state = (jnp.int32(0), any_lane(open_lanes(kmin, hi0, n_adm)), kmin, hi0, n_adm, zero,
             log_excess(n_adm), log_excess(zero), zero)
    state = lax.while_loop(lambda st: (st[1] > 0.0) & (st[0] < MAX_SEARCH_STEPS), step, state)
    thr = jnp.where(enough, state[2], INT_MIN)
    has_tie = any_lane(enough & (state[4] > kf)) > 0.0

    @pl.when(has_tie)
    def _():
        need = kf - count(thr, True)

        def lower(n):
            a = lax.broadcasted_iota(I32, (n, n), 0)
            b = lax.broadcasted_iota(I32, (n, n), 1)
            return jnp.where(b < a, 1.0, 0.0).astype(BF16)

        def demote(keys, run, lt):
            eq = (keys == thr) & (keys > INT_MIN)
            eqf = jnp.where(eq, 1.0, 0.0)
            before = _dot(lt, eqf.astype(BF16)) + run
            keys = jnp.where(eq & (before >= need), INT_MIN, keys)
            return keys, run + jnp.sum(eqf, axis=0, keepdims=True)

        lt_far = lower(kw)

        def step(c, run):
            keys, run = demote(fkeys_ref[c], run, lt_far)
            fkeys_ref[c] = keys
            return run

        run = _loop(nf, step, jnp.zeros((1, tq), F32))
        keys, _ = demote(nkeys_ref[...], run, lower(nw))
        nkeys_ref[...] = keys

    thr_sel = jnp.maximum(thr, INT_MIN + 1)
    m_ref[...] = jnp.full(m_ref.shape, MASKED, F32)
    acc_ref[...] = jnp.zeros(acc_ref.shape, F32)

    def logits(k_blk):
        return [_dot_nt(k_blk[g], q_g[g]) for g in groups]

    def attend(slot, keys_rows, vx_blk, bias_ref, n):
        tiles = [slice(r0, r0 + SOFTMAX_ROWS) for r0 in range(0, n, SOFTMAX_ROWS)]
        heads = [slice(r * tq, (r + 1) * tq) for r in range(rep)]

        def masked_logits(g, rows, selb):
            s = ss_ref[slot, g, rows, :]
            if bias_ref is not None:
                s = s + bias_ref[g, rows, :]
            return [s[:, hd] + selb for hd in heads]

        run = [None] * N_KV_C
        for rows in tiles:
            selb = jnp.where(keys_rows(rows) >= thr_sel, 0.0, MASKED)
            for g in groups:
                top = jnp.concatenate([_fold_rows(s, jnp.maximum) for s in masked_logits(g, rows, selb)], axis=1)
                run[g] = top if run[g] is None else jnp.maximum(run[g], top)
        m_new = []
        for g in groups:
            m_old = m_ref[g]
            m_new.append(jnp.maximum(m_old, jnp.max(run[g], axis=0, keepdims=True)))
            acc_ref[g] = jnp.exp2(m_old - m_new[g]) * acc_ref[g]
            m_ref[g] = m_new[g]
        for rows in tiles:
            selb = jnp.where(keys_rows(rows) >= thr_sel, 0.0, MASKED)
            for g in groups:
                for hd, s in zip(heads, masked_logits(g, rows, selb)):
                    p_ref[g, rows, hd] = jnp.exp2(s - m_new[g][:, hd]).astype(BF16)
        for g in groups:
            acc_ref[g] += _dot(vx_blk[g], p_ref[g, :n, :])

    last = fkeys_ref.shape[0] - 1

    def put_logits(slot, k_blk, n):
        for g, s in enumerate(logits(k_blk)):
            ss_ref[slot, g, :n, :] = s

    def far_body(c, cur):
        put_logits(1 - cur, far_k(min(c + 1, last) if isinstance(c, int) else jnp.minimum(c + 1, last)), kw)
        attend(cur, lambda rows: fkeys_ref[c, rows, :], far_vx(c), None, kw)

    def far_step(c, carry):
        if isinstance(c, int):
            far_body(c, c % 2)
        else:
            for cur in range(2):
                pl.when(c % 2 == cur)(functools.partial(far_body, c, cur))
        return carry

    put_logits(0, far_k(0), kw)
    _loop(nf, far_step, 0)
    put_logits(0, near_k, nw)
    attend(0, lambda rows: nkeys_ref[rows, :], near_vx, bias_near_ref, nw)

    for g in groups:
        emit(g, acc_ref[g])


def _with_ones(vt):
    return jnp.concatenate([vt, jnp.ones((ONES_ROWS, vt.shape[1]), BF16)], axis=0)


def _dsa_scratch(nf, tq, kw, nw, vx_rows):
    rep = REP_C
    return [pltpu.VMEM((nf, kw, tq), I32),
            pltpu.VMEM((nw, tq), I32),
            pltpu.VMEM((N_KV_C, 1, rep * tq), F32),
            pltpu.VMEM((N_KV_C, vx_rows, rep * tq), F32),
            pltpu.VMEM((2, N_KV_C, max(kw, nw), rep * tq), F32),
            pltpu.VMEM((N_KV_C, max(kw, nw), rep * tq), BF16)]


def _dsa_prompt_kernel(q_ref, qi_ref, misc_ref, k_ref, vt_far_ref, vt_near_ref, ki_ref, bias_near_ref,
                       out_ref, *scratch, tq, topk):
    i = pl.program_id(0)
    kw = FAR_KW
    nw = NEAR_SPAN + tq
    q0 = i * tq
    far_limit = q0 - NEAR_SPAN
    near0 = jnp.maximum(far_limit, 0)
    nf = (near0 + kw - 1) // kw

    def rows(c):
        return pl.ds(pl.multiple_of(c * kw, kw), kw)

    def far_adm(c):
        return (c * kw + lax.broadcasted_iota(I32, (kw, tq), 0)) < far_limit

    left = pl.ds(pl.multiple_of(near0, NEAR_SPAN), NEAR_SPAN)
    right = pl.ds(pl.multiple_of(q0, tq), tq)
    near_ki = jnp.concatenate([ki_ref[left, :], ki_ref[right, :]], axis=0)
    near_k = jnp.concatenate([k_ref[left, :], k_ref[right, :]], axis=0)
    per = tq // NEAR_SPAN
    near_vt = jnp.concatenate([vt_near_ref[near0 // NEAR_SPAN]] + [vt_near_ref[i * per + j] for j in range(per)],
                              axis=1)
    key = lax.broadcasted_iota(I32, (nw, tq), 0)
    qry = lax.broadcasted_iota(I32, (nw, tq), 1)
    first_key = jnp.where(i > 0, 0, NEAR_SPAN)
    near_adm = (key >= first_key) & (((key - NEAR_SPAN) >> 6) <= (qry >> 6))
    n_adm = ((((q0 + lax.broadcasted_iota(I32, (1, tq), 1)) >> 6) + 1) * CHUNK).astype(F32)

    qi_h, wi_t, q_h = _scaled_queries(q_ref[...], qi_ref[...], misc_ref[...].T)
    qi_all = jnp.concatenate(qi_h, axis=0)
    q_g = [jnp.concatenate(q_h[g * REP_C:(g + 1) * REP_C], axis=0) for g in range(N_KV_C)]

    def emit(g, acc):
        o = acc[:HD_C] / acc[HD_C:HD_C + 1]
        for r in range(REP_C):
            hh = g * REP_C + r
            out_ref[:, hh * HD_C:(hh + 1) * HD_C] = o[:, r * tq:(r + 1) * tq].T.astype(BF16)

    _dsa_core([qi_all], wi_t, q_g,
              lambda c: [ki_ref[rows(c), :]], lambda c: [k_ref[rows(c), sl] for sl in KV_SLICES],
              lambda c: [_with_ones(vt_far_ref[c, sl, :]) for sl in KV_SLICES],
              nf, far_adm, [near_ki], [near_k[:, sl] for sl in KV_SLICES],
              [_with_ones(near_vt[sl, :]) for sl in KV_SLICES], near_adm, n_adm,
              bias_near_ref, emit, scratch,
              tq=tq, kw=kw, nw=nw, topk=topk)


def _dsa_prompt(h, kb, vt_far, vt_near, kib, bias_near, t):
    tq = PROMPT_TQ
    topk = min(TOPK_MAX, t // 4)
    return pl.pallas_call(
        functools.partial(_dsa_prompt_kernel, tq=tq, topk=topk),
        grid=(t // tq,),
        in_specs=[pl.BlockSpec((tq, WIDTH_C), lambda i: (i, C_QC // WIDTH_C)),
                  pl.BlockSpec((tq, QI_C), lambda i: (i, C_QIC // QI_C)),
                  pl.BlockSpec((tq, LANES), lambda i: (i, C_MISC // LANES)),
                  pl.BlockSpec((t, KV_C), lambda i: (0, 0)),
                  pl.BlockSpec(vt_far.shape, lambda i: (0, 0, 0)),
                  pl.BlockSpec(vt_near.shape, lambda i: (0, 0, 0)),
                  pl.BlockSpec((t, D_IDX), lambda i: (0, 0)),
                  pl.BlockSpec(bias_near.shape, lambda i: (0, 0, 0))],
        out_specs=pl.BlockSpec((tq, WIDTH_C), lambda i: (i, 0)),
        out_shape=jax.ShapeDtypeStruct((t, WIDTH_C), BF16),
        scratch_shapes=_dsa_scratch(t // FAR_KW, tq, FAR_KW, NEAR_SPAN + tq, HD_C + ONES_ROWS),
        compiler_params=_cparams(("arbitrary",)),
        name="dsa_prompt",
    )(h, h, h, kb, vt_far, vt_near, kib, bias_near)


def _dsa_sample_kernel(q_ref, qi_ref, misc_ref, kvn_ref, *refs, kw, nf, topk):
    nb = SAMPLE_PAIR
    pk_ref, pv_ref = refs[:nb], refs[nb:2 * nb]
    pki_ref, bias_near_ref, out_ref = refs[2 * nb:2 * nb + 3]
    scratch = refs[2 * nb + 3:]
    tq = nb * CHUNK
    far_len = kw * nf
    vx_rows = HD_C + ONES_ROWS
    misc = misc_ref[...]
    kvn = kvn_ref[...]
    lane_seq = lax.broadcasted_iota(I32, (1, tq), 1) >> 6
    q, qi, misc_t = q_ref[...], qi_ref[...], misc.T

    def own_rows(x, b):
        return jnp.where((lax.broadcasted_iota(I32, x.shape, 0) >> 6) == b, x, 0.0)

    per_seq = [_scaled_queries(own_rows(q, b), own_rows(qi, b), misc_t) for b in range(nb)]
    wi_t = per_seq[0][1]
    qi_all = [jnp.concatenate(per_seq[b][0], axis=0) for b in range(nb)]
    q_g = [jnp.concatenate([jnp.concatenate([per_seq[b][2][g * REP_C + r] for b in range(nb)], axis=1)
                            for r in range(REP_C)], axis=0) for g in range(N_KV_C)]
    seq_rows = [slice(b * CHUNK, (b + 1) * CHUNK) for b in range(nb)]

    def group_rows(refs, b, start, n, g):
        return refs[b][0, pl.ds(2 * start + g, n, stride=2), :]

    def side_by_side(per_seq):
        return jnp.concatenate([per_seq(b) for b in range(nb)], axis=1).astype(BF16)

    def stacked_vx(per_seq):
        return jnp.concatenate([_with_ones(per_seq(b).astype(BF16)) for b in range(nb)], axis=0)

    def far_ki(c):
        return [pki_ref[b, pl.ds(c * kw, kw), :].astype(BF16) for b in range(nb)]

    def far_k(c):
        return [side_by_side(lambda b: group_rows(pk_ref, b, c * kw, kw, g)) for g in range(N_KV_C)]

    def far_vx(c):
        return [stacked_vx(lambda b: group_rows(pv_ref, b, c * kw, kw, g).T) for g in range(N_KV_C)]

    near_ki = [jnp.concatenate([pki_ref[b, pl.ds(far_len, NEAR_SPAN), :], misc[seq_rows[b], :D_IDX]],
                               axis=0).astype(BF16) for b in range(nb)]
    near_k = [side_by_side(lambda b: jnp.concatenate(
        [group_rows(pk_ref, b, far_len, NEAR_SPAN, g), kvn[seq_rows[b], KV_SLICES[g]]], axis=0))
        for g in range(N_KV_C)]
    near_vx = [stacked_vx(lambda b: jnp.concatenate(
        [group_rows(pv_ref, b, far_len, NEAR_SPAN, g).T, kvn[seq_rows[b], KV_C + g * HD_C:KV_C + (g + 1) * HD_C].T],
        axis=1)) for g in range(N_KV_C)]

    def emit(g, acc):
        for r in range(REP_C):
            hh = g * REP_C + r
            cols = slice(r * tq, (r + 1) * tq)
            o = None
            for b in range(nb):
                top = b * vx_rows
                ob = acc[top:top + HD_C, cols] / acc[top + HD_C:top + HD_C + 1, cols]
                o = ob if o is None else jnp.where(lane_seq == b, ob, o)
            out_ref[:, hh * HD_C:(hh + 1) * HD_C] = o.T.astype(BF16)

    _dsa_core(qi_all, wi_t, q_g, far_ki, far_k, far_vx, nf, None, near_ki, near_k, near_vx, None,
              jnp.full((1, tq), float(far_len + NEAR_SPAN + CHUNK), F32),
              bias_near_ref, emit, scratch,
              tq=tq, kw=kw, nw=NEAR_SPAN + CHUNK, topk=topk)


def _dsa_sample(h, pk, pv, pki, bias_near, t, l):
    nb, p = pki.shape[1], pki.shape[2]
    tq = SAMPLE_PAIR * CHUNK
    far_len = p - NEAR_SPAN
    kw = _pick(far_len, (640, 512, 384, 256, 128))
    nf = far_len // kw
    nw = NEAR_SPAN + CHUNK
    topk = min(TOPK_MAX, (p + CHUNK) // 4)
    row0 = t // tq

    def cache_block(s, b):
        return (l, SAMPLE_PAIR * b + s, 0, 0)

    return pl.pallas_call(
        functools.partial(_dsa_sample_kernel, kw=kw, nf=nf, topk=topk),
        grid=(nb // SAMPLE_PAIR,),
        in_specs=[pl.BlockSpec((tq, WIDTH_C), lambda b: (row0 + b, C_QC // WIDTH_C)),
                  pl.BlockSpec((tq, QI_C), lambda b: (row0 + b, C_QIC // QI_C)),
                  pl.BlockSpec((tq, LANES), lambda b: (row0 + b, C_MISC // LANES)),
                  pl.BlockSpec((tq, 2 * KV_C), lambda b: (row0 + b, C_KC // (2 * KV_C)))]
                 + 2 * [pl.BlockSpec((None, 1, N_KV_C * p, HD_C), functools.partial(cache_block, s))
                        for s in range(SAMPLE_PAIR)]
                 + [pl.BlockSpec((None, SAMPLE_PAIR, p, D_IDX), lambda b: (l, b, 0, 0)),
                    pl.BlockSpec(bias_near.shape, lambda b: (0, 0, 0))],
        out_specs=pl.BlockSpec((tq, WIDTH_C), lambda b: (b, 0)),
        out_shape=jax.ShapeDtypeStruct((nb * CHUNK, WIDTH_C), BF16),
        scratch_shapes=_dsa_scratch(nf, tq, kw, nw, SAMPLE_PAIR * (HD_C + ONES_ROWS)),
        compiler_params=_cparams(("arbitrary",)),
        name="dsa_sample",
    )(h, h, h, h, *(SAMPLE_PAIR * [pk]), *(SAMPLE_PAIR * [pv]), pki, bias_near)


def _outproj_kernel(x_ref, ya_ref, yb_ref, yc_ref, wa_ref, wb_ref, wc_ref, o_ref):
    o_ref[...] = (x_ref[...] + _dot(ya_ref[...], wa_ref[...]) + _dot(yb_ref[...], wb_ref[...])
                  + _dot(yc_ref[...], wc_ref[...]))


def _outproj(x, ya, yb, yc, wa, wb, wc, l):
    m = x.shape[0]
    tm = _pick(m, (1024, 512, 256, 128))
    tn = 512
    return pl.pallas_call(
        _outproj_kernel,
        grid=(m // tm, D_MODEL // tn),
        in_specs=[pl.BlockSpec((tm, tn), lambda i, j: (i, j)),
                  pl.BlockSpec((tm, WIDTH_A), lambda i, j: (i, 0)),
                  pl.BlockSpec((tm, WIDTH_B), lambda i, j: (i, 0)),
                  pl.BlockSpec((tm, WIDTH_C), lambda i, j: (i, 0)),
                  pl.BlockSpec((None, WIDTH_A, tn), lambda i, j: (l, 0, j)),
                  pl.BlockSpec((None, WIDTH_B, tn), lambda i, j: (l, 0, j)),
                  pl.BlockSpec((None, WIDTH_C, tn), lambda i, j: (l, 0, j))],
        out_specs=pl.BlockSpec((tm, tn), lambda i, j: (i, j)),
        out_shape=jax.ShapeDtypeStruct((m, D_MODEL), F32),
        compiler_params=_cparams(("arbitrary", "arbitrary")),
        name="outproj",
    )(x, ya, yb, yc, wa, wb, wc)


def _ffn_kernel(x_ref, g_ref, wu_ref, wd_ref, o_ref, xn_ref, acc_ref):
    f = pl.program_id(1)

    @pl.when(f == 0)
    def _():
        xn_ref[...] = (_rms(x_ref[...]) * g_ref[...]).astype(BF16)
        acc_ref[...] = jnp.zeros(acc_ref.shape, F32)

    a = jnp.maximum(_dot(xn_ref[...], wu_ref[...]), 0.0)
    acc_ref[...] += _dot((a * a).astype(BF16), wd_ref[...])

    @pl.when(f == pl.num_programs(1) - 1)
    def _():
        o_ref[...] = x_ref[...] + acc_ref[...]


def _ffn(x, g, wu, wd, l):
    m = x.shape[0]
    tm = _pick(m, (512, 256, 128))
    tf = 1024
    return pl.pallas_call(
        _ffn_kernel,
        grid=(m // tm, D_FF // tf),
        in_specs=[pl.BlockSpec((tm, D_MODEL), lambda i, f: (i, 0)),
                  pl.BlockSpec((None, 1, D_MODEL), lambda i, f: (l, 0, 0)),
                  pl.BlockSpec((None, D_MODEL, tf), lambda i, f: (l, 0, f)),
                  pl.BlockSpec((None, tf, D_MODEL), lambda i, f: (l, f, 0))],
        out_specs=pl.BlockSpec((tm, D_MODEL), lambda i, f: (i, 0)),
        out_shape=jax.ShapeDtypeStruct((m, D_MODEL), F32),
        scratch_shapes=[pltpu.VMEM((tm, D_MODEL), BF16), pltpu.VMEM((tm, D_MODEL), F32)],
        compiler_params=_cparams(("arbitrary", "arbitrary")),
        name="ffn",
    )(x, g, wu, wd)


def _norm_kernel(x_ref, g_ref, o_ref):
    o_ref[...] = _rms(x_ref[...]) * g_ref[...]


def _final_norm(x, g):
    m = x.shape[0]
    tm = _pick(m, (1024, 512, 256, 128))
    return pl.pallas_call(
        _norm_kernel,
        grid=(m // tm,),
        in_specs=[pl.BlockSpec((tm, D_MODEL), lambda i: (i, 0)),
                  pl.BlockSpec((1, D_MODEL), lambda i: (0, 0))],
        out_specs=pl.BlockSpec((tm, D_MODEL), lambda i: (i, 0)),
        out_shape=jax.ShapeDtypeStruct((m, D_MODEL), F32),
        compiler_params=_cparams(("arbitrary",)),
        name="final_norm",
    )(x, g)


def _t5_bucket(rel):
    half = N_BUCKETS // 2
    max_exact = half // 2
    n = jnp.abs(rel)
    nf = jnp.maximum(n, 1).astype(F32)
    large = max_exact + (jnp.log(nf / max_exact) / jnp.log(MAX_DISTANCE / max_exact)
                         * (half - max_exact)).astype(I32)
    large = jnp.minimum(large, half - 1)
    return jnp.where(rel > 0, half, 0) + jnp.where(n < max_exact, n, large)


def _bias_table(rel_bias, tq, nw, copies=1):
    j = jnp.arange(nw, dtype=I32)[:, None]
    t = jnp.arange(tq, dtype=I32)[None, :]
    hot = jax.nn.one_hot(_t5_bucket(j - NEAR_SPAN - t), N_BUCKETS, dtype=F32)
    near = jnp.einsum("jtb,bh->jth", hot, rel_bias, precision=lax.Precision.HIGHEST)
    far = rel_bias[_t5_bucket(jnp.int32(-(NEAR_SPAN + 1)))]
    rep = N_HEADS_C // N_KV_C
    near = ((near - far) * LOG2E).astype(F32).reshape(nw, tq, N_KV_C, rep)
    near = jnp.transpose(near, (2, 0, 3, 1))[:, :, :, None, :]
    return jnp.broadcast_to(near, (N_KV_C, nw, rep, copies, tq)).reshape(N_KV_C, nw, rep * copies * tq)


def _permute_w_in(w):
    sizes = (WIDTH_A, WIDTH_A, QK_B, QK_B, WIDTH_B, GATE_RANK_B, WIDTH_B, WIDTH_C, KV_C, KV_C, QI_C,
             D_IDX, N_IDX_HEADS)
    parts, off = [], 0
    for n in sizes:
        parts.append(w[..., off:off + n])
        off += n
    ua, va, qb, kb, vb, lrb, og, qc, kc, vc, qic, kic, wic = parts
    pad = jnp.zeros(w.shape[:-1] + (H_WIDTH - (C_MISC + D_IDX + GATE_RANK_B + N_IDX_HEADS),), w.dtype)
    return jnp.concatenate([qb, kb, vb, og, qc, kc, vc, qic, ua, va, kic, lrb, wic, pad], axis=-1).astype(BF16)


def kernel(x_prompt, x_sample, cache_c_k, cache_c_v, cache_c_kidx, state_b_s, norm_mix, w_in, norm_a_v,
           w_s_a, b_s_a, w_gate_b, b_gate_b, norm_b_o, rel_bias, w_o, norm_ffn, w_up, w_down, norm_final):
    depth = w_in.shape[0]
    nbp, t, _ = x_prompt.shape
    nbs, ts, _ = x_sample.shape
    p = cache_c_k.shape[2]
    assert nbp == 1 and ts == CHUNK and t % FAR_KW == 0 and t % GMLP_ROWS == 0 and (nbs * ts) % GMLP_ROWS == 0
    assert p % NEAR_SPAN == 0 and p >= 2 * NEAR_SPAN
    assert t % PROMPT_TQ == 0 and t % (GLA_SUB * CHUNK) == 0 and nbs % GLA_SUB == 0

    x = jnp.concatenate([x_prompt[0], x_sample.reshape(nbs * ts, D_MODEL)], axis=0)
    w_in_p = _permute_w_in(w_in)
    wo_a = w_o[:, :WIDTH_A].astype(BF16)
    wo_b = w_o[:, WIDTH_A:WIDTH_A + WIDTH_B].astype(BF16)
    wo_c = w_o[:, WIDTH_A + WIDTH_B:].astype(BF16)
    w_up_b = w_up.astype(BF16)
    w_down_b = w_down.astype(BF16)
    w_gate = w_gate_b.astype(BF16)
    np_, ns_ = GMLP_ROWS // GROUP_A, GMLP_ROWS // CHUNK
    w_s2 = jnp.stack([jnp.tile(w_s_a, (1, 1, np_, np_)),
                      jnp.tile(w_s_a[:, :, :CHUNK, :CHUNK], (1, 1, ns_, ns_))], axis=1)
    b_s2 = jnp.stack([jnp.tile(b_s_a, (1, 1, np_)), jnp.tile(b_s_a[:, :, :CHUNK], (1, 1, ns_))], axis=1)
    b_s2 = jnp.swapaxes(b_s2, 2, 3)
    s_all = jnp.concatenate([jnp.zeros((depth, GLA_SUB) + state_b_s.shape[2:], F32), state_b_s], axis=1)
    bias_near = _bias_table(rel_bias, PROMPT_TQ, NEAR_SPAN + PROMPT_TQ)
    bias_near_s = _bias_table(rel_bias, CHUNK, NEAR_SPAN + CHUNK, copies=SAMPLE_PAIR)
    pk_all = cache_c_k.reshape(depth, nbs, N_KV_C * p, HD_C)
    pv_all = cache_c_v.reshape(depth, nbs, N_KV_C * p, HD_C)

    row = lambda a: a[:, None, :]
    outs = [[] for _ in range(9)]
    for l in range(depth):
        h = _inproj(x, row(norm_mix), w_in_p, l)
        hp, hs = h[:t], h[t:]
        ya, vn = _gmlp(h, row(norm_a_v), w_s2, b_s2, t, l)
        yb, s_out = _gla(h, w_gate, row(b_gate_b), row(norm_b_o), s_all, t, l)
        kb = hp[:, C_KC:C_KC + KV_C].astype(BF16)
        vt = hp[:, C_VC:C_VC + KV_C].astype(BF16).T
        vt_far = jnp.swapaxes(vt.reshape(KV_C, t // FAR_KW, FAR_KW), 0, 1)
        vt_near = jnp.swapaxes(vt.reshape(KV_C, t // NEAR_SPAN, NEAR_SPAN), 0, 1)
        kib = hp[:, C_MISC:C_MISC + D_IDX].astype(BF16)
        yc_p = _dsa_prompt(h, kb, vt_far, vt_near, kib, bias_near, t)
        yc_s = _dsa_sample(h, pk_all, pv_all, cache_c_kidx, bias_near_s, t, l)
        yc = jnp.concatenate([yc_p, yc_s], axis=0)
        x = _outproj(x, ya, yb, yc, wo_a, wo_b, wo_c, l)
        x = _ffn(x, row(norm_ffn), w_up_b, w_down_b, l)

        outs[0].append(hp[:, C_KC:C_KC + KV_C].reshape(1, t, N_KV_C, HD_C))
        outs[1].append(hp[:, C_VC:C_VC + KV_C].reshape(1, t, N_KV_C, HD_C))
        outs[2].append(hp[:, C_MISC:C_MISC + D_IDX].reshape(1, t, D_IDX))
        outs[3].append(s_out[GLA_SUB - 1:GLA_SUB])
        outs[4].append(hs[:, C_KC:C_KC + KV_C].reshape(nbs, ts, N_KV_C, HD_C))
        outs[5].append(hs[:, C_VC:C_VC + KV_C].reshape(nbs, ts, N_KV_C, HD_C))
        outs[6].append(hs[:, C_MISC:C_MISC + D_IDX].reshape(nbs, ts, D_IDX))
        outs[7].append(s_out[GLA_SUB:])
        outs[8].append(vn.reshape(nbs, ts, WIDTH_A))

    y = _final_norm(x, norm_final[None])
    return (y[:t][None], y[t:].reshape(nbs, ts, D_MODEL)) + tuple(jnp.stack(o) for o in outs)
```

```python
import functools
import math

import jax
import jax.numpy as jnp
from jax import lax
from jax.experimental import pallas as pl
from jax.experimental.pallas import tpu as pltpu

BF16 = jnp.bfloat16
F32 = jnp.float32
I32 = jnp.int32

D_MODEL = 2048
EPS = 1e-6
CHUNK = 64
GROUP_A = 128
N_GROUPS_A = 4
WIDTH_A = 512
N_HEADS_B = 6
DK_B = 64
DV_B = 128
GATE_RANK_B = 16
GATE_TEMP_B = 16.0
WIDTH_B = 768
N_HEADS_C = 6
N_KV_C = 2
HD_C = 128
N_IDX_HEADS = 8
D_IDX = 64
TOPK_MAX = 256
WIDTH_C = 768
N_BUCKETS = 32
MAX_DISTANCE = 128
D_FF = 4 * D_MODEL
QK_B = N_HEADS_B * DK_B
KV_C = N_KV_C * HD_C
QI_C = N_IDX_HEADS * D_IDX

LANES = 128
SUBLANES = 8
ONES_ROWS = 16
VMEM_LIMIT = 56 * 1024 * 1024

C_QB = 0
C_KB = 384
C_VB = 768
C_OG = 1536
C_QC = 2304
C_KC = 3072
C_VC = 3328
C_QIC = 3584
C_UA = 4096
C_VA = 4608
C_MISC = 5120
MISC_LRB = D_IDX
MISC_WIC = D_IDX + GATE_RANK_B
H_WIDTH = 5376

INT_MIN = -2147483648
INT_MAX = 2147483647
SNAP_EVERY = 8
BISECT_AFTER = 40
MAX_SEARCH_STEPS = 80
MASKED = -1e30
LOG2E = 1.4426950408889634
NEAR_SPAN = 128
FAR_KW = 512
PROMPT_TQ = 256
SOFTMAX_ROWS = 64
GLA_SUB = 4
GMLP_ROWS = 256
SAMPLE_PAIR = 1


def _pick(n, cands):
    for c in cands:
        if n % c == 0:
            return c
    raise ValueError(f"no tile for {n}")


def _cparams(sem):
    return pltpu.CompilerParams(dimension_semantics=sem, vmem_limit_bytes=VMEM_LIMIT)


def _rms(x):
    return x * lax.rsqrt(jnp.mean(x * x, axis=-1, keepdims=True) + EPS)


def _dot(a, b):
    return jnp.dot(a, b, preferred_element_type=F32)


def _dot_nt(a, b):
    return lax.dot_general(a, b, (((1,), (1,)), ((), ())), preferred_element_type=F32)


def _dot_tn(a, b):
    return lax.dot_general(a, b, (((0,), (0,)), ((), ())), preferred_element_type=F32)


def _inproj_kernel(x_ref, g_ref, w_ref, o_ref, xn_ref):
    @pl.when(pl.program_id(1) == 0)
    def _():
        xn_ref[...] = (_rms(x_ref[...]) * g_ref[...]).astype(BF16)

    o_ref[...] = _dot_nt(xn_ref[...], w_ref[...])


def _inproj(x, g, w, l):
    m = x.shape[0]
    tm = _pick(m, (1024, 512, 256, 128))
    tn = 768
    return pl.pallas_call(
        _inproj_kernel,
        grid=(m // tm, H_WIDTH // tn),
        in_specs=[pl.BlockSpec((tm, D_MODEL), lambda i, j: (i, 0)),
                  pl.BlockSpec((None, 1, D_MODEL), lambda i, j: (l, 0, 0)),
                  pl.BlockSpec((None, tn, D_MODEL), lambda i, j: (l, j, 0))],
        out_specs=pl.BlockSpec((tm, tn), lambda i, j: (i, j)),
        out_shape=jax.ShapeDtypeStruct((m, H_WIDTH), F32),
        scratch_shapes=[pltpu.VMEM((tm, D_MODEL), BF16)],
        compiler_params=_cparams(("arbitrary", "arbitrary")),
        name="inproj",
    )(x, g, w)


def _gmlp_kernel(h_ref, gv_ref, w_ref, b_ref, ya_ref, vn_ref, *, n_prompt_blocks):
    i = pl.program_id(0)
    hv = h_ref[...]
    u = jax.nn.gelu(hv[:, :WIDTH_A])
    v = jax.nn.gelu(hv[:, WIDTH_A:])
    r = lax.broadcasted_iota(I32, (GMLP_ROWS, GMLP_ROWS), 0)
    c = lax.broadcasted_iota(I32, (GMLP_ROWS, GMLP_ROWS), 1)
    shift = 7 - (i >= n_prompt_blocks).astype(I32)
    keep = (c <= r) & (jnp.right_shift(r, shift) == jnp.right_shift(c, shift))
    for g in range(N_GROUPS_A):
        sl = slice(g * GROUP_A, (g + 1) * GROUP_A)
        vn = _rms(v[:, sl]) * gv_ref[:, sl]
        vn_ref[:, sl] = vn
        wm = jnp.where(keep, w_ref[0, g], 0.0).astype(BF16)
        z = _dot(wm, vn.astype(BF16)) + b_ref[0][:, g:g + 1]
        ya_ref[:, sl] = (u[:, sl] * z).astype(BF16)


def _gmlp(h, gv, w2, b2, n_prompt_rows, l):
    m = h.shape[0]
    rows = GMLP_ROWS
    npb = n_prompt_rows // rows
    nsb = (m - n_prompt_rows) // rows
    return pl.pallas_call(
        functools.partial(_gmlp_kernel, n_prompt_blocks=npb),
        grid=(npb + nsb,),
        in_specs=[pl.BlockSpec((rows, 2 * WIDTH_A), lambda i: (i, C_UA // (2 * WIDTH_A))),
                  pl.BlockSpec((None, 1, WIDTH_A), lambda i: (l, 0, 0)),
                  pl.BlockSpec((None, 1, N_GROUPS_A, rows, rows),
                               lambda i: (l, jnp.where(i >= npb, 1, 0), 0, 0, 0)),
                  pl.BlockSpec((None, 1, rows, N_GROUPS_A),
                               lambda i: (l, jnp.where(i >= npb, 1, 0), 0, 0))],
        out_specs=[pl.BlockSpec((rows, WIDTH_A), lambda i: (i, 0)),
                   pl.BlockSpec((rows, WIDTH_A), lambda i: (jnp.maximum(i - npb, 0), 0))],
        out_shape=[jax.ShapeDtypeStruct((m, WIDTH_A), BF16),
                   jax.ShapeDtypeStruct((nsb * rows, WIDTH_A), F32)],
        compiler_params=_cparams(("arbitrary",)),
        name="gmlp",
    )(h, gv, w2, b2)


def _gla_kernel(q_ref, k_ref, v_ref, og_ref, misc_ref, wg_ref, bg_ref, gbo_ref, s0_ref,
                yb_ref, sout_ref, st_ref, *, n_prompt_steps):
    j = pl.program_id(0)
    rows = GLA_SUB * CHUNK
    fresh = j >= n_prompt_steps

    lrb = misc_ref[:, MISC_LRB:MISC_LRB + GATE_RANK_B].astype(BF16)
    x = _dot(lrb, wg_ref[...]) + bg_ref[...]
    g = (jnp.minimum(x, 0.0) - jnp.log1p(jnp.exp(-jnp.abs(x)))) * (1.0 / GATE_TEMP_B)
    rr = lax.broadcasted_iota(I32, (rows, rows), 0)
    cc = lax.broadcasted_iota(I32, (rows, rows), 1)
    same = (rr >> 6) == (cc >> 6)
    tril = same & (cc <= rr)
    g1 = g.astype(BF16)
    r1 = g - g1.astype(F32)
    g2 = r1.astype(BF16)
    g3 = (r1 - g2.astype(F32)).astype(BF16)

    def chunk_sums(mask):
        m = jnp.where(mask, 1.0, 0.0).astype(BF16)
        return _dot(m, g1) + _dot(m, g2) + _dot(m, g3)

    b = chunk_sums(tril)
    b_last = chunk_sums(same)
    b_mid = chunk_sums(same & ((cc & (CHUNK - 1)) < CHUNK // 2))
    qs = q_ref[...] * (DK_B ** -0.5)
    kk = k_ref[...]
    q_inter = (qs * jnp.exp(b)).astype(BF16)
    q_intra = (qs * jnp.exp(b - b_mid)).astype(BF16)
    k_intra = (kk * jnp.exp(b_mid - b)).astype(BF16)
    k_state = (kk * jnp.exp(b_last - b)).astype(BF16)
    dec = jnp.exp(b_last)
    for h in range(N_HEADS_B):
        sk = slice(h * DK_B, (h + 1) * DK_B)
        sv = slice(h * DV_B, (h + 1) * DV_B)
        vh = v_ref[:, sv].astype(BF16)
        att = jnp.where(tril, _dot_nt(q_intra[:, sk], k_intra[:, sk]), 0.0)
        o_intra = _dot(att.astype(BF16), vh)
        st = st_ref[h]
        o_inter = []
        for c in range(GLA_SUB):
            rc = slice(c * CHUNK, (c + 1) * CHUNK)
            start = fresh | ((j == 0) & (c == 0))
            st = jnp.where(start, s0_ref[c, h].T, st)
            o_inter.append(_dot_nt(q_inter[rc, sk], st.astype(BF16)))
            st = st * dec[c * CHUNK:c * CHUNK + 1, sk] + _dot_tn(vh[rc], k_state[rc, sk])
            sout_ref[c, h] = st.T
        st_ref[h] = st
        o = jnp.concatenate(o_inter, axis=0) + o_intra
        on = _rms(o) * gbo_ref[:, sv]
        yb_ref[:, sv] = (on * jax.nn.silu(og_ref[:, sv])).astype(BF16)


def _gla(h, wg, bg, gbo, s_in, n_prompt_rows, l):
    m = h.shape[0]
    rows = GLA_SUB * CHUNK
    nps = n_prompt_rows // rows
    nslots = s_in.shape[1]

    def blk(j):
        return jnp.where(j < nps, 0, j - nps + 1)

    return pl.pallas_call(
        functools.partial(_gla_kernel, n_prompt_steps=nps),
        grid=(m // rows,),
        in_specs=[pl.BlockSpec((rows, QK_B), lambda j: (j, C_QB // QK_B)),
                  pl.BlockSpec((rows, QK_B), lambda j: (j, C_KB // QK_B)),
                  pl.BlockSpec((rows, WIDTH_B), lambda j: (j, C_VB // WIDTH_B)),
                  pl.BlockSpec((rows, WIDTH_B), lambda j: (j, C_OG // WIDTH_B)),
                  pl.BlockSpec((rows, LANES), lambda j: (j, C_MISC // LANES)),
                  pl.BlockSpec((None, GATE_RANK_B, QK_B), lambda j: (l, 0, 0)),
                  pl.BlockSpec((None, 1, QK_B), lambda j: (l, 0, 0)),
                  pl.BlockSpec((None, 1, WIDTH_B), lambda j: (l, 0, 0)),
                  pl.BlockSpec((None, GLA_SUB, N_HEADS_B, DK_B, DV_B), lambda j: (l, blk(j), 0, 0, 0))],
        out_specs=[pl.BlockSpec((rows, WIDTH_B), lambda j: (j, 0)),
                   pl.BlockSpec((GLA_SUB, N_HEADS_B, DK_B, DV_B), lambda j: (blk(j), 0, 0, 0))],
        out_shape=[jax.ShapeDtypeStruct((m, WIDTH_B), BF16),
                   jax.ShapeDtypeStruct((nslots, N_HEADS_B, DK_B, DV_B), F32)],
        scratch_shapes=[pltpu.VMEM((N_HEADS_B, DV_B, DK_B), F32)],
        compiler_params=_cparams(("arbitrary",)),
        name="gla",
    )(h, h, h, h, h, wg, bg, gbo, s_in)


def _fold_rows(x, op):
    tile = SUBLANES * (4 // x.dtype.itemsize)
    parts = [x[j * tile:(j + 1) * tile] for j in range(x.shape[0] // tile)]
    while len(parts) > 1:
        nxt = [op(parts[j], parts[j + 1]) for j in range(0, len(parts) - 1, 2)]
        parts = nxt + parts[len(parts) - len(parts) % 2:]
    return parts[0]


def _loop(n, body, init):
    if isinstance(n, int):
        carry = init
        for c in range(n):
            carry = body(c, carry)
        return carry
    return lax.fori_loop(0, n, body, init)


REP_C = N_HEADS_C // N_KV_C
KV_SLICES = [slice(g * HD_C, (g + 1) * HD_C) for g in range(N_KV_C)]


def _scaled_queries(q, qi, misc_t):
    qi_s = (qi * (D_IDX ** -0.5)).astype(BF16)
    wi_t = misc_t[MISC_WIC:MISC_WIC + N_IDX_HEADS, :] * (N_IDX_HEADS ** -0.5)
    qs = (q * (HD_C ** -0.5 * LOG2E)).astype(BF16)
    return ([qi_s[:, h * D_IDX:(h + 1) * D_IDX] for h in range(N_IDX_HEADS)], wi_t,
            [qs[:, hh * HD_C:(hh + 1) * HD_C] for hh in range(N_HEADS_C)])


def _dsa_core(qi_all, wi_t, q_g, far_ki, far_k, far_vx, nf, far_adm, near_ki, near_k, near_vx, near_adm, n_adm,
              bias_near_ref, emit, scratch, *, tq, kw, nw, topk):
    fkeys_ref, nkeys_ref, m_ref, acc_ref, ss_ref, p_ref = scratch
    rep = REP_C
    groups = range(N_KV_C)
    kf = float(topk)

    def keys_of(ki_blk, adm):
        d = _dot_nt(ki_blk[0], qi_all[0])
        for kb, qb in zip(ki_blk[1:], qi_all[1:]):
            d = d + _dot_nt(kb, qb)
        acc = None
        for h in range(N_IDX_HEADS):
            t = wi_t[h:h + 1, :] * jnp.maximum(d[:, h * tq:(h + 1) * tq], 0.0)
            acc = t if acc is None else acc + t
        bits = lax.bitcast_convert_type(acc, I32)
        key = bits ^ ((bits >> 31) & 0x7FFFFFFF)
        key = jnp.where(acc == 0.0, 0, key)
        if adm is not None:
            key = jnp.where(adm, key, INT_MIN)
        return key

    def extremes(key, carry):
        kmax, kmin = carry
        kmax = jnp.maximum(kmax, _fold_rows(key, jnp.maximum))
        kmin = jnp.minimum(kmin, _fold_rows(jnp.where(key == INT_MIN, INT_MAX, key), jnp.minimum))
        return kmax, kmin

    def fill(c, carry):
        key = keys_of(far_ki(c), None if far_adm is None else far_adm(c))
        fkeys_ref[c] = key
        return extremes(key, carry)

    carry = _loop(nf, fill, (jnp.full((SUBLANES, tq), INT_MIN, I32), jnp.full((SUBLANES, tq), INT_MAX, I32)))
    key = keys_of(near_ki, near_adm)
    nkeys_ref[...] = key
    kmax, kmin = extremes(key, carry)
    kmax = jnp.max(kmax, axis=0, keepdims=True)
    kmin = jnp.min(kmin, axis=0, keepdims=True)

    def count(thr, strict):
        def cmp(x):
            hit = (x > thr) if strict else (x >= thr)
            return _fold_rows(jnp.where(hit, 1.0, 0.0), jnp.add)

        part = _loop(nf, lambda c, a: a + cmp(fkeys_ref[c]), jnp.zeros((SUBLANES, tq), F32))
        part = part + cmp(nkeys_ref[...])
        return jnp.sum(part, axis=0, keepdims=True)

    def bits_flip(x):
        return x ^ ((x >> 31) & 0x7FFFFFFF)

    def any_lane(flag):
        return jnp.max(jnp.where(flag, 1.0, 0.0))

    def snap(lo, hi):
        def pull(x, carry):
            kin, kax = carry
            kin = jnp.minimum(kin, _fold_rows(jnp.where(x >= lo, x, INT_MAX), jnp.minimum))
            kax = jnp.maximum(kax, _fold_rows(jnp.where(x < hi, x, INT_MIN), jnp.maximum))
            return kin, kax

        init = (jnp.full((SUBLANES, tq), INT_MAX, I32), jnp.full((SUBLANES, tq), INT_MIN, I32))
        kin, kax = pull(nkeys_ref[...], _loop(nf, lambda c, carry: pull(fkeys_ref[c], carry), init))
        return jnp.min(kin, axis=0, keepdims=True), jnp.max(kax, axis=0, keepdims=True)

    enough = n_adm >= kf
    log_k = math.log2(topk)

    def open_lanes(lo, hi, flo):
        return enough & (flo > kf) & (hi > lo + 1)

    def log_excess(c):
        return jnp.log2(jnp.maximum(c, 0.5)) - log_k

    def step(state):
        it, _, lo, hi, flo, fhi, glo, ghi, side = state
        active = open_lanes(lo, hi, flo)

        def pull_in(_):
            kin, kax = snap(lo, hi)
            return jnp.where(active, kin, lo), jnp.where(active, kax + 1, hi), flo, fhi, glo, ghi, side

        def probe(_):
            v_lo = lax.bitcast_convert_type(bits_flip(lo), F32)
            v_hi = lax.bitcast_convert_type(bits_flip(hi), F32)
            frac = jnp.minimum(jnp.maximum(glo / (glo - ghi), 0.02), 0.98)
            t_int = bits_flip(lax.bitcast_convert_type(v_lo + (v_hi - v_lo) * frac, I32))
            t_mid = (lo & hi) + ((lo ^ hi) >> 1)
            t = jnp.where(it >= BISECT_AFTER, t_mid, t_int)
            t = jnp.minimum(jnp.maximum(t, lo + 1), hi - 1)
            c = count(t, False)
            g = log_excess(c)
            up = active & (c >= kf)
            dn = active & (c < kf)
            ghi2 = jnp.where(up & (side > 0.0), ghi * 0.5, ghi)
            glo2 = jnp.where(dn & (side < 0.0), glo * 0.5, glo)
            return (jnp.where(up, t, lo), jnp.where(dn, t, hi), jnp.where(up, c, flo), jnp.where(dn, c, fhi),
                    jnp.where(up, g, glo2), jnp.where(dn, g, ghi2), jnp.where(up, 1.0, jnp.where(dn, -1.0, side)))

        is_snap = (it % SNAP_EVERY == SNAP_EVERY - 1) & (it < BISECT_AFTER)
        lo, hi, flo, fhi, glo, ghi, side = lax.cond(is_snap, pull_in, probe, 0)
        return it + 1, any_lane(open_lanes(lo, hi, flo)), lo, hi, flo, fhi, glo, ghi, side

    hi0 = kmax + 1
    zero = jnp.zeros((1, tq), F32)
    state = (jnp.int32(0), any_lane(open_lanes(kmin, hi0, n_adm)), kmin, hi0, n_adm, zero,
             log_excess(n_adm), log_excess(zero), zero)
    state = lax.while_loop(lambda st: (st[1] > 0.0) & (st[0] < MAX_SEARCH_STEPS), step, state)
    thr = jnp.where(enough, state[2], INT_MIN)
    has_tie = any_lane(enough & (state[4] > kf)) > 0.0

    @pl.when(has_tie)
    def _():
        need = kf - count(thr, True)

        def lower(n):
            a = lax.broadcasted_iota(I32, (n, n), 0)
            b = lax.broadcasted_iota(I32, (n, n), 1)
            return jnp.where(b < a, 1.0, 0.0).astype(BF16)

        def demote(keys, run, lt):
            eq = (keys == thr) & (keys > INT_MIN)
            eqf = jnp.where(eq, 1.0, 0.0)
            before = _dot(lt, eqf.astype(BF16)) + run
            keys = jnp.where(eq & (before >= need), INT_MIN, keys)
            return keys, run + jnp.sum(eqf, axis=0, keepdims=True)

        lt_far = lower(kw)

        def step(c, run):
            keys, run = demote(fkeys_ref[c], run, lt_far)
            fkeys_ref[c] = keys
            return run

        run = _loop(nf, step, jnp.zeros((1, tq), F32))
        keys, _ = demote(nkeys_ref[...], run, lower(nw))
        nkeys_ref[...] = keys

    thr_sel = jnp.maximum(thr, INT_MIN + 1)
    m_ref[...] = jnp.full(m_ref.shape, MASKED, F32)
    acc_ref[...] = jnp.zeros(acc_ref.shape, F32)

    def logits(k_blk):
        out = []
        for g in groups:
            s = _dot_nt(k_blk[g][0], q_g[g][0])
            for kb, qb in zip(k_blk[g][1:], q_g[g][1:]):
                s = s + _dot_nt(kb, qb)
            out.append(s)
        return out

    def attend(slot, keys_rows, vx_blk, bias_ref, n):
        tiles = [slice(r0, r0 + SOFTMAX_ROWS) for r0 in range(0, n, SOFTMAX_ROWS)]
        heads = [slice(r * tq, (r + 1) * tq) for r in range(rep)]

        def masked_logits(g, rows, selb):
            s = ss_ref[slot, g, rows, :]
            if bias_ref is not None:
                s = s + bias_ref[g, rows, :]
            return [s[:, hd] + selb for hd in heads]

        run = [None] * N_KV_C
        for rows in tiles:
            selb = jnp.where(keys_rows(rows) >= thr_sel, 0.0, MASKED)
            for g in groups:
                top = jnp.concatenate([_fold_rows(s, jnp.maximum) for s in masked_logits(g, rows, selb)], axis=1)
                run[g] = top if run[g] is None else jnp.maximum(run[g], top)
        m_new = []
        for g in groups:
            m_old = m_ref[g]
            m_new.append(jnp.maximum(m_old, jnp.max(run[g], axis=0, keepdims=True)))
            acc_ref[g] = jnp.exp2(m_old - m_new[g]) * acc_ref[g]
            m_ref[g] = m_new[g]
        for rows in tiles:
            selb = jnp.where(keys_rows(rows) >= thr_sel, 0.0, MASKED)
            for g in groups:
                for hd, s in zip(heads, masked_logits(g, rows, selb)):
                    p_ref[g, rows, hd] = jnp.exp2(s - m_new[g][:, hd]).astype(BF16)
        for g in groups:
            p = p_ref[g, :n, :]
            pv = [_dot(vx, p) for vx in vx_blk[g]]
            acc_ref[g] += pv[0] if len(pv) == 1 else jnp.concatenate(pv, axis=0)

    last = fkeys_ref.shape[0] - 1

    def put_logits(slot, k_blk, n):
        for g, s in enumerate(logits(k_blk)):
            ss_ref[slot, g, :n, :] = s

    def far_body(c, cur):
        put_logits(1 - cur, far_k(min(c + 1, last) if isinstance(c, int) else jnp.minimum(c + 1, last)), kw)
        attend(cur, lambda rows: fkeys_ref[c, rows, :], far_vx(c), None, kw)

    def far_step(c, carry):
        if isinstance(c, int):
            far_body(c, c % 2)
        else:
            for cur in range(2):
                pl.when(c % 2 == cur)(functools.partial(far_body, c, cur))
        return carry

    put_logits(0, far_k(0), kw)
    _loop(nf, far_step, 0)
    put_logits(0, near_k, nw)
    attend(0, lambda rows: nkeys_ref[rows, :], near_vx, bias_near_ref, nw)

    for g in groups:
        emit(g, acc_ref[g])


def _with_ones(vt):
    return jnp.concatenate([vt, jnp.ones((ONES_ROWS, vt.shape[1]), BF16)], axis=0)


def _dsa_scratch(nf, tq, kw, nw, vx_rows):
    rep = REP_C
    return [pltpu.VMEM((nf, kw, tq), I32),
            pltpu.VMEM((nw, tq), I32),
            pltpu.VMEM((N_KV_C, 1, rep * tq), F32),
            pltpu.VMEM((N_KV_C, vx_rows, rep * tq), F32),
            pltpu.VMEM((2, N_KV_C, max(kw, nw), rep * tq), F32),
            pltpu.VMEM((N_KV_C, max(kw, nw), rep * tq), BF16)]


def _dsa_prompt_kernel(q_ref, qi_ref, misc_ref, k_ref, vt_far_ref, vt_near_ref, ki_ref, bias_near_ref,
                       out_ref, *scratch, tq, topk):
    i = pl.program_id(0)
    kw = FAR_KW
    nw = NEAR_SPAN + tq
    q0 = i * tq
    far_limit = q0 - NEAR_SPAN
    near0 = jnp.maximum(far_limit, 0)
    nf = (near0 + kw - 1) // kw

    def rows(c):
        return pl.ds(pl.multiple_of(c * kw, kw), kw)

    def far_adm(c):
        return (c * kw + lax.broadcasted_iota(I32, (kw, tq), 0)) < far_limit

    left = pl.ds(pl.multiple_of(near0, NEAR_SPAN), NEAR_SPAN)
    right = pl.ds(pl.multiple_of(q0, tq), tq)
    near_ki = jnp.concatenate([ki_ref[left, :], ki_ref[right, :]], axis=0)
    near_k = jnp.concatenate([k_ref[left, :], k_ref[right, :]], axis=0)
    per = tq // NEAR_SPAN
    near_vt = jnp.concatenate([vt_near_ref[near0 // NEAR_SPAN]] + [vt_near_ref[i * per + j] for j in range(per)],
                              axis=1)
    key = lax.broadcasted_iota(I32, (nw, tq), 0)
    qry = lax.broadcasted_iota(I32, (nw, tq), 1)
    first_key = jnp.where(i > 0, 0, NEAR_SPAN)
    near_adm = (key >= first_key) & (((key - NEAR_SPAN) >> 6) <= (qry >> 6))
    n_adm = ((((q0 + lax.broadcasted_iota(I32, (1, tq), 1)) >> 6) + 1) * CHUNK).astype(F32)

    qi_h, wi_t, q_h = _scaled_queries(q_ref[...], qi_ref[...], misc_ref[...].T)
    qi_all = jnp.concatenate(qi_h, axis=0)
    q_g = [[jnp.concatenate(q_h[g * REP_C:(g + 1) * REP_C], axis=0)] for g in range(N_KV_C)]

    def emit(g, acc):
        o = acc[:HD_C] / acc[HD_C:HD_C + 1]
        for r in range(REP_C):
            hh = g * REP_C + r
            out_ref[:, hh * HD_C:(hh + 1) * HD_C] = o[:, r * tq:(r + 1) * tq].T.astype(BF16)

    _dsa_core([qi_all], wi_t, q_g,
              lambda c: [ki_ref[rows(c), :]], lambda c: [[k_ref[rows(c), sl]] for sl in KV_SLICES],
              lambda c: [[_with_ones(vt_far_ref[c, sl, :])] for sl in KV_SLICES],
              nf, far_adm, [near_ki], [[near_k[:, sl]] for sl in KV_SLICES],
              [[_with_ones(near_vt[sl, :])] for sl in KV_SLICES], near_adm, n_adm,
              bias_near_ref, emit, scratch,
              tq=tq, kw=kw, nw=nw, topk=topk)


def _dsa_prompt(h, kb, vt_far, vt_near, kib, bias_near, t):
    tq = PROMPT_TQ
    topk = min(TOPK_MAX, t // 4)
    return pl.pallas_call(
        functools.partial(_dsa_prompt_kernel, tq=tq, topk=topk),
        grid=(t // tq,),
        in_specs=[pl.BlockSpec((tq, WIDTH_C), lambda i: (i, C_QC // WIDTH_C)),
                  pl.BlockSpec((tq, QI_C), lambda i: (i, C_QIC // QI_C)),
                  pl.BlockSpec((tq, LANES), lambda i: (i, C_MISC // LANES)),
                  pl.BlockSpec((t, KV_C), lambda i: (0, 0)),
                  pl.BlockSpec(vt_far.shape, lambda i: (0, 0, 0)),
                  pl.BlockSpec(vt_near.shape, lambda i: (0, 0, 0)),
                  pl.BlockSpec((t, D_IDX), lambda i: (0, 0)),
                  pl.BlockSpec(bias_near.shape, lambda i: (0, 0, 0))],
        out_specs=pl.BlockSpec((tq, WIDTH_C), lambda i: (i, 0)),
        out_shape=jax.ShapeDtypeStruct((t, WIDTH_C), BF16),
        scratch_shapes=_dsa_scratch(t // FAR_KW, tq, FAR_KW, NEAR_SPAN + tq, HD_C + ONES_ROWS),
        compiler_params=_cparams(("arbitrary",)),
        name="dsa_prompt",
    )(h, h, h, kb, vt_far, vt_near, kib, bias_near)


def _dsa_sample_kernel(q_ref, qi_ref, misc_ref, *refs, kw, nf, topk):
    nb = SAMPLE_PAIR
    kvn_ref, pk_ref, pv_ref = refs[:nb], refs[nb:2 * nb], refs[2 * nb:3 * nb]
    pki_ref, bias_near_ref, out_ref = refs[3 * nb:3 * nb + 3]
    scratch = refs[3 * nb + 3:]
    tq = nb * CHUNK
    far_len = kw * nf
    vx_rows = HD_C + ONES_ROWS
    misc = misc_ref[...]
    kvn = [r[...] for r in kvn_ref]
    lane_seq = lax.broadcasted_iota(I32, (1, tq), 1) >> 6
    q, qi, misc_t = q_ref[...], qi_ref[...], misc.T

    def own_rows(x, b):
        return jnp.where((lax.broadcasted_iota(I32, x.shape, 0) >> 6) == b, x, 0.0)

    per_seq = [_scaled_queries(own_rows(q, b), own_rows(qi, b), misc_t) for b in range(nb)]
    wi_t = per_seq[0][1]
    qi_all = [jnp.concatenate(per_seq[b][0], axis=0) for b in range(nb)]
    q_g = [[jnp.concatenate(per_seq[b][2][g * REP_C:(g + 1) * REP_C], axis=0) for b in range(nb)]
           for g in range(N_KV_C)]
    seq_rows = [slice(b * CHUNK, (b + 1) * CHUNK) for b in range(nb)]

    def group_rows(refs, b, start, n, g):
        return refs[b][0, pl.ds(2 * start + g, n, stride=2), :]

    def side_by_side(per_seq):
        return [per_seq(b).astype(BF16) for b in range(nb)]

    def stacked_vx(per_seq):
        return [_with_ones(per_seq(b).astype(BF16)) for b in range(nb)]

    def far_ki(c):
        return [pki_ref[b, pl.ds(c * kw, kw), :].astype(BF16) for b in range(nb)]

    def far_k(c):
        return [side_by_side(lambda b: group_rows(pk_ref, b, c * kw, kw, g)) for g in range(N_KV_C)]

    def far_vx(c):
        return [stacked_vx(lambda b: group_rows(pv_ref, b, c * kw, kw, g).T) for g in range(N_KV_C)]

    near_ki = [jnp.concatenate([pki_ref[b, pl.ds(far_len, NEAR_SPAN), :], misc[seq_rows[b], :D_IDX]],
                               axis=0).astype(BF16) for b in range(nb)]
    near_k = [side_by_side(lambda b: jnp.concatenate(
        [group_rows(pk_ref, b, far_len, NEAR_SPAN, g), kvn[b][:, KV_SLICES[g]]], axis=0))
        for g in range(N_KV_C)]
    near_vx = [stacked_vx(lambda b: jnp.concatenate(
        [group_rows(pv_ref, b, far_len, NEAR_SPAN, g).T, kvn[b][:, KV_C + g * HD_C:KV_C + (g + 1) * HD_C].T],
        axis=1)) for g in range(N_KV_C)]

    def emit(g, acc):
        for r in range(REP_C):
            hh = g * REP_C + r
            cols = slice(r * tq, (r + 1) * tq)
            o = None
            for b in range(nb):
                top = b * vx_rows
                ob = acc[top:top + HD_C, cols] / acc[top + HD_C:top + HD_C + 1, cols]
                o = ob if o is None else jnp.where(lane_seq == b, ob, o)
            out_ref[:, hh * HD_C:(hh + 1) * HD_C] = o.T.astype(BF16)

    _dsa_core(qi_all, wi_t, q_g, far_ki, far_k, far_vx, nf, None, near_ki, near_k, near_vx, None,
              jnp.full((1, tq), float(far_len + NEAR_SPAN + CHUNK), F32),
              bias_near_ref, emit, scratch,
              tq=tq, kw=kw, nw=NEAR_SPAN + CHUNK, topk=topk)


def _dsa_sample(h, pk, pv, pki, bias_near, t, l):
    nb, p = pki.shape[1], pki.shape[2]
    tq = SAMPLE_PAIR * CHUNK
    far_len = p - NEAR_SPAN
    kw = _pick(far_len, (640, 512, 384, 256, 128))
    nf = far_len // kw
    nw = NEAR_SPAN + CHUNK
    topk = min(TOPK_MAX, (p + CHUNK) // 4)
    row0 = t // tq

    def cache_block(s, b):
        return (l, SAMPLE_PAIR * b + s, 0, 0)

    def new_rows(s, b):
        return (t // CHUNK + SAMPLE_PAIR * b + s, C_KC // (2 * KV_C))

    return pl.pallas_call(
        functools.partial(_dsa_sample_kernel, kw=kw, nf=nf, topk=topk),
        grid=(nb // SAMPLE_PAIR,),
        in_specs=[pl.BlockSpec((tq, WIDTH_C), lambda b: (row0 + b, C_QC // WIDTH_C)),
                  pl.BlockSpec((tq, QI_C), lambda b: (row0 + b, C_QIC // QI_C)),
                  pl.BlockSpec((tq, LANES), lambda b: (row0 + b, C_MISC // LANES))]
                 + [pl.BlockSpec((CHUNK, 2 * KV_C), functools.partial(new_rows, s)) for s in range(SAMPLE_PAIR)]
                 + 2 * [pl.BlockSpec((None, 1, N_KV_C * p, HD_C), functools.partial(cache_block, s))
                        for s in range(SAMPLE_PAIR)]
                 + [pl.BlockSpec((None, SAMPLE_PAIR, p, D_IDX), lambda b: (l, b, 0, 0)),
                    pl.BlockSpec(bias_near.shape, lambda b: (0, 0, 0))],
        out_specs=pl.BlockSpec((tq, WIDTH_C), lambda b: (b, 0)),
        out_shape=jax.ShapeDtypeStruct((nb * CHUNK, WIDTH_C), BF16),
        scratch_shapes=_dsa_scratch(nf, tq, kw, nw, SAMPLE_PAIR * (HD_C + ONES_ROWS)),
        compiler_params=_cparams(("arbitrary",)),
        name="dsa_sample",
    )(h, h, h, *(SAMPLE_PAIR * [h]), *(SAMPLE_PAIR * [pk]), *(SAMPLE_PAIR * [pv]), pki, bias_near)


def _outproj_kernel(x_ref, ya_ref, yb_ref, ycp_ref, ycs_ref, wa_ref, wb_ref, wc_ref, o_ref, *, n_prompt_blocks):
    yc = jnp.where(pl.program_id(0) < n_prompt_blocks, ycp_ref[...], ycs_ref[...])
    o_ref[...] = (x_ref[...] + _dot(ya_ref[...], wa_ref[...]) + _dot(yb_ref[...], wb_ref[...])
                  + _dot(yc, wc_ref[...]))


def _outproj(x, ya, yb, yc_p, yc_s, wa, wb, wc, l):
    m = x.shape[0]
    tm = _pick(yc_s.shape[0], (1024, 512, 256, 128))
    tn = 512
    npb = yc_p.shape[0] // tm
    assert yc_p.shape[0] % tm == 0
    return pl.pallas_call(
        functools.partial(_outproj_kernel, n_prompt_blocks=npb),
        grid=(m // tm, D_MODEL // tn),
        in_specs=[pl.BlockSpec((tm, tn), lambda i, j: (i, j)),
                  pl.BlockSpec((tm, WIDTH_A), lambda i, j: (i, 0)),
                  pl.BlockSpec((tm, WIDTH_B), lambda i, j: (i, 0)),
                  pl.BlockSpec((tm, WIDTH_C), lambda i, j: (jnp.minimum(i, npb - 1), 0)),
                  pl.BlockSpec((tm, WIDTH_C), lambda i, j: (jnp.maximum(i - npb, 0), 0)),
                  pl.BlockSpec((None, WIDTH_A, tn), lambda i, j: (l, 0, j)),
                  pl.BlockSpec((None, WIDTH_B, tn), lambda i, j: (l, 0, j)),
                  pl.BlockSpec((None, WIDTH_C, tn), lambda i, j: (l, 0, j))],
        out_specs=pl.BlockSpec((tm, tn), lambda i, j: (i, j)),
        out_shape=jax.ShapeDtypeStruct((m, D_MODEL), F32),
        compiler_params=_cparams(("arbitrary", "arbitrary")),
        name="outproj",
    )(x, ya, yb, yc_p, yc_s, wa, wb, wc)


def _ffn_kernel(x_ref, g_ref, wu_ref, wd_ref, o_ref, xn_ref, acc_ref):
    f = pl.program_id(1)

    @pl.when(f == 0)
    def _():
        xn_ref[...] = (_rms(x_ref[...]) * g_ref[...]).astype(BF16)
        acc_ref[...] = jnp.zeros(acc_ref.shape, F32)

    a = jnp.maximum(_dot(xn_ref[...], wu_ref[...]), 0.0)
    acc_ref[...] += _dot((a * a).astype(BF16), wd_ref[...])

    @pl.when(f == pl.num_programs(1) - 1)
    def _():
        o_ref[...] = x_ref[...] + acc_ref[...]


def _ffn(x, g, wu, wd, l):
    m = x.shape[0]
    tm = _pick(m, (512, 256, 128))
    tf = 1024
    return pl.pallas_call(
        _ffn_kernel,
        grid=(m // tm, D_FF // tf),
        in_specs=[pl.BlockSpec((tm, D_MODEL), lambda i, f: (i, 0)),
                  pl.BlockSpec((None, 1, D_MODEL), lambda i, f: (l, 0, 0)),
                  pl.BlockSpec((None, D_MODEL, tf), lambda i, f: (l, 0, f)),
                  pl.BlockSpec((None, tf, D_MODEL), lambda i, f: (l, f, 0))],
        out_specs=pl.BlockSpec((tm, D_MODEL), lambda i, f: (i, 0)),
        out_shape=jax.ShapeDtypeStruct((m, D_MODEL), F32),
        scratch_shapes=[pltpu.VMEM((tm, D_MODEL), BF16), pltpu.VMEM((tm, D_MODEL), F32)],
        compiler_params=_cparams(("arbitrary", "arbitrary")),
        name="ffn",
    )(x, g, wu, wd)


def _norm_kernel(x_ref, g_ref, o_ref):
    o_ref[...] = _rms(x_ref[...]) * g_ref[...]


def _final_norm(x, g, row0, rows):
    tm = _pick(rows, (1024, 512, 256, 128))
    assert row0 % tm == 0
    return pl.pallas_call(
        _norm_kernel,
        grid=(rows // tm,),
        in_specs=[pl.BlockSpec((tm, D_MODEL), lambda i: (row0 // tm + i, 0)),
                  pl.BlockSpec((1, D_MODEL), lambda i: (0, 0))],
        out_specs=pl.BlockSpec((tm, D_MODEL), lambda i: (i, 0)),
        out_shape=jax.ShapeDtypeStruct((rows, D_MODEL), F32),
        compiler_params=_cparams(("arbitrary",)),
        name="final_norm",
    )(x, g)


def _t5_bucket(rel):
    half = N_BUCKETS // 2
    max_exact = half // 2
    n = jnp.abs(rel)
    nf = jnp.maximum(n, 1).astype(F32)
    large = max_exact + (jnp.log(nf / max_exact) / jnp.log(MAX_DISTANCE / max_exact)
                         * (half - max_exact)).astype(I32)
    large = jnp.minimum(large, half - 1)
    return jnp.where(rel > 0, half, 0) + jnp.where(n < max_exact, n, large)


def _bias_table(rel_bias, tq, nw, copies=1):
    j = jnp.arange(nw, dtype=I32)[:, None]
    t = jnp.arange(tq, dtype=I32)[None, :]
    hot = jax.nn.one_hot(_t5_bucket(j - NEAR_SPAN - t), N_BUCKETS, dtype=F32)
    near = jnp.einsum("jtb,bh->jth", hot, rel_bias, precision=lax.Precision.HIGHEST)
    far = rel_bias[_t5_bucket(jnp.int32(-(NEAR_SPAN + 1)))]
    rep = N_HEADS_C // N_KV_C
    near = ((near - far) * LOG2E).astype(F32).reshape(nw, tq, N_KV_C, rep)
    near = jnp.transpose(near, (2, 0, 3, 1))[:, :, :, None, :]
    return jnp.broadcast_to(near, (N_KV_C, nw, rep, copies, tq)).reshape(N_KV_C, nw, rep * copies * tq)


def _permute_w_in(w):
    wt = jnp.swapaxes(w, 1, 2)
    sizes = (WIDTH_A, WIDTH_A, QK_B, QK_B, WIDTH_B, GATE_RANK_B, WIDTH_B, WIDTH_C, KV_C, KV_C, QI_C,
             D_IDX, N_IDX_HEADS)
    parts, off = [], 0
    for n in sizes:
        parts.append(wt[:, off:off + n])
        off += n
    ua, va, qb, kb, vb, lrb, og, qc, kc, vc, qic, kic, wic = parts
    pad = jnp.zeros((w.shape[0], H_WIDTH - (C_MISC + D_IDX + GATE_RANK_B + N_IDX_HEADS), w.shape[1]), w.dtype)
    return jnp.concatenate([qb, kb, vb, og, qc, kc, vc, qic, ua, va, kic, lrb, wic, pad], axis=1).astype(BF16)


def kernel(x_prompt, x_sample, cache_c_k, cache_c_v, cache_c_kidx, state_b_s, norm_mix, w_in, norm_a_v,
           w_s_a, b_s_a, w_gate_b, b_gate_b, norm_b_o, rel_bias, w_o, norm_ffn, w_up, w_down, norm_final):
    depth = w_in.shape[0]
    nbp, t, _ = x_prompt.shape
    nbs, ts, _ = x_sample.shape
    p = cache_c_k.shape[2]
    assert nbp == 1 and ts == CHUNK and t % FAR_KW == 0 and t % GMLP_ROWS == 0 and (nbs * ts) % GMLP_ROWS == 0
    assert p % NEAR_SPAN == 0 and p >= 2 * NEAR_SPAN
    assert t % PROMPT_TQ == 0 and t % (GLA_SUB * CHUNK) == 0 and nbs % GLA_SUB == 0

    x = jnp.concatenate([x_prompt[0], x_sample.reshape(nbs * ts, D_MODEL)], axis=0)
    w_in_p = _permute_w_in(w_in)
    wo_a = w_o[:, :WIDTH_A].astype(BF16)
    wo_b = w_o[:, WIDTH_A:WIDTH_A + WIDTH_B].astype(BF16)
    wo_c = w_o[:, WIDTH_A + WIDTH_B:].astype(BF16)
    w_up_b = w_up.astype(BF16)
    w_down_b = w_down.astype(BF16)
    w_gate = w_gate_b.astype(BF16)
    np_, ns_ = GMLP_ROWS // GROUP_A, GMLP_ROWS // CHUNK
    w_s2 = jnp.stack([jnp.tile(w_s_a, (1, 1, np_, np_)),
                      jnp.tile(w_s_a[:, :, :CHUNK, :CHUNK], (1, 1, ns_, ns_))], axis=1)
    b_s2 = jnp.stack([jnp.tile(b_s_a, (1, 1, np_)), jnp.tile(b_s_a[:, :, :CHUNK], (1, 1, ns_))], axis=1)
    b_s2 = jnp.swapaxes(b_s2, 2, 3)
    s_all = jnp.concatenate([jnp.zeros((depth, GLA_SUB) + state_b_s.shape[2:], F32), state_b_s], axis=1)
    bias_near = _bias_table(rel_bias, PROMPT_TQ, NEAR_SPAN + PROMPT_TQ)
    bias_near_s = _bias_table(rel_bias, CHUNK, NEAR_SPAN + CHUNK, copies=SAMPLE_PAIR)
    pk_all = cache_c_k.reshape(depth, nbs, N_KV_C * p, HD_C)
    pv_all = cache_c_v.reshape(depth, nbs, N_KV_C * p, HD_C)

    row = lambda a: a[:, None, :]
    outs = [[] for _ in range(9)]
    for l in range(depth):
        h = _inproj(x, row(norm_mix), w_in_p, l)
        hp, hs = h[:t], h[t:]
        ya, vn = _gmlp(h, row(norm_a_v), w_s2, b_s2, t, l)
        yb, s_out = _gla(h, w_gate, row(b_gate_b), row(norm_b_o), s_all, t, l)
        kb = hp[:, C_KC:C_KC + KV_C].astype(BF16)
        vt = hp[:, C_VC:C_VC + KV_C].astype(BF16).T
        vt_far = jnp.swapaxes(vt.reshape(KV_C, t // FAR_KW, FAR_KW), 0, 1)
        vt_near = jnp.swapaxes(vt.reshape(KV_C, t // NEAR_SPAN, NEAR_SPAN), 0, 1)
        kib = hp[:, C_MISC:C_MISC + D_IDX].astype(BF16)
        yc_p = _dsa_prompt(h, kb, vt_far, vt_near, kib, bias_near, t)
        yc_s = _dsa_sample(h, pk_all, pv_all, cache_c_kidx, bias_near_s, t, l)
        x = _outproj(x, ya, yb, yc_p, yc_s, wo_a, wo_b, wo_c, l)
        x = _ffn(x, row(norm_ffn), w_up_b, w_down_b, l)

        outs[0].append(hp[:, C_KC:C_KC + KV_C].reshape(1, t, N_KV_C, HD_C))
        outs[1].append(hp[:, C_VC:C_VC + KV_C].reshape(1, t, N_KV_C, HD_C))
        outs[2].append(hp[:, C_MISC:C_MISC + D_IDX].reshape(1, t, D_IDX))
        outs[3].append(s_out[GLA_SUB - 1:GLA_SUB])
        outs[4].append(hs[:, C_KC:C_KC + KV_C].reshape(nbs, ts, N_KV_C, HD_C))
        outs[5].append(hs[:, C_VC:C_VC + KV_C].reshape(nbs, ts, N_KV_C, HD_C))
        outs[6].append(hs[:, C_MISC:C_MISC + D_IDX].reshape(nbs, ts, D_IDX))
        outs[7].append(s_out[GLA_SUB:])
        outs[8].append(vn.reshape(nbs, ts, WIDTH_A))

    y_prompt = _final_norm(x, norm_final[None], 0, t)
    y_sample = _final_norm(x, norm_final[None], t, nbs * ts)
    return (y_prompt[None], y_sample.reshape(nbs, ts, D_MODEL)) + tuple(jnp.stack(o) for o in outs)
```

```python
import functools
import math

import jax
import jax.numpy as jnp
from jax import lax
from jax.experimental import pallas as pl
from jax.experimental.pallas import tpu as pltpu

BF16 = jnp.bfloat16
F32 = jnp.float32
I32 = jnp.int32

D_MODEL = 2048
EPS = 1e-6
CHUNK = 64
GROUP_A = 128
N_GROUPS_A = 4
WIDTH_A = 512
N_HEADS_B = 6
DK_B = 64
DV_B = 128
GATE_RANK_B = 16
GATE_TEMP_B = 16.0
WIDTH_B = 768
N_HEADS_C = 6
N_KV_C = 2
HD_C = 128
N_IDX_HEADS = 8
D_IDX = 64
TOPK_MAX = 256
WIDTH_C = 768
N_BUCKETS = 32
MAX_DISTANCE = 128
D_FF = 4 * D_MODEL
QK_B = N_HEADS_B * DK_B
KV_C = N_KV_C * HD_C
QI_C = N_IDX_HEADS * D_IDX

LANES = 128
SUBLANES = 8
ONES_ROWS = 16
VMEM_LIMIT = 56 * 1024 * 1024

C_QB = 0
C_KB = 384
C_VB = 768
C_OG = 1536
C_QC = 2304
C_KC = 3072
C_VC = 3328
C_QIC = 3584
C_UA = 4096
C_VA = 4608
C_MISC = 5120
MISC_LRB = D_IDX
MISC_WIC = D_IDX + GATE_RANK_B
H_WIDTH = 5376

INT_MIN = -2147483648
INT_MAX = 2147483647
SNAP_EVERY = 8
BISECT_AFTER = 40
MAX_SEARCH_STEPS = 80
MASKED = -1e30
LOG2E = 1.4426950408889634
NEAR_SPAN = 128
FAR_KW = 512
PROMPT_TQ = 256
SOFTMAX_ROWS = 64
GLA_SUB = 4
GMLP_ROWS = 256
SAMPLE_PAIR = 1


def _pick(n, cands):
    for c in cands:
        if n % c == 0:
            return c
    raise ValueError(f"no tile for {n}")


def _cparams(sem):
    return pltpu.CompilerParams(dimension_semantics=sem, vmem_limit_bytes=VMEM_LIMIT)


def _rms(x):
    return x * lax.rsqrt(jnp.mean(x * x, axis=-1, keepdims=True) + EPS)


def _dot(a, b):
    return jnp.dot(a, b, preferred_element_type=F32)


def _dot_nt(a, b):
    return lax.dot_general(a, b, (((1,), (1,)), ((), ())), preferred_element_type=F32)


def _dot_tn(a, b):
    return lax.dot_general(a, b, (((0,), (0,)), ((), ())), preferred_element_type=F32)


def _inproj_kernel(x_ref, g_ref, w_ref, o_ref, xn_ref):
    @pl.when(pl.program_id(1) == 0)
    def _():
        xn_ref[...] = (_rms(x_ref[...]) * g_ref[...]).astype(BF16)

    o_ref[...] = _dot_nt(xn_ref[...], w_ref[...])


def _inproj(x, g, w, l):
    m = x.shape[0]
    tm = _pick(m, (1024, 512, 256, 128))
    tn = 768
    return pl.pallas_call(
        _inproj_kernel,
        grid=(m // tm, H_WIDTH // tn),
        in_specs=[pl.BlockSpec((tm, D_MODEL), lambda i, j: (i, 0)),
                  pl.BlockSpec((None, 1, D_MODEL), lambda i, j: (l, 0, 0)),
                  pl.BlockSpec((None, tn, D_MODEL), lambda i, j: (l, j, 0))],
        out_specs=pl.BlockSpec((tm, tn), lambda i, j: (i, j)),
        out_shape=jax.ShapeDtypeStruct((m, H_WIDTH), F32),
        scratch_shapes=[pltpu.VMEM((tm, D_MODEL), BF16)],
        compiler_params=_cparams(("arbitrary", "arbitrary")),
        name="inproj",
    )(x, g, w)


def _gmlp_kernel(h_ref, gv_ref, w_ref, b_ref, ya_ref, vn_ref, *, n_prompt_blocks):
    i = pl.program_id(0)
    hv = h_ref[...]
    u = jax.nn.gelu(hv[:, :WIDTH_A])
    v = jax.nn.gelu(hv[:, WIDTH_A:])
    r = lax.broadcasted_iota(I32, (GMLP_ROWS, GMLP_ROWS), 0)
    c = lax.broadcasted_iota(I32, (GMLP_ROWS, GMLP_ROWS), 1)
    shift = 7 - (i >= n_prompt_blocks).astype(I32)
    keep = (c <= r) & (jnp.right_shift(r, shift) == jnp.right_shift(c, shift))
    for g in range(N_GROUPS_A):
        sl = slice(g * GROUP_A, (g + 1) * GROUP_A)
        vn = _rms(v[:, sl]) * gv_ref[:, sl]
        vn_ref[:, sl] = vn
        wm = jnp.where(keep, w_ref[0, g], 0.0).astype(BF16)
        z = _dot(wm, vn.astype(BF16)) + b_ref[0][:, g:g + 1]
        ya_ref[:, sl] = (u[:, sl] * z).astype(BF16)


def _gmlp(h, gv, w2, b2, n_prompt_rows, l):
    m = h.shape[0]
    rows = GMLP_ROWS
    npb = n_prompt_rows // rows
    nsb = (m - n_prompt_rows) // rows
    return pl.pallas_call(
        functools.partial(_gmlp_kernel, n_prompt_blocks=npb),
        grid=(npb + nsb,),
        in_specs=[pl.BlockSpec((rows, 2 * WIDTH_A), lambda i: (i, C_UA // (2 * WIDTH_A))),
                  pl.BlockSpec((None, 1, WIDTH_A), lambda i: (l, 0, 0)),
                  pl.BlockSpec((None, 1, N_GROUPS_A, rows, rows),
                               lambda i: (l, jnp.where(i >= npb, 1, 0), 0, 0, 0)),
                  pl.BlockSpec((None, 1, rows, N_GROUPS_A),
                               lambda i: (l, jnp.where(i >= npb, 1, 0), 0, 0))],
        out_specs=[pl.BlockSpec((rows, WIDTH_A), lambda i: (i, 0)),
                   pl.BlockSpec((rows, WIDTH_A), lambda i: (jnp.maximum(i - npb, 0), 0))],
        out_shape=[jax.ShapeDtypeStruct((m, WIDTH_A), BF16),
                   jax.ShapeDtypeStruct((nsb * rows, WIDTH_A), F32)],
        compiler_params=_cparams(("arbitrary",)),
        name="gmlp",
    )(h, gv, w2, b2)


def _gla_kernel(q_ref, k_ref, v_ref, og_ref, misc_ref, wg_ref, bg_ref, gbo_ref, s0_ref,
                yb_ref, sout_ref, st_ref, *, n_prompt_steps):
    j = pl.program_id(0)
    rows = GLA_SUB * CHUNK
    fresh = j >= n_prompt_steps

    lrb = misc_ref[:, MISC_LRB:MISC_LRB + GATE_RANK_B].astype(BF16)
    x = _dot(lrb, wg_ref[...]) + bg_ref[...]
    g = (jnp.minimum(x, 0.0) - jnp.log1p(jnp.exp(-jnp.abs(x)))) * (1.0 / GATE_TEMP_B)
    rr = lax.broadcasted_iota(I32, (rows, rows), 0)
    cc = lax.broadcasted_iota(I32, (rows, rows), 1)
    same = (rr >> 6) == (cc >> 6)
    tril = same & (cc <= rr)
    g1 = g.astype(BF16)
    r1 = g - g1.astype(F32)
    g2 = r1.astype(BF16)
    g3 = (r1 - g2.astype(F32)).astype(BF16)

    tri = jnp.where(tril, 1.0, 0.0).astype(BF16)
    b = _dot(tri, g1) + _dot(tri, g2) + _dot(tri, g3)

    def chunk_row(r):
        return jnp.concatenate([jnp.broadcast_to(b[c * CHUNK + r:c * CHUNK + r + 1], (CHUNK, QK_B))
                                for c in range(GLA_SUB)], axis=0)

    b_last = chunk_row(CHUNK - 1)
    b_mid = chunk_row(CHUNK // 2 - 1)
    qs = q_ref[...] * (DK_B ** -0.5)
    kk = k_ref[...]
    q_inter = (qs * jnp.exp(b)).astype(BF16)
    q_intra = (qs * jnp.exp(b - b_mid)).astype(BF16)
    k_intra = (kk * jnp.exp(b_mid - b)).astype(BF16)
    k_state = (kk * jnp.exp(b_last - b)).astype(BF16)
    dec = jnp.exp(b_last)
    for h in range(N_HEADS_B):
        sk = slice(h * DK_B, (h + 1) * DK_B)
        sv = slice(h * DV_B, (h + 1) * DV_B)
        vh = v_ref[:, sv].astype(BF16)
        att = jnp.where(tril, _dot_nt(q_intra[:, sk], k_intra[:, sk]), 0.0)
        o_intra = _dot(att.astype(BF16), vh)
        st = st_ref[h]
        o_inter = []
        for c in range(GLA_SUB):
            rc = slice(c * CHUNK, (c + 1) * CHUNK)
            start = fresh | ((j == 0) & (c == 0))
            st = jnp.where(start, s0_ref[c, h].T, st)
            o_inter.append(_dot_nt(q_inter[rc, sk], st.astype(BF16)))
            st = st * dec[c * CHUNK:c * CHUNK + 1, sk] + _dot_tn(vh[rc], k_state[rc, sk])
            sout_ref[c, h] = st.T
        st_ref[h] = st
        o = jnp.concatenate(o_inter, axis=0) + o_intra
        on = _rms(o) * gbo_ref[:, sv]
        yb_ref[:, sv] = (on * jax.nn.silu(og_ref[:, sv])).astype(BF16)


def _gla(h, wg, bg, gbo, s_in, n_prompt_rows, l):
    m = h.shape[0]
    rows = GLA_SUB * CHUNK
    nps = n_prompt_rows // rows
    nslots = s_in.shape[1]

    def blk(j):
        return jnp.where(j < nps, 0, j - nps + 1)

    return pl.pallas_call(
        functools.partial(_gla_kernel, n_prompt_steps=nps),
        grid=(m // rows,),
        in_specs=[pl.BlockSpec((rows, QK_B), lambda j: (j, C_QB // QK_B)),
                  pl.BlockSpec((rows, QK_B), lambda j: (j, C_KB // QK_B)),
                  pl.BlockSpec((rows, WIDTH_B), lambda j: (j, C_VB // WIDTH_B)),
                  pl.BlockSpec((rows, WIDTH_B), lambda j: (j, C_OG // WIDTH_B)),
                  pl.BlockSpec((rows, LANES), lambda j: (j, C_MISC // LANES)),
                  pl.BlockSpec((None, GATE_RANK_B, QK_B), lambda j: (l, 0, 0)),
                  pl.BlockSpec((None, 1, QK_B), lambda j: (l, 0, 0)),
                  pl.BlockSpec((None, 1, WIDTH_B), lambda j: (l, 0, 0)),
                  pl.BlockSpec((None, GLA_SUB, N_HEADS_B, DK_B, DV_B), lambda j: (l, blk(j), 0, 0, 0))],
        out_specs=[pl.BlockSpec((rows, WIDTH_B), lambda j: (j, 0)),
                   pl.BlockSpec((GLA_SUB, N_HEADS_B, DK_B, DV_B), lambda j: (blk(j), 0, 0, 0))],
        out_shape=[jax.ShapeDtypeStruct((m, WIDTH_B), BF16),
                   jax.ShapeDtypeStruct((nslots, N_HEADS_B, DK_B, DV_B), F32)],
        scratch_shapes=[pltpu.VMEM((N_HEADS_B, DV_B, DK_B), F32)],
        compiler_params=_cparams(("arbitrary",)),
        name="gla",
    )(h, h, h, h, h, wg, bg, gbo, s_in)


def _fold_rows(x, op):
    tile = SUBLANES * (4 // x.dtype.itemsize)
    parts = [x[j * tile:(j + 1) * tile] for j in range(x.shape[0] // tile)]
    while len(parts) > 1:
        nxt = [op(parts[j], parts[j + 1]) for j in range(0, len(parts) - 1, 2)]
        parts = nxt + parts[len(parts) - len(parts) % 2:]
    return parts[0]


def _loop(n, body, init):
    if isinstance(n, int):
        carry = init
        for c in range(n):
            carry = body(c, carry)
        return carry
    return lax.fori_loop(0, n, body, init)


REP_C = N_HEADS_C // N_KV_C
KV_SLICES = [slice(g * HD_C, (g + 1) * HD_C) for g in range(N_KV_C)]


def _scaled_queries(q, qi, misc_t):
    qi_s = (qi * (D_IDX ** -0.5)).astype(BF16)
    wi_t = misc_t[MISC_WIC:MISC_WIC + N_IDX_HEADS, :] * (N_IDX_HEADS ** -0.5)
    qs = (q * (HD_C ** -0.5 * LOG2E)).astype(BF16)
    return ([qi_s[:, h * D_IDX:(h + 1) * D_IDX] for h in range(N_IDX_HEADS)], wi_t,
            [qs[:, hh * HD_C:(hh + 1) * HD_C] for hh in range(N_HEADS_C)])


def _dsa_core(qi_all, wi_t, q_g, far_ki, far_k, far_vx, nf, far_adm, near_ki, near_k, near_vx, near_adm, n_adm,
              bias_near_ref, emit, scratch, *, tq, kw, nw, topk):
    fkeys_ref, nkeys_ref, m_ref, acc_ref, ss_ref, p_ref = scratch
    rep = REP_C
    groups = range(N_KV_C)
    kf = float(topk)

    def keys_of(ki_blk, adm):
        d = _dot_nt(ki_blk[0], qi_all[0])
        for kb, qb in zip(ki_blk[1:], qi_all[1:]):
            d = d + _dot_nt(kb, qb)
        acc = None
        for h in range(N_IDX_HEADS):
            t = wi_t[h:h + 1, :] * jnp.maximum(d[:, h * tq:(h + 1) * tq], 0.0)
            acc = t if acc is None else acc + t
        bits = lax.bitcast_convert_type(acc, I32)
        key = bits ^ ((bits >> 31) & 0x7FFFFFFF)
        key = jnp.where(acc == 0.0, 0, key)
        if adm is not None:
            key = jnp.where(adm, key, INT_MIN)
        return key

    def extremes(key, carry):
        kmax, kmin = carry
        kmax = jnp.maximum(kmax, _fold_rows(key, jnp.maximum))
        kmin = jnp.minimum(kmin, _fold_rows(jnp.where(key == INT_MIN, INT_MAX, key), jnp.minimum))
        return kmax, kmin

    def fill(c, carry):
        key = keys_of(far_ki(c), None if far_adm is None else far_adm(c))
        fkeys_ref[c] = key
        return extremes(key, carry)

    carry = _loop(nf, fill, (jnp.full((SUBLANES, tq), INT_MIN, I32), jnp.full((SUBLANES, tq), INT_MAX, I32)))
    key = keys_of(near_ki, near_adm)
    nkeys_ref[...] = key
    kmax, kmin = extremes(key, carry)
    kmax = jnp.max(kmax, axis=0, keepdims=True)
    kmin = jnp.min(kmin, axis=0, keepdims=True)

    def count(thr, strict):
        def cmp(x):
            hit = (x > thr) if strict else (x >= thr)
            return _fold_rows(jnp.where(hit, 1.0, 0.0), jnp.add)

        part = _loop(nf, lambda c, a: a + cmp(fkeys_ref[c]), jnp.zeros((SUBLANES, tq), F32))
        part = part + cmp(nkeys_ref[...])
        return jnp.sum(part, axis=0, keepdims=True)

    def bits_flip(x):
        return x ^ ((x >> 31) & 0x7FFFFFFF)

    def any_lane(flag):
        return jnp.max(jnp.where(flag, 1.0, 0.0))

    def snap(lo, hi):
        def pull(x, carry):
            kin, kax = carry
            kin = jnp.minimum(kin, _fold_rows(jnp.where(x >= lo, x, INT_MAX), jnp.minimum))
            kax = jnp.maximum(kax, _fold_rows(jnp.where(x < hi, x, INT_MIN), jnp.maximum))
            return kin, kax

        init = (jnp.full((SUBLANES, tq), INT_MAX, I32), jnp.full((SUBLANES, tq), INT_MIN, I32))
        kin, kax = pull(nkeys_ref[...], _loop(nf, lambda c, carry: pull(fkeys_ref[c], carry), init))
        return jnp.min(kin, axis=0, keepdims=True), jnp.max(kax, axis=0, keepdims=True)

    enough = n_adm >= kf
    log_k = math.log2(topk)

    def open_lanes(lo, hi, flo):
        return enough & (flo > kf) & (hi > lo + 1)

    def log_excess(c):
        return jnp.log2(jnp.maximum(c, 0.5)) - log_k

    def step(state):
        it, _, lo, hi, flo, fhi, glo, ghi, side = state
        active = open_lanes(lo, hi, flo)

        def pull_in(_):
            kin, kax = snap(lo, hi)
            return jnp.where(active, kin, lo), jnp.where(active, kax + 1, hi), flo, fhi, glo, ghi, side

        def probe(_):
            v_lo = lax.bitcast_convert_type(bits_flip(lo), F32)
            v_hi = lax.bitcast_convert_type(bits_flip(hi), F32)
            frac = jnp.minimum(jnp.maximum(glo / (glo - ghi), 0.02), 0.98)
            t_int = bits_flip(lax.bitcast_convert_type(v_lo + (v_hi - v_lo) * frac, I32))
            t_mid = (lo & hi) + ((lo ^ hi) >> 1)
            t = jnp.where(it >= BISECT_AFTER, t_mid, t_int)
            t = jnp.minimum(jnp.maximum(t, lo + 1), hi - 1)
            c = count(t, False)
            g = log_excess(c)
            up = active & (c >= kf)
            dn = active & (c < kf)
            ghi2 = jnp.where(up & (side > 0.0), ghi * 0.5, ghi)
            glo2 = jnp.where(dn & (side < 0.0), glo * 0.5, glo)
            return (jnp.where(up, t, lo), jnp.where(dn, t, hi), jnp.where(up, c, flo), jnp.where(dn, c, fhi),
                    jnp.where(up, g, glo2), jnp.where(dn, g, ghi2), jnp.where(up, 1.0, jnp.where(dn, -1.0, side)))

        is_snap = (it % SNAP_EVERY == SNAP_EVERY - 1) & (it < BISECT_AFTER)
        lo, hi, flo, fhi, glo, ghi, side = lax.cond(is_snap, pull_in, probe, 0)
        return it + 1, any_lane(open_lanes(lo, hi, flo)), lo, hi, flo, fhi, glo, ghi, side

    hi0 = kmax + 1
    zero = jnp.zeros((1, tq), F32)
    state = (jnp.int32(0), any_lane(open_lanes(kmin, hi0, n_adm)), kmin, hi0, n_adm, zero,
             log_excess(n_adm), log_excess(zero), zero)
    state = lax.while_loop(lambda st: (st[1] > 0.0) & (st[0] < MAX_SEARCH_STEPS), step, state)
    thr = jnp.where(enough, state[2], INT_MIN)
    has_tie = any_lane(enough & (state[4] > kf)) > 0.0

    @pl.when(has_tie)
    def _():
        need = kf - count(thr, True)

        def lower(n):
            a = lax.broadcasted_iota(I32, (n, n), 0)
            b = lax.broadcasted_iota(I32, (n, n), 1)
            return jnp.where(b < a, 1.0, 0.0).astype(BF16)

        def demote(keys, run, lt):
            eq = (keys == thr) & (keys > INT_MIN)
            eqf = jnp.where(eq, 1.0, 0.0)
            before = _dot(lt, eqf.astype(BF16)) + run
            keys = jnp.where(eq & (before >= need), INT_MIN, keys)
            return keys, run + jnp.sum(eqf, axis=0, keepdims=True)

        lt_far = lower(kw)

        def step(c, run):
            keys, run = demote(fkeys_ref[c], run, lt_far)
            fkeys_ref[c] = keys
            return run

        run = _loop(nf, step, jnp.zeros((1, tq), F32))
        keys, _ = demote(nkeys_ref[...], run, lower(nw))
        nkeys_ref[...] = keys

    thr_sel = jnp.maximum(thr, INT_MIN + 1)
    m_ref[...] = jnp.full(m_ref.shape, MASKED, F32)
    acc_ref[...] = jnp.zeros(acc_ref.shape, F32)

    def logits(k_blk):
        out = []
        for g in groups:
            s = _dot_nt(k_blk[g][0], q_g[g][0])
            for kb, qb in zip(k_blk[g][1:], q_g[g][1:]):
                s = s + _dot_nt(kb, qb)
            out.append(s)
        return out

    def attend(slot, keys_rows, vx_blk, bias_ref, n):
        tiles = [slice(r0, r0 + SOFTMAX_ROWS) for r0 in range(0, n, SOFTMAX_ROWS)]
        heads = [slice(r * tq, (r + 1) * tq) for r in range(rep)]

        def masked_logits(g, rows, selb):
            s = ss_ref[slot, g, rows, :]
            if bias_ref is not None:
                s = s + bias_ref[g, rows, :]
            return [s[:, hd] + selb for hd in heads]

        run = [None] * N_KV_C
        for rows in tiles:
            selb = jnp.where(keys_rows(rows) >= thr_sel, 0.0, MASKED)
            for g in groups:
                top = jnp.concatenate([_fold_rows(s, jnp.maximum) for s in masked_logits(g, rows, selb)], axis=1)
                run[g] = top if run[g] is None else jnp.maximum(run[g], top)
        m_new = []
        for g in groups:
            m_old = m_ref[g]
            m_new.append(jnp.maximum(m_old, jnp.max(run[g], axis=0, keepdims=True)))
            acc_ref[g] = jnp.exp2(m_old - m_new[g]) * acc_ref[g]
            m_ref[g] = m_new[g]
        for rows in tiles:
            selb = jnp.where(keys_rows(rows) >= thr_sel, 0.0, MASKED)
            for g in groups:
                for hd, s in zip(heads, masked_logits(g, rows, selb)):
                    p_ref[g, rows, hd] = jnp.exp2(s - m_new[g][:, hd]).astype(BF16)
        for g in groups:
            p = p_ref[g, :n, :]
            pv = [_dot(vx, p) for vx in vx_blk[g]]
            acc_ref[g] += pv[0] if len(pv) == 1 else jnp.concatenate(pv, axis=0)

    last = fkeys_ref.shape[0] - 1

    def put_logits(slot, k_blk, n):
        for g, s in enumerate(logits(k_blk)):
            ss_ref[slot, g, :n, :] = s

    def far_body(c, cur):
        put_logits(1 - cur, far_k(min(c + 1, last) if isinstance(c, int) else jnp.minimum(c + 1, last)), kw)
        attend(cur, lambda rows: fkeys_ref[c, rows, :], far_vx(c), None, kw)

    def far_step(c, carry):
        if isinstance(c, int):
            far_body(c, c % 2)
        else:
            for cur in range(2):
                pl.when(c % 2 == cur)(functools.partial(far_body, c, cur))
        return carry

    put_logits(0, far_k(0), kw)
    _loop(nf, far_step, 0)
    put_logits(0, near_k, nw)
    attend(0, lambda rows: nkeys_ref[rows, :], near_vx, bias_near_ref, nw)

    for g in groups:
        emit(g, acc_ref[g])


def _with_ones(vt):
    return jnp.concatenate([vt, jnp.ones((ONES_ROWS, vt.shape[1]), BF16)], axis=0)


def _dsa_scratch(nf, tq, kw, nw, vx_rows):
    rep = REP_C
    return [pltpu.VMEM((nf, kw, tq), I32),
            pltpu.VMEM((nw, tq), I32),
            pltpu.VMEM((N_KV_C, 1, rep * tq), F32),
            pltpu.VMEM((N_KV_C, vx_rows, rep * tq), F32),
            pltpu.VMEM((2, N_KV_C, max(kw, nw), rep * tq), F32),
            pltpu.VMEM((N_KV_C, max(kw, nw), rep * tq), BF16)]


def _dsa_prompt_kernel(q_ref, qi_ref, misc_ref, k_ref, vt_far_ref, vt_near_ref, ki_ref, bias_near_ref,
                       out_ref, *scratch, tq, topk):
    i = pl.program_id(0)
    kw = FAR_KW
    nw = NEAR_SPAN + tq
    q0 = i * tq
    far_limit = q0 - NEAR_SPAN
    near0 = jnp.maximum(far_limit, 0)
    nf = (near0 + kw - 1) // kw

    def rows(c):
        return pl.ds(pl.multiple_of(c * kw, kw), kw)

    def far_adm(c):
        return (c * kw + lax.broadcasted_iota(I32, (kw, tq), 0)) < far_limit

    left = pl.ds(pl.multiple_of(near0, NEAR_SPAN), NEAR_SPAN)
    right = pl.ds(pl.multiple_of(q0, tq), tq)
    near_ki = jnp.concatenate([ki_ref[left, :], ki_ref[right, :]], axis=0)
    near_k = jnp.concatenate([k_ref[left, :], k_ref[right, :]], axis=0)
    per = tq // NEAR_SPAN
    near_vt = jnp.concatenate([vt_near_ref[near0 // NEAR_SPAN]] + [vt_near_ref[i * per + j] for j in range(per)],
                              axis=1)
    key = lax.broadcasted_iota(I32, (nw, tq), 0)
    qry = lax.broadcasted_iota(I32, (nw, tq), 1)
    first_key = jnp.where(i > 0, 0, NEAR_SPAN)
    near_adm = (key >= first_key) & (((key - NEAR_SPAN) >> 6) <= (qry >> 6))
    n_adm = ((((q0 + lax.broadcasted_iota(I32, (1, tq), 1)) >> 6) + 1) * CHUNK).astype(F32)

    qi_h, wi_t, q_h = _scaled_queries(q_ref[...], qi_ref[...], misc_ref[...].T)
    qi_all = jnp.concatenate(qi_h, axis=0)
    q_g = [[jnp.concatenate(q_h[g * REP_C:(g + 1) * REP_C], axis=0)] for g in range(N_KV_C)]

    def emit(g, acc):
        o = acc[:HD_C] / acc[HD_C:HD_C + 1]
        for r in range(REP_C):
            hh = g * REP_C + r
            out_ref[:, hh * HD_C:(hh + 1) * HD_C] = o[:, r * tq:(r + 1) * tq].T.astype(BF16)

    _dsa_core([qi_all], wi_t, q_g,
              lambda c: [ki_ref[rows(c), :]], lambda c: [[k_ref[rows(c), sl]] for sl in KV_SLICES],
              lambda c: [[_with_ones(vt_far_ref[c, sl, :])] for sl in KV_SLICES],
              nf, far_adm, [near_ki], [[near_k[:, sl]] for sl in KV_SLICES],
              [[_with_ones(near_vt[sl, :])] for sl in KV_SLICES], near_adm, n_adm,
              bias_near_ref, emit, scratch,
              tq=tq, kw=kw, nw=nw, topk=topk)


def _dsa_prompt(h, kb, vt_far, vt_near, kib, bias_near, t):
    tq = PROMPT_TQ
    topk = min(TOPK_MAX, t // 4)
    return pl.pallas_call(
        functools.partial(_dsa_prompt_kernel, tq=tq, topk=topk),
        grid=(t // tq,),
        in_specs=[pl.BlockSpec((tq, WIDTH_C), lambda i: (i, C_QC // WIDTH_C)),
                  pl.BlockSpec((tq, QI_C), lambda i: (i, C_QIC // QI_C)),
                  pl.BlockSpec((tq, LANES), lambda i: (i, C_MISC // LANES)),
                  pl.BlockSpec((t, KV_C), lambda i: (0, 0)),
                  pl.BlockSpec(vt_far.shape, lambda i: (0, 0, 0)),
                  pl.BlockSpec(vt_near.shape, lambda i: (0, 0, 0)),
                  pl.BlockSpec((t, D_IDX), lambda i: (0, 0)),
                  pl.BlockSpec(bias_near.shape, lambda i: (0, 0, 0))],
        out_specs=pl.BlockSpec((tq, WIDTH_C), lambda i: (i, 0)),
        out_shape=jax.ShapeDtypeStruct((t, WIDTH_C), BF16),
        scratch_shapes=_dsa_scratch(t // FAR_KW, tq, FAR_KW, NEAR_SPAN + tq, HD_C + ONES_ROWS),
        compiler_params=_cparams(("arbitrary",)),
        name="dsa_prompt",
    )(h, h, h, kb, vt_far, vt_near, kib, bias_near)


def _dsa_sample_kernel(q_ref, qi_ref, misc_ref, *refs, kw, nf, topk):
    nb = SAMPLE_PAIR
    kvn_ref, pk_ref, pv_ref = refs[:nb], refs[nb:2 * nb], refs[2 * nb:3 * nb]
    pki_ref, bias_near_ref, out_ref = refs[3 * nb:3 * nb + 3]
    scratch = refs[3 * nb + 3:]
    tq = nb * CHUNK
    far_len = kw * nf
    vx_rows = HD_C + ONES_ROWS
    misc = misc_ref[...]
    kvn = [r[...] for r in kvn_ref]
    lane_seq = lax.broadcasted_iota(I32, (1, tq), 1) >> 6
    q, qi, misc_t = q_ref[...], qi_ref[...], misc.T

    def own_rows(x, b):
        return jnp.where((lax.broadcasted_iota(I32, x.shape, 0) >> 6) == b, x, 0.0)

    per_seq = [_scaled_queries(own_rows(q, b), own_rows(qi, b), misc_t) for b in range(nb)]
    wi_t = per_seq[0][1]
    qi_all = [jnp.concatenate(per_seq[b][0], axis=0) for b in range(nb)]
    q_g = [[jnp.concatenate(per_seq[b][2][g * REP_C:(g + 1) * REP_C], axis=0) for b in range(nb)]
           for g in range(N_KV_C)]
    seq_rows = [slice(b * CHUNK, (b + 1) * CHUNK) for b in range(nb)]

    def group_rows(refs, b, start, n, g):
        return refs[b][0, pl.ds(2 * start + g, n, stride=2), :]

    def side_by_side(per_seq):
        return [per_seq(b).astype(BF16) for b in range(nb)]

    def stacked_vx(per_seq):
        return [_with_ones(per_seq(b).astype(BF16)) for b in range(nb)]

    def far_ki(c):
        return [pki_ref[b, :, pl.ds(c * kw, kw)].T.astype(BF16) for b in range(nb)]

    def far_k(c):
        return [side_by_side(lambda b: group_rows(pk_ref, b, c * kw, kw, g)) for g in range(N_KV_C)]

    def far_vx(c):
        return [stacked_vx(lambda b: group_rows(pv_ref, b, c * kw, kw, g).T) for g in range(N_KV_C)]

    near_ki = [jnp.concatenate([pki_ref[b, :, pl.ds(far_len, NEAR_SPAN)].T, misc[seq_rows[b], :D_IDX]],
                               axis=0).astype(BF16) for b in range(nb)]
    near_k = [side_by_side(lambda b: jnp.concatenate(
        [group_rows(pk_ref, b, far_len, NEAR_SPAN, g), kvn[b][:, KV_SLICES[g]]], axis=0))
        for g in range(N_KV_C)]
    near_vx = [stacked_vx(lambda b: jnp.concatenate(
        [group_rows(pv_ref, b, far_len, NEAR_SPAN, g).T, kvn[b][:, KV_C + g * HD_C:KV_C + (g + 1) * HD_C].T],
        axis=1)) for g in range(N_KV_C)]

    def emit(g, acc):
        for r in range(REP_C):
            hh = g * REP_C + r
            cols = slice(r * tq, (r + 1) * tq)
            o = None
            for b in range(nb):
                top = b * vx_rows
                ob = acc[top:top + HD_C, cols] / acc[top + HD_C:top + HD_C + 1, cols]
                o = ob if o is None else jnp.where(lane_seq == b, ob, o)
            out_ref[:, hh * HD_C:(hh + 1) * HD_C] = o.T.astype(BF16)

    _dsa_core(qi_all, wi_t, q_g, far_ki, far_k, far_vx, nf, None, near_ki, near_k, near_vx, None,
              jnp.full((1, tq), float(far_len + NEAR_SPAN + CHUNK), F32),
              bias_near_ref, emit, scratch,
              tq=tq, kw=kw, nw=NEAR_SPAN + CHUNK, topk=topk)


def _dsa_sample(h, pk, pv, pki, bias_near, t, l):
    nb, p = pki.shape[1], pki.shape[3]
    tq = SAMPLE_PAIR * CHUNK
    far_len = p - NEAR_SPAN
    kw = _pick(far_len, (640, 512, 384, 256, 128))
    nf = far_len // kw
    nw = NEAR_SPAN + CHUNK
    topk = min(TOPK_MAX, (p + CHUNK) // 4)
    row0 = t // tq

    def cache_block(s, b):
        return (l, SAMPLE_PAIR * b + s, 0, 0)

    def new_rows(s, b):
        return (t // CHUNK + SAMPLE_PAIR * b + s, C_KC // (2 * KV_C))

    return pl.pallas_call(
        functools.partial(_dsa_sample_kernel, kw=kw, nf=nf, topk=topk),
        grid=(nb // SAMPLE_PAIR,),
        in_specs=[pl.BlockSpec((tq, WIDTH_C), lambda b: (row0 + b, C_QC // WIDTH_C)),
                  pl.BlockSpec((tq, QI_C), lambda b: (row0 + b, C_QIC // QI_C)),
                  pl.BlockSpec((tq, LANES), lambda b: (row0 + b, C_MISC // LANES))]
                 + [pl.BlockSpec((CHUNK, 2 * KV_C), functools.partial(new_rows, s)) for s in range(SAMPLE_PAIR)]
                 + 2 * [pl.BlockSpec((None, 1, N_KV_C * p, HD_C), functools.partial(cache_block, s))
                        for s in range(SAMPLE_PAIR)]
                 + [pl.BlockSpec((None, SAMPLE_PAIR, D_IDX, p), lambda b: (l, b, 0, 0)),
                    pl.BlockSpec(bias_near.shape, lambda b: (0, 0, 0))],
        out_specs=pl.BlockSpec((tq, WIDTH_C), lambda b: (b, 0)),
        out_shape=jax.ShapeDtypeStruct((nb * CHUNK, WIDTH_C), BF16),
        scratch_shapes=_dsa_scratch(nf, tq, kw, nw, SAMPLE_PAIR * (HD_C + ONES_ROWS)),
        compiler_params=_cparams(("arbitrary",)),
        name="dsa_sample",
    )(h, h, h, *(SAMPLE_PAIR * [h]), *(SAMPLE_PAIR * [pk]), *(SAMPLE_PAIR * [pv]), pki, bias_near)


def _outproj_kernel(x_ref, ya_ref, yb_ref, ycp_ref, ycs_ref, wa_ref, wb_ref, wc_ref, o_ref, *, n_prompt_blocks):
    yc = jnp.where(pl.program_id(0) < n_prompt_blocks, ycp_ref[...], ycs_ref[...])
    o_ref[...] = (x_ref[...] + _dot(ya_ref[...], wa_ref[...]) + _dot(yb_ref[...], wb_ref[...])
                  + _dot(yc, wc_ref[...]))


def _outproj(x, ya, yb, yc_p, yc_s, wa, wb, wc, l):
    m = x.shape[0]
    tm = _pick(yc_s.shape[0], (1024, 512, 256, 128))
    tn = 512
    npb = yc_p.shape[0] // tm
    assert yc_p.shape[0] % tm == 0
    return pl.pallas_call(
        functools.partial(_outproj_kernel, n_prompt_blocks=npb),
        grid=(m // tm, D_MODEL // tn),
        in_specs=[pl.BlockSpec((tm, tn), lambda i, j: (i, j)),
                  pl.BlockSpec((tm, WIDTH_A), lambda i, j: (i, 0)),
                  pl.BlockSpec((tm, WIDTH_B), lambda i, j: (i, 0)),
                  pl.BlockSpec((tm, WIDTH_C), lambda i, j: (jnp.minimum(i, npb - 1), 0)),
                  pl.BlockSpec((tm, WIDTH_C), lambda i, j: (jnp.maximum(i - npb, 0), 0)),
                  pl.BlockSpec((None, WIDTH_A, tn), lambda i, j: (l, 0, j)),
                  pl.BlockSpec((None, WIDTH_B, tn), lambda i, j: (l, 0, j)),
                  pl.BlockSpec((None, WIDTH_C, tn), lambda i, j: (l, 0, j))],
        out_specs=pl.BlockSpec((tm, tn), lambda i, j: (i, j)),
        out_shape=jax.ShapeDtypeStruct((m, D_MODEL), F32),
        compiler_params=_cparams(("arbitrary", "arbitrary")),
        name="outproj",
    )(x, ya, yb, yc_p, yc_s, wa, wb, wc)


def _ffn_kernel(x_ref, g_ref, wu_ref, wd_ref, o_ref, xn_ref, acc_ref):
    f = pl.program_id(1)

    @pl.when(f == 0)
    def _():
        xn_ref[...] = (_rms(x_ref[...]) * g_ref[...]).astype(BF16)
        acc_ref[...] = jnp.zeros(acc_ref.shape, F32)

    a = jnp.maximum(_dot(xn_ref[...], wu_ref[...]), 0.0)
    acc_ref[...] += _dot((a * a).astype(BF16), wd_ref[...])

    @pl.when(f == pl.num_programs(1) - 1)
    def _():
        o_ref[...] = x_ref[...] + acc_ref[...]


def _ffn(x, g, wu, wd, l):
    m = x.shape[0]
    tm = _pick(m, (512, 256, 128))
    tf = 1024
    return pl.pallas_call(
        _ffn_kernel,
        grid=(m // tm, D_FF // tf),
        in_specs=[pl.BlockSpec((tm, D_MODEL), lambda i, f: (i, 0)),
                  pl.BlockSpec((None, 1, D_MODEL), lambda i, f: (l, 0, 0)),
                  pl.BlockSpec((None, D_MODEL, tf), lambda i, f: (l, 0, f)),
                  pl.BlockSpec((None, tf, D_MODEL), lambda i, f: (l, f, 0))],
        out_specs=pl.BlockSpec((tm, D_MODEL), lambda i, f: (i, 0)),
        out_shape=jax.ShapeDtypeStruct((m, D_MODEL), F32),
        scratch_shapes=[pltpu.VMEM((tm, D_MODEL), BF16), pltpu.VMEM((tm, D_MODEL), F32)],
        compiler_params=_cparams(("arbitrary", "arbitrary")),
        name="ffn",
    )(x, g, wu, wd)


def _norm_kernel(x_ref, g_ref, o_ref):
    o_ref[...] = _rms(x_ref[...]) * g_ref[...]


def _final_norm(x, g, row0, rows):
    tm = _pick(rows, (1024, 512, 256, 128))
    assert row0 % tm == 0
    return pl.pallas_call(
        _norm_kernel,
        grid=(rows // tm,),
        in_specs=[pl.BlockSpec((tm, D_MODEL), lambda i: (row0 // tm + i, 0)),
                  pl.BlockSpec((1, D_MODEL), lambda i: (0, 0))],
        out_specs=pl.BlockSpec((tm, D_MODEL), lambda i: (i, 0)),
        out_shape=jax.ShapeDtypeStruct((rows, D_MODEL), F32),
        compiler_params=_cparams(("arbitrary",)),
        name="final_norm",
    )(x, g)


def _t5_bucket(rel):
    half = N_BUCKETS // 2
    max_exact = half // 2
    n = jnp.abs(rel)
    nf = jnp.maximum(n, 1).astype(F32)
    large = max_exact + (jnp.log(nf / max_exact) / jnp.log(MAX_DISTANCE / max_exact)
                         * (half - max_exact)).astype(I32)
    large = jnp.minimum(large, half - 1)
    return jnp.where(rel > 0, half, 0) + jnp.where(n < max_exact, n, large)


def _bias_table(rel_bias, tq, nw, copies=1):
    j = jnp.arange(nw, dtype=I32)[:, None]
    t = jnp.arange(tq, dtype=I32)[None, :]
    hot = jax.nn.one_hot(_t5_bucket(j - NEAR_SPAN - t), N_BUCKETS, dtype=F32)
    near = jnp.einsum("jtb,bh->jth", hot, rel_bias, precision=lax.Precision.HIGHEST)
    far = rel_bias[_t5_bucket(jnp.int32(-(NEAR_SPAN + 1)))]
    rep = N_HEADS_C // N_KV_C
    near = ((near - far) * LOG2E).astype(F32).reshape(nw, tq, N_KV_C, rep)
    near = jnp.transpose(near, (2, 0, 3, 1))[:, :, :, None, :]
    return jnp.broadcast_to(near, (N_KV_C, nw, rep, copies, tq)).reshape(N_KV_C, nw, rep * copies * tq)


def _permute_w_in(w):
    wt = jnp.swapaxes(w, 1, 2)
    sizes = (WIDTH_A, WIDTH_A, QK_B, QK_B, WIDTH_B, GATE_RANK_B, WIDTH_B, WIDTH_C, KV_C, KV_C, QI_C,
             D_IDX, N_IDX_HEADS)
    parts, off = [], 0
    for n in sizes:
        parts.append(wt[:, off:off + n])
        off += n
    ua, va, qb, kb, vb, lrb, og, qc, kc, vc, qic, kic, wic = parts
    pad = jnp.zeros((w.shape[0], H_WIDTH - (C_MISC + D_IDX + GATE_RANK_B + N_IDX_HEADS), w.shape[1]), w.dtype)
    return jnp.concatenate([qb, kb, vb, og, qc, kc, vc, qic, ua, va, kic, lrb, wic, pad], axis=1).astype(BF16)


def kernel(x_prompt, x_sample, cache_c_k, cache_c_v, cache_c_kidx, state_b_s, norm_mix, w_in, norm_a_v,
           w_s_a, b_s_a, w_gate_b, b_gate_b, norm_b_o, rel_bias, w_o, norm_ffn, w_up, w_down, norm_final):
    depth = w_in.shape[0]
    nbp, t, _ = x_prompt.shape
    nbs, ts, _ = x_sample.shape
    p = cache_c_k.shape[2]
    assert nbp == 1 and ts == CHUNK and t % FAR_KW == 0 and t % GMLP_ROWS == 0 and (nbs * ts) % GMLP_ROWS == 0
    assert p % NEAR_SPAN == 0 and p >= 2 * NEAR_SPAN
    assert t % PROMPT_TQ == 0 and t % (GLA_SUB * CHUNK) == 0 and nbs % GLA_SUB == 0

    x = jnp.concatenate([x_prompt[0], x_sample.reshape(nbs * ts, D_MODEL)], axis=0)
    w_in_p = _permute_w_in(w_in)
    wo_a = w_o[:, :WIDTH_A].astype(BF16)
    wo_b = w_o[:, WIDTH_A:WIDTH_A + WIDTH_B].astype(BF16)
    wo_c = w_o[:, WIDTH_A + WIDTH_B:].astype(BF16)
    w_up_b = w_up.astype(BF16)
    w_down_b = w_down.astype(BF16)
    w_gate = w_gate_b.astype(BF16)
    np_, ns_ = GMLP_ROWS // GROUP_A, GMLP_ROWS // CHUNK
    w_s2 = jnp.stack([jnp.tile(w_s_a, (1, 1, np_, np_)),
                      jnp.tile(w_s_a[:, :, :CHUNK, :CHUNK], (1, 1, ns_, ns_))], axis=1)
    b_s2 = jnp.stack([jnp.tile(b_s_a, (1, 1, np_)), jnp.tile(b_s_a[:, :, :CHUNK], (1, 1, ns_))], axis=1)
    b_s2 = jnp.swapaxes(b_s2, 2, 3)
    s_all = jnp.concatenate([jnp.zeros((depth, GLA_SUB) + state_b_s.shape[2:], F32), state_b_s], axis=1)
    bias_near = _bias_table(rel_bias, PROMPT_TQ, NEAR_SPAN + PROMPT_TQ)
    bias_near_s = _bias_table(rel_bias, CHUNK, NEAR_SPAN + CHUNK, copies=SAMPLE_PAIR)
    pk_all = cache_c_k.reshape(depth, nbs, N_KV_C * p, HD_C)
    pv_all = cache_c_v.reshape(depth, nbs, N_KV_C * p, HD_C)
    pki_all = jnp.swapaxes(cache_c_kidx, 2, 3)

    row = lambda a: a[:, None, :]
    outs = [[] for _ in range(9)]
    for l in range(depth):
        h = _inproj(x, row(norm_mix), w_in_p, l)
        hp, hs = h[:t], h[t:]
        ya, vn = _gmlp(h, row(norm_a_v), w_s2, b_s2, t, l)
        yb, s_out = _gla(h, w_gate, row(b_gate_b), row(norm_b_o), s_all, t, l)
        kb = hp[:, C_KC:C_KC + KV_C].astype(BF16)
        vt = hp[:, C_VC:C_VC + KV_C].astype(BF16).T
        vt_far = jnp.swapaxes(vt.reshape(KV_C, t // FAR_KW, FAR_KW), 0, 1)
        vt_near = jnp.swapaxes(vt.reshape(KV_C, t // NEAR_SPAN, NEAR_SPAN), 0, 1)
        kib = hp[:, C_MISC:C_MISC + D_IDX].astype(BF16)
        yc_p = _dsa_prompt(h, kb, vt_far, vt_near, kib, bias_near, t)
        yc_s = _dsa_sample(h, pk_all, pv_all, pki_all, bias_near_s, t, l)
        x = _outproj(x, ya, yb, yc_p, yc_s, wo_a, wo_b, wo_c, l)
        x = _ffn(x, row(norm_ffn), w_up_b, w_down_b, l)

        outs[0].append(hp[:, C_KC:C_KC + KV_C].reshape(1, t, N_KV_C, HD_C))
        outs[1].append(hp[:, C_VC:C_VC + KV_C].reshape(1, t, N_KV_C, HD_C))
        outs[2].append(hp[:, C_MISC:C_MISC + D_IDX].reshape(1, t, D_IDX))
        outs[3].append(s_out[GLA_SUB - 1:GLA_SUB])
        outs[4].append(hs[:, C_KC:C_KC + KV_C].reshape(nbs, ts, N_KV_C, HD_C))
        outs[5].append(hs[:, C_VC:C_VC + KV_C].reshape(nbs, ts, N_KV_C, HD_C))
        outs[6].append(hs[:, C_MISC:C_MISC + D_IDX].reshape(nbs, ts, D_IDX))
        outs[7].append(s_out[GLA_SUB:])
        outs[8].append(vn.reshape(nbs, ts, WIDTH_A))

    y_prompt = _final_norm(x, norm_final[None], 0, t)
    y_sample = _final_norm(x, norm_final[None], t, nbs * ts)
    return (y_prompt[None], y_sample.reshape(nbs, ts, D_MODEL)) + tuple(jnp.stack(o) for o in outs)
```

```python
import functools
import math

import jax
import jax.numpy as jnp
from jax import lax
from jax.experimental import pallas as pl
from jax.experimental.pallas import tpu as pltpu

BF16 = jnp.bfloat16
F32 = jnp.float32
I32 = jnp.int32

D_MODEL = 2048
EPS = 1e-6
CHUNK = 64
GROUP_A = 128
N_GROUPS_A = 4
WIDTH_A = 512
N_HEADS_B = 6
DK_B = 64
DV_B = 128
GATE_RANK_B = 16
GATE_TEMP_B = 16.0
WIDTH_B = 768
N_HEADS_C = 6
N_KV_C = 2
HD_C = 128
N_IDX_HEADS = 8
D_IDX = 64
TOPK_MAX = 256
WIDTH_C = 768
N_BUCKETS = 32
MAX_DISTANCE = 128
D_FF = 4 * D_MODEL
QK_B = N_HEADS_B * DK_B
KV_C = N_KV_C * HD_C
QI_C = N_IDX_HEADS * D_IDX

LANES = 128
SUBLANES = 8
ONES_ROWS = 16
VMEM_LIMIT = 56 * 1024 * 1024

C_QB = 0
C_KB = 384
C_VB = 768
C_OG = 1536
C_QC = 2304
C_KC = 3072
C_VC = 3328
C_QIC = 3584
C_UA = 4096
C_VA = 4608
C_MISC = 5120
MISC_LRB = D_IDX
MISC_WIC = D_IDX + GATE_RANK_B
H_WIDTH = 5376

INT_MIN = -2147483648
INT_MAX = 2147483647
SNAP_EVERY = 8
BISECT_AFTER = 40
MAX_SEARCH_STEPS = 80
MASKED = -1e30
LOG2E = 1.4426950408889634
NEAR_SPAN = 128
FAR_KW = 512
PROMPT_TQ = 256
SOFTMAX_ROWS = 64
FOLD_WAYS = 8
GLA_SUB = 4
GMLP_ROWS = 256
SAMPLE_PAIR = 1


def _pick(n, cands):
    for c in cands:
        if n % c == 0:
            return c
    raise ValueError(f"no tile for {n}")


def _cparams(sem):
    return pltpu.CompilerParams(dimension_semantics=sem, vmem_limit_bytes=VMEM_LIMIT)


def _rms(x):
    return x * lax.rsqrt(jnp.mean(x * x, axis=-1, keepdims=True) + EPS)


def _dot(a, b):
    return jnp.dot(a, b, preferred_element_type=F32)


def _dot_nt(a, b):
    return lax.dot_general(a, b, (((1,), (1,)), ((), ())), preferred_element_type=F32)


def _dot_tn(a, b):
    return lax.dot_general(a, b, (((0,), (0,)), ((), ())), preferred_element_type=F32)


def _inproj_kernel(x_ref, g_ref, w_ref, o_ref, xn_ref):
    @pl.when(pl.program_id(1) == 0)
    def _():
        xn_ref[...] = (_rms(x_ref[...]) * g_ref[...]).astype(BF16)

    o_ref[...] = _dot_nt(xn_ref[...], w_ref[...])


def _inproj(x, g, w, l):
    m = x.shape[0]
    tm = _pick(m, (1024, 512, 256, 128))
    tn = 768
    return pl.pallas_call(
        _inproj_kernel,
        grid=(m // tm, H_WIDTH // tn),
        in_specs=[pl.BlockSpec((tm, D_MODEL), lambda i, j: (i, 0)),
                  pl.BlockSpec((None, 1, D_MODEL), lambda i, j: (l, 0, 0)),
                  pl.BlockSpec((None, tn, D_MODEL), lambda i, j: (l, j, 0))],
        out_specs=pl.BlockSpec((tm, tn), lambda i, j: (i, j)),
        out_shape=jax.ShapeDtypeStruct((m, H_WIDTH), F32),
        scratch_shapes=[pltpu.VMEM((tm, D_MODEL), BF16)],
        compiler_params=_cparams(("arbitrary", "arbitrary")),
        name="inproj",
    )(x, g, w)


def _gmlp_kernel(h_ref, gv_ref, w_ref, b_ref, ya_ref, vn_ref, *, n_prompt_blocks):
    i = pl.program_id(0)
    hv = h_ref[...]
    u = jax.nn.gelu(hv[:, :WIDTH_A])
    v = jax.nn.gelu(hv[:, WIDTH_A:])
    r = lax.broadcasted_iota(I32, (GMLP_ROWS, GMLP_ROWS), 0)
    c = lax.broadcasted_iota(I32, (GMLP_ROWS, GMLP_ROWS), 1)
    shift = 7 - (i >= n_prompt_blocks).astype(I32)
    keep = (c <= r) & (jnp.right_shift(r, shift) == jnp.right_shift(c, shift))
    for g in range(N_GROUPS_A):
        sl = slice(g * GROUP_A, (g + 1) * GROUP_A)
        vn = _rms(v[:, sl]) * gv_ref[:, sl]
        vn_ref[:, sl] = vn
        wm = jnp.where(keep, w_ref[0, g], 0.0).astype(BF16)
        z = _dot(wm, vn.astype(BF16)) + b_ref[0][:, g:g + 1]
        ya_ref[:, sl] = (u[:, sl] * z).astype(BF16)


def _gmlp(h, gv, w2, b2, n_prompt_rows, l):
    m = h.shape[0]
    rows = GMLP_ROWS
    npb = n_prompt_rows // rows
    nsb = (m - n_prompt_rows) // rows
    return pl.pallas_call(
        functools.partial(_gmlp_kernel, n_prompt_blocks=npb),
        grid=(npb + nsb,),
        in_specs=[pl.BlockSpec((rows, 2 * WIDTH_A), lambda i: (i, C_UA // (2 * WIDTH_A))),
                  pl.BlockSpec((None, 1, WIDTH_A), lambda i: (l, 0, 0)),
                  pl.BlockSpec((None, 1, N_GROUPS_A, rows, rows),
                               lambda i: (l, jnp.where(i >= npb, 1, 0), 0, 0, 0)),
                  pl.BlockSpec((None, 1, rows, N_GROUPS_A),
                               lambda i: (l, jnp.where(i >= npb, 1, 0), 0, 0))],
        out_specs=[pl.BlockSpec((rows, WIDTH_A), lambda i: (i, 0)),
                   pl.BlockSpec((rows, WIDTH_A), lambda i: (jnp.maximum(i - npb, 0), 0))],
        out_shape=[jax.ShapeDtypeStruct((m, WIDTH_A), BF16),
                   jax.ShapeDtypeStruct((nsb * rows, WIDTH_A), F32)],
        compiler_params=_cparams(("arbitrary",)),
        name="gmlp",
    )(h, gv, w2, b2)


def _gla_kernel(q_ref, k_ref, v_ref, og_ref, misc_ref, wg_ref, bg_ref, gbo_ref, s0_ref,
                yb_ref, sout_ref, st_ref, *, n_prompt_steps):
    j = pl.program_id(0)
    rows = GLA_SUB * CHUNK
    fresh = j >= n_prompt_steps

    lrb = misc_ref[:, MISC_LRB:MISC_LRB + GATE_RANK_B].astype(BF16)
    x = _dot(lrb, wg_ref[...]) + bg_ref[...]
    g = (jnp.minimum(x, 0.0) - jnp.log1p(jnp.exp(-jnp.abs(x)))) * (1.0 / GATE_TEMP_B)
    rr = lax.broadcasted_iota(I32, (rows, rows), 0)
    cc = lax.broadcasted_iota(I32, (rows, rows), 1)
    same = (rr >> 6) == (cc >> 6)
    tril = same & (cc <= rr)
    g1 = g.astype(BF16)
    r1 = g - g1.astype(F32)
    g2 = r1.astype(BF16)
    g3 = (r1 - g2.astype(F32)).astype(BF16)

    tri = jnp.where(tril, 1.0, 0.0).astype(BF16)
    b = _dot(tri, g1) + _dot(tri, g2) + _dot(tri, g3)

    def chunk_row(r):
        return jnp.concatenate([jnp.broadcast_to(b[c * CHUNK + r:c * CHUNK + r + 1], (CHUNK, QK_B))
                                for c in range(GLA_SUB)], axis=0)

    b_last = chunk_row(CHUNK - 1)
    b_mid = chunk_row(CHUNK // 2 - 1)
    qs = q_ref[...] * (DK_B ** -0.5)
    kk = k_ref[...]
    q_inter = (qs * jnp.exp(b)).astype(BF16)
    q_intra = (qs * jnp.exp(b - b_mid)).astype(BF16)
    k_intra = (kk * jnp.exp(b_mid - b)).astype(BF16)
    k_state = (kk * jnp.exp(b_last - b)).astype(BF16)
    dec = jnp.exp(b_last)
    for h in range(N_HEADS_B):
        sk = slice(h * DK_B, (h + 1) * DK_B)
        sv = slice(h * DV_B, (h + 1) * DV_B)
        vh = v_ref[:, sv].astype(BF16)
        att = jnp.where(tril, _dot_nt(q_intra[:, sk], k_intra[:, sk]), 0.0)
        o_intra = _dot(att.astype(BF16), vh)
        st = st_ref[h]
        o_inter = []
        for c in range(GLA_SUB):
            rc = slice(c * CHUNK, (c + 1) * CHUNK)
            start = fresh | ((j == 0) & (c == 0))
            st = jnp.where(start, s0_ref[c, h].T, st)
            o_inter.append(_dot_nt(q_inter[rc, sk], st.astype(BF16)))
            st = st * dec[c * CHUNK:c * CHUNK + 1, sk] + _dot_tn(vh[rc], k_state[rc, sk])
            sout_ref[c, h] = st.T
        st_ref[h] = st
        o = jnp.concatenate(o_inter, axis=0) + o_intra
        on = _rms(o) * gbo_ref[:, sv]
        yb_ref[:, sv] = (on * jax.nn.silu(og_ref[:, sv])).astype(BF16)


def _gla(h, wg, bg, gbo, s_in, n_prompt_rows, l):
    m = h.shape[0]
    rows = GLA_SUB * CHUNK
    nps = n_prompt_rows // rows
    nslots = s_in.shape[1]

    def blk(j):
        return jnp.where(j < nps, 0, j - nps + 1)

    return pl.pallas_call(
        functools.partial(_gla_kernel, n_prompt_steps=nps),
        grid=(m // rows,),
        in_specs=[pl.BlockSpec((rows, QK_B), lambda j: (j, C_QB // QK_B)),
                  pl.BlockSpec((rows, QK_B), lambda j: (j, C_KB // QK_B)),
                  pl.BlockSpec((rows, WIDTH_B), lambda j: (j, C_VB // WIDTH_B)),
                  pl.BlockSpec((rows, WIDTH_B), lambda j: (j, C_OG // WIDTH_B)),
                  pl.BlockSpec((rows, LANES), lambda j: (j, C_MISC // LANES)),
                  pl.BlockSpec((None, GATE_RANK_B, QK_B), lambda j: (l, 0, 0)),
                  pl.BlockSpec((None, 1, QK_B), lambda j: (l, 0, 0)),
                  pl.BlockSpec((None, 1, WIDTH_B), lambda j: (l, 0, 0)),
                  pl.BlockSpec((None, GLA_SUB, N_HEADS_B, DK_B, DV_B), lambda j: (l, blk(j), 0, 0, 0))],
        out_specs=[pl.BlockSpec((rows, WIDTH_B), lambda j: (j, 0)),
                   pl.BlockSpec((GLA_SUB, N_HEADS_B, DK_B, DV_B), lambda j: (blk(j), 0, 0, 0))],
        out_shape=[jax.ShapeDtypeStruct((m, WIDTH_B), BF16),
                   jax.ShapeDtypeStruct((nslots, N_HEADS_B, DK_B, DV_B), F32)],
        scratch_shapes=[pltpu.VMEM((N_HEADS_B, DV_B, DK_B), F32)],
        compiler_params=_cparams(("arbitrary",)),
        name="gla",
    )(h, h, h, h, h, wg, bg, gbo, s_in)


def _fold_rows(x, op, each=lambda v: v):
    tile = SUBLANES * (4 // x.dtype.itemsize)
    n_tiles = x.shape[0] // tile
    parts = [each(x[j * tile:(j + 1) * tile]) for j in range(min(FOLD_WAYS, n_tiles))]
    for j in range(FOLD_WAYS, n_tiles):
        parts[j % FOLD_WAYS] = op(parts[j % FOLD_WAYS], each(x[j * tile:(j + 1) * tile]))
    while len(parts) > 1:
        nxt = [op(parts[j], parts[j + 1]) for j in range(0, len(parts) - 1, 2)]
        parts = nxt + parts[len(parts) - len(parts) % 2:]
    return parts[0]


def _loop(n, body, init):
    if isinstance(n, int):
        carry = init
        for c in range(n):
            carry = body(c, carry)
        return carry
    return lax.fori_loop(0, n, body, init)


REP_C = N_HEADS_C // N_KV_C
KV_SLICES = [slice(g * HD_C, (g + 1) * HD_C) for g in range(N_KV_C)]


def _scaled_queries(q, qi, misc_t):
    qi_s = (qi * (D_IDX ** -0.5)).astype(BF16)
    wi_t = misc_t[MISC_WIC:MISC_WIC + N_IDX_HEADS, :] * (N_IDX_HEADS ** -0.5)
    qs = (q * (HD_C ** -0.5 * LOG2E)).astype(BF16)
    return ([qi_s[:, h * D_IDX:(h + 1) * D_IDX] for h in range(N_IDX_HEADS)], wi_t,
            [qs[:, hh * HD_C:(hh + 1) * HD_C] for hh in range(N_HEADS_C)])


def _dsa_core(qi_all, wi_t, q_g, far_ki, far_k, far_vx, nf, far_adm, near_ki, near_k, near_vx, near_adm, n_adm,
              bias_near_ref, emit, scratch, *, tq, kw, nw, topk):
    fkeys_ref, nkeys_ref, m_ref, acc_ref, ss_ref, p_ref = scratch
    rep = REP_C
    groups = range(N_KV_C)
    kf = float(topk)

    def extremes(key, carry):
        kmax, kmin = carry
        kmax = jnp.maximum(kmax, _fold_rows(key, jnp.maximum))
        kmin = jnp.minimum(kmin, _fold_rows(key, jnp.minimum, lambda v: jnp.where(v == INT_MIN, INT_MAX, v)))
        return kmax, kmin

    def fill_keys(ki_blk, adm, put, carry):
        d = _dot_nt(ki_blk[0], qi_all[0])
        for kb, qb in zip(ki_blk[1:], qi_all[1:]):
            d = d + _dot_nt(kb, qb)
        for r0 in range(0, d.shape[0], SOFTMAX_ROWS):
            rows = slice(r0, r0 + SOFTMAX_ROWS)
            acc = None
            for h in range(N_IDX_HEADS):
                t = wi_t[h:h + 1, :] * jnp.maximum(d[rows, h * tq:(h + 1) * tq], 0.0)
                acc = t if acc is None else acc + t
            bits = lax.bitcast_convert_type(acc, I32)
            key = bits ^ ((bits >> 31) & 0x7FFFFFFF)
            key = jnp.where(acc == 0.0, 0, key)
            if adm is not None:
                key = jnp.where(adm(rows), key, INT_MIN)
            put(rows, key)
            carry = extremes(key, carry)
        return carry

    def fill(c, carry):
        def put(rows, key):
            fkeys_ref[c, rows, :] = key

        return fill_keys(far_ki(c), None if far_adm is None else far_adm(c), put, carry)

    def put_near(rows, key):
        nkeys_ref[rows, :] = key

    carry = _loop(nf, fill, (jnp.full((SUBLANES, tq), INT_MIN, I32), jnp.full((SUBLANES, tq), INT_MAX, I32)))
    kmax, kmin = fill_keys(near_ki, near_adm, put_near, carry)
    kmax = jnp.max(kmax, axis=0, keepdims=True)
    kmin = jnp.min(kmin, axis=0, keepdims=True)

    def count(thr, strict):
        def cmp(x):
            return _fold_rows(x, jnp.add, lambda v: jnp.where((v > thr) if strict else (v >= thr), 1.0, 0.0))

        part = _loop(nf, lambda c, a: a + cmp(fkeys_ref[c]), jnp.zeros((SUBLANES, tq), F32))
        part = part + cmp(nkeys_ref[...])
        return jnp.sum(part, axis=0, keepdims=True)

    def bits_flip(x):
        return x ^ ((x >> 31) & 0x7FFFFFFF)

    def any_lane(flag):
        return jnp.max(jnp.where(flag, 1.0, 0.0))

    def snap(lo, hi):
        def pull(x, carry):
            kin, kax = carry
            kin = jnp.minimum(kin, _fold_rows(x, jnp.minimum, lambda v: jnp.where(v >= lo, v, INT_MAX)))
            kax = jnp.maximum(kax, _fold_rows(x, jnp.maximum, lambda v: jnp.where(v < hi, v, INT_MIN)))
            return kin, kax

        init = (jnp.full((SUBLANES, tq), INT_MAX, I32), jnp.full((SUBLANES, tq), INT_MIN, I32))
        kin, kax = pull(nkeys_ref[...], _loop(nf, lambda c, carry: pull(fkeys_ref[c], carry), init))
        return jnp.min(kin, axis=0, keepdims=True), jnp.max(kax, axis=0, keepdims=True)

    enough = n_adm >= kf
    log_k = math.log2(topk)

    def open_lanes(lo, hi, flo):
        return enough & (flo > kf) & (hi > lo + 1)

    def log_excess(c):
        return jnp.log2(jnp.maximum(c, 0.5)) - log_k

    def step(state):
        it, _, lo, hi, flo, fhi, glo, ghi, side = state
        active = open_lanes(lo, hi, flo)

        def pull_in(_):
            kin, kax = snap(lo, hi)
            return jnp.where(active, kin, lo), jnp.where(active, kax + 1, hi), flo, fhi, glo, ghi, side

        def probe(_):
            v_lo = lax.bitcast_convert_type(bits_flip(lo), F32)
            v_hi = lax.bitcast_convert_type(bits_flip(hi), F32)
            frac = jnp.minimum(jnp.maximum(glo / (glo - ghi), 0.02), 0.98)
            t_int = bits_flip(lax.bitcast_convert_type(v_lo + (v_hi - v_lo) * frac, I32))
            t_mid = (lo & hi) + ((lo ^ hi) >> 1)
            t = jnp.where(it >= BISECT_AFTER, t_mid, t_int)
            t = jnp.minimum(jnp.maximum(t, lo + 1), hi - 1)
            c = count(t, False)
            g = log_excess(c)
            up = active & (c >= kf)
            dn = active & (c < kf)
            ghi2 = jnp.where(up & (side > 0.0), ghi * 0.5, ghi)
            glo2 = jnp.where(dn & (side < 0.0), glo * 0.5, glo)
            return (jnp.where(up, t, lo), jnp.where(dn, t, hi), jnp.where(up, c, flo), jnp.where(dn, c, fhi),
                    jnp.where(up, g, glo2), jnp.where(dn, g, ghi2), jnp.where(up, 1.0, jnp.where(dn, -1.0, side)))

        is_snap = (it % SNAP_EVERY == SNAP_EVERY - 1) & (it < BISECT_AFTER)
        lo, hi, flo, fhi, glo, ghi, side = lax.cond(is_snap, pull_in, probe, 0)
        return it + 1, any_lane(open_lanes(lo, hi, flo)), lo, hi, flo, fhi, glo, ghi, side

    hi0 = kmax + 1
    zero = jnp.zeros((1, tq), F32)
    state = (jnp.int32(0), any_lane(open_lanes(kmin, hi0, n_adm)), kmin, hi0, n_adm, zero,
             log_excess(n_adm), log_excess(zero), zero)
    state = lax.while_loop(lambda st: (st[1] > 0.0) & (st[0] < MAX_SEARCH_STEPS), step, state)
    thr = jnp.where(enough, state[2], INT_MIN)
    has_tie = any_lane(enough & (state[4] > kf)) > 0.0

    @pl.when(has_tie)
    def _():
        need = kf - count(thr, True)

        def lower(n):
            a = lax.broadcasted_iota(I32, (n, n), 0)
            b = lax.broadcasted_iota(I32, (n, n), 1)
            return jnp.where(b < a, 1.0, 0.0).astype(BF16)

        def demote(keys, run, lt):
            eq = (keys == thr) & (keys > INT_MIN)
            eqf = jnp.where(eq, 1.0, 0.0)
            before = _dot(lt, eqf.astype(BF16)) + run
            keys = jnp.where(eq & (before >= need), INT_MIN, keys)
            return keys, run + jnp.sum(eqf, axis=0, keepdims=True)

        lt_far = lower(kw)

        def step(c, run):
            keys, run = demote(fkeys_ref[c], run, lt_far)
            fkeys_ref[c] = keys
            return run

        run = _loop(nf, step, jnp.zeros((1, tq), F32))
        keys, _ = demote(nkeys_ref[...], run, lower(nw))
        nkeys_ref[...] = keys

    thr_sel = jnp.maximum(thr, INT_MIN + 1)
    m_ref[...] = jnp.full(m_ref.shape, MASKED, F32)
    acc_ref[...] = jnp.zeros(acc_ref.shape, F32)

    def logits(k_blk):
        out = []
        for g in groups:
            s = _dot_nt(k_blk[g][0], q_g[g][0])
            for kb, qb in zip(k_blk[g][1:], q_g[g][1:]):
                s = s + _dot_nt(kb, qb)
            out.append(s)
        return out

    def attend(slot, keys_rows, vx_blk, bias_ref, n):
        tiles = [slice(r0, r0 + SOFTMAX_ROWS) for r0 in range(0, n, SOFTMAX_ROWS)]
        heads = [slice(r * tq, (r + 1) * tq) for r in range(rep)]

        run = [None] * N_KV_C
        for rows in tiles:
            selb = jnp.where(keys_rows(rows) >= thr_sel, 0.0, MASKED)
            for g in groups:
                s = ss_ref[slot, g, rows, :]
                if bias_ref is not None:
                    s = s + bias_ref[g, rows, :]
                tops = []
                for hd in heads:
                    sm = s[:, hd] + selb
                    ss_ref[slot, g, rows, hd] = sm
                    tops.append(_fold_rows(sm, jnp.maximum))
                top = jnp.concatenate(tops, axis=1)
                run[g] = top if run[g] is None else jnp.maximum(run[g], top)
        m_new = []
        for g in groups:
            m_old = m_ref[g]
            m_new.append(jnp.maximum(m_old, jnp.max(run[g], axis=0, keepdims=True)))
            acc_ref[g] = jnp.exp2(m_old - m_new[g]) * acc_ref[g]
            m_ref[g] = m_new[g]
        for rows in tiles:
            for g in groups:
                p_ref[g, rows, :] = jnp.exp2(ss_ref[slot, g, rows, :] - m_new[g]).astype(BF16)
        for g in groups:
            p = p_ref[g, :n, :]
            pv = [_dot(vx, p) for vx in vx_blk[g]]
            acc_ref[g] += pv[0] if len(pv) == 1 else jnp.concatenate(pv, axis=0)

    last = fkeys_ref.shape[0] - 1

    def put_logits(slot, k_blk, n):
        for g, s in enumerate(logits(k_blk)):
            ss_ref[slot, g, :n, :] = s

    def far_body(c, cur):
        put_logits(1 - cur, far_k(min(c + 1, last) if isinstance(c, int) else jnp.minimum(c + 1, last)), kw)
        attend(cur, lambda rows: fkeys_ref[c, rows, :], far_vx(c), None, kw)

    def far_step(c, carry):
        if isinstance(c, int):
            far_body(c, c % 2)
        else:
            for cur in range(2):
                pl.when(c % 2 == cur)(functools.partial(far_body, c, cur))
        return carry

    put_logits(0, far_k(0), kw)
    _loop(nf, far_step, 0)
    put_logits(0, near_k, nw)
    attend(0, lambda rows: nkeys_ref[rows, :], near_vx, bias_near_ref, nw)

    for g in groups:
        emit(g, acc_ref[g])


def _with_ones(vt):
    return jnp.concatenate([vt, jnp.ones((ONES_ROWS, vt.shape[1]), BF16)], axis=0)


def _dsa_scratch(nf, tq, kw, nw, vx_rows):
    rep = REP_C
    return [pltpu.VMEM((nf, kw, tq), I32),
            pltpu.VMEM((nw, tq), I32),
            pltpu.VMEM((N_KV_C, 1, rep * tq), F32),
            pltpu.VMEM((N_KV_C, vx_rows, rep * tq), F32),
            pltpu.VMEM((2, N_KV_C, max(kw, nw), rep * tq), F32),
            pltpu.VMEM((N_KV_C, max(kw, nw), rep * tq), BF16)]


def _dsa_prompt_kernel(q_ref, qi_ref, misc_ref, k_ref, vt_far_ref, vt_near_ref, ki_ref, bias_near_ref,
                       out_ref, *scratch, tq, topk):
    i = pl.program_id(0)
    kw = FAR_KW
    nw = NEAR_SPAN + tq
    q0 = i * tq
    far_limit = q0 - NEAR_SPAN
    near0 = jnp.maximum(far_limit, 0)
    nf = (near0 + kw - 1) // kw

    def rows(c):
        return pl.ds(pl.multiple_of(c * kw, kw), kw)

    def far_adm(c):
        def rows_ok(rows):
            n = rows.stop - rows.start
            return (c * kw + rows.start + lax.broadcasted_iota(I32, (n, tq), 0)) < far_limit

        return rows_ok

    left = pl.ds(pl.multiple_of(near0, NEAR_SPAN), NEAR_SPAN)
    right = pl.ds(pl.multiple_of(q0, tq), tq)
    near_ki = jnp.concatenate([ki_ref[left, :], ki_ref[right, :]], axis=0)
    near_k = jnp.concatenate([k_ref[left, :], k_ref[right, :]], axis=0)
    per = tq // NEAR_SPAN
    near_vt = jnp.concatenate([vt_near_ref[near0 // NEAR_SPAN]] + [vt_near_ref[i * per + j] for j in range(per)],
                              axis=1)
    first_key = jnp.where(i > 0, 0, NEAR_SPAN)

    def near_adm(rows):
        n = rows.stop - rows.start
        key = rows.start + lax.broadcasted_iota(I32, (n, tq), 0)
        qry = lax.broadcasted_iota(I32, (n, tq), 1)
        return (key >= first_key) & (((key - NEAR_SPAN) >> 6) <= (qry >> 6))

    n_adm = ((((q0 + lax.broadcasted_iota(I32, (1, tq), 1)) >> 6) + 1) * CHUNK).astype(F32)

    qi_h, wi_t, q_h = _scaled_queries(q_ref[...], qi_ref[...], misc_ref[...].T)
    qi_all = jnp.concatenate(qi_h, axis=0)
    q_g = [[jnp.concatenate(q_h[g * REP_C:(g + 1) * REP_C], axis=0)] for g in range(N_KV_C)]

    def emit(g, acc):
        o = acc[:HD_C] / acc[HD_C:HD_C + 1]
        for r in range(REP_C):
            hh = g * REP_C + r
            out_ref[:, hh * HD_C:(hh + 1) * HD_C] = o[:, r * tq:(r + 1) * tq].T.astype(BF16)

    _dsa_core([qi_all], wi_t, q_g,
              lambda c: [ki_ref[rows(c), :]], lambda c: [[k_ref[rows(c), sl]] for sl in KV_SLICES],
              lambda c: [[_with_ones(vt_far_ref[c, sl, :])] for sl in KV_SLICES],
              nf, far_adm, [near_ki], [[near_k[:, sl]] for sl in KV_SLICES],
              [[_with_ones(near_vt[sl, :])] for sl in KV_SLICES], near_adm, n_adm,
              bias_near_ref, emit, scratch,
              tq=tq, kw=kw, nw=nw, topk=topk)


def _dsa_prompt(h, kb, vt_far, vt_near, kib, bias_near, t):
    tq = PROMPT_TQ
    topk = min(TOPK_MAX, t // 4)
    return pl.pallas_call(
        functools.partial(_dsa_prompt_kernel, tq=tq, topk=topk),
        grid=(t // tq,),
        in_specs=[pl.BlockSpec((tq, WIDTH_C), lambda i: (i, C_QC // WIDTH_C)),
                  pl.BlockSpec((tq, QI_C), lambda i: (i, C_QIC // QI_C)),
                  pl.BlockSpec((tq, LANES), lambda i: (i, C_MISC // LANES)),
                  pl.BlockSpec((t, KV_C), lambda i: (0, 0)),
                  pl.BlockSpec(vt_far.shape, lambda i: (0, 0, 0)),
                  pl.BlockSpec(vt_near.shape, lambda i: (0, 0, 0)),
                  pl.BlockSpec((t, D_IDX), lambda i: (0, 0)),
                  pl.BlockSpec(bias_near.shape, lambda i: (0, 0, 0))],
        out_specs=pl.BlockSpec((tq, WIDTH_C), lambda i: (i, 0)),
        out_shape=jax.ShapeDtypeStruct((t, WIDTH_C), BF16),
        scratch_shapes=_dsa_scratch(t // FAR_KW, tq, FAR_KW, NEAR_SPAN + tq, HD_C + ONES_ROWS),
        compiler_params=_cparams(("arbitrary",)),
        name="dsa_prompt",
    )(h, h, h, kb, vt_far, vt_near, kib, bias_near)


def _dsa_sample_kernel(q_ref, qi_ref, misc_ref, *refs, kw, nf, topk):
    nb = SAMPLE_PAIR
    kvn_ref, pk_ref, pv_ref = refs[:nb], refs[nb:2 * nb], refs[2 * nb:3 * nb]
    pki_ref, bias_near_ref, out_ref = refs[3 * nb:3 * nb + 3]
    scratch = refs[3 * nb + 3:]
    tq = nb * CHUNK
    far_len = kw * nf
    vx_rows = HD_C + ONES_ROWS
    misc = misc_ref[...]
    kvn = [r[...] for r in kvn_ref]
    lane_seq = lax.broadcasted_iota(I32, (1, tq), 1) >> 6
    q, qi, misc_t = q_ref[...], qi_ref[...], misc.T

    def own_rows(x, b):
        return jnp.where((lax.broadcasted_iota(I32, x.shape, 0) >> 6) == b, x, 0.0)

    per_seq = [_scaled_queries(own_rows(q, b), own_rows(qi, b), misc_t) for b in range(nb)]
    wi_t = per_seq[0][1]
    qi_all = [jnp.concatenate(per_seq[b][0], axis=0) for b in range(nb)]
    q_g = [[jnp.concatenate(per_seq[b][2][g * REP_C:(g + 1) * REP_C], axis=0) for b in range(nb)]
           for g in range(N_KV_C)]
    seq_rows = [slice(b * CHUNK, (b + 1) * CHUNK) for b in range(nb)]

    def group_rows(refs, b, start, n, g):
        return refs[b][0, pl.ds(2 * start + g, n, stride=2), :]

    def side_by_side(per_seq):
        return [per_seq(b).astype(BF16) for b in range(nb)]

    def stacked_vx(per_seq):
        return [_with_ones(per_seq(b).astype(BF16)) for b in range(nb)]

    def far_ki(c):
        return [pki_ref[b, :, pl.ds(c * kw, kw)].T.astype(BF16) for b in range(nb)]

    def far_k(c):
        return [side_by_side(lambda b: group_rows(pk_ref, b, c * kw, kw, g)) for g in range(N_KV_C)]

    def far_vx(c):
        return [stacked_vx(lambda b: group_rows(pv_ref, b, c * kw, kw, g).T) for g in range(N_KV_C)]

    near_ki = [jnp.concatenate([pki_ref[b, :, pl.ds(far_len, NEAR_SPAN)].T, misc[seq_rows[b], :D_IDX]],
                               axis=0).astype(BF16) for b in range(nb)]
    near_k = [side_by_side(lambda b: jnp.concatenate(
        [group_rows(pk_ref, b, far_len, NEAR_SPAN, g), kvn[b][:, KV_SLICES[g]]], axis=0))
        for g in range(N_KV_C)]
    near_vx = [stacked_vx(lambda b: jnp.concatenate(
        [group_rows(pv_ref, b, far_len, NEAR_SPAN, g).T, kvn[b][:, KV_C + g * HD_C:KV_C + (g + 1) * HD_C].T],
        axis=1)) for g in range(N_KV_C)]

    def emit(g, acc):
        for r in range(REP_C):
            hh = g * REP_C + r
            cols = slice(r * tq, (r + 1) * tq)
            o = None
            for b in range(nb):
                top = b * vx_rows
                ob = acc[top:top + HD_C, cols] / acc[top + HD_C:top + HD_C + 1, cols]
                o = ob if o is None else jnp.where(lane_seq == b, ob, o)
            out_ref[:, hh * HD_C:(hh + 1) * HD_C] = o.T.astype(BF16)

    _dsa_core(qi_all, wi_t, q_g, far_ki, far_k, far_vx, nf, None, near_ki, near_k, near_vx, None,
              jnp.full((1, tq), float(far_len + NEAR_SPAN + CHUNK), F32),
              bias_near_ref, emit, scratch,
              tq=tq, kw=kw, nw=NEAR_SPAN + CHUNK, topk=topk)


def _dsa_sample(h, pk, pv, pki, bias_near, t, l):
    nb, p = pki.shape[1], pki.shape[3]
    tq = SAMPLE_PAIR * CHUNK
    far_len = p - NEAR_SPAN
    kw = _pick(far_len, (640, 512, 384, 256, 128))
    nf = far_len // kw
    nw = NEAR_SPAN + CHUNK
    topk = min(TOPK_MAX, (p + CHUNK) // 4)
    row0 = t // tq

    def cache_block(s, b):
        return (l, SAMPLE_PAIR * b + s, 0, 0)

    def new_rows(s, b):
        return (t // CHUNK + SAMPLE_PAIR * b + s, C_KC // (2 * KV_C))

    return pl.pallas_call(
        functools.partial(_dsa_sample_kernel, kw=kw, nf=nf, topk=topk),
        grid=(nb // SAMPLE_PAIR,),
        in_specs=[pl.BlockSpec((tq, WIDTH_C), lambda b: (row0 + b, C_QC // WIDTH_C)),
                  pl.BlockSpec((tq, QI_C), lambda b: (row0 + b, C_QIC // QI_C)),
                  pl.BlockSpec((tq, LANES), lambda b: (row0 + b, C_MISC // LANES))]
                 + [pl.BlockSpec((CHUNK, 2 * KV_C), functools.partial(new_rows, s)) for s in range(SAMPLE_PAIR)]
                 + 2 * [pl.BlockSpec((None, 1, N_KV_C * p, HD_C), functools.partial(cache_block, s))
                        for s in range(SAMPLE_PAIR)]
                 + [pl.BlockSpec((None, SAMPLE_PAIR, D_IDX, p), lambda b: (l, b, 0, 0)),
                    pl.BlockSpec(bias_near.shape, lambda b: (0, 0, 0))],
        out_specs=pl.BlockSpec((tq, WIDTH_C), lambda b: (b, 0)),
        out_shape=jax.ShapeDtypeStruct((nb * CHUNK, WIDTH_C), BF16),
        scratch_shapes=_dsa_scratch(nf, tq, kw, nw, SAMPLE_PAIR * (HD_C + ONES_ROWS)),
        compiler_params=_cparams(("arbitrary",)),
        name="dsa_sample",
    )(h, h, h, *(SAMPLE_PAIR * [h]), *(SAMPLE_PAIR * [pk]), *(SAMPLE_PAIR * [pv]), pki, bias_near)


def _outproj_kernel(x_ref, ya_ref, yb_ref, ycp_ref, ycs_ref, wa_ref, wb_ref, wc_ref, o_ref, *, n_prompt_blocks):
    yc = jnp.where(pl.program_id(0) < n_prompt_blocks, ycp_ref[...], ycs_ref[...])
    o_ref[...] = (x_ref[...] + _dot(ya_ref[...], wa_ref[...]) + _dot(yb_ref[...], wb_ref[...])
                  + _dot(yc, wc_ref[...]))


def _outproj(x, ya, yb, yc_p, yc_s, wa, wb, wc, l):
    m = x.shape[0]
    tm = _pick(yc_s.shape[0], (1024, 512, 256, 128))
    tn = 512
    npb = yc_p.shape[0] // tm
    assert yc_p.shape[0] % tm == 0
    return pl.pallas_call(
        functools.partial(_outproj_kernel, n_prompt_blocks=npb),
        grid=(m // tm, D_MODEL // tn),
        in_specs=[pl.BlockSpec((tm, tn), lambda i, j: (i, j)),
                  pl.BlockSpec((tm, WIDTH_A), lambda i, j: (i, 0)),
                  pl.BlockSpec((tm, WIDTH_B), lambda i, j: (i, 0)),
                  pl.BlockSpec((tm, WIDTH_C), lambda i, j: (jnp.minimum(i, npb - 1), 0)),
                  pl.BlockSpec((tm, WIDTH_C), lambda i, j: (jnp.maximum(i - npb, 0), 0)),
                  pl.BlockSpec((None, WIDTH_A, tn), lambda i, j: (l, 0, j)),
                  pl.BlockSpec((None, WIDTH_B, tn), lambda i, j: (l, 0, j)),
                  pl.BlockSpec((None, WIDTH_C, tn), lambda i, j: (l, 0, j))],
        out_specs=pl.BlockSpec((tm, tn), lambda i, j: (i, j)),
        out_shape=jax.ShapeDtypeStruct((m, D_MODEL), F32),
        compiler_params=_cparams(("arbitrary", "arbitrary")),
        name="outproj",
    )(x, ya, yb, yc_p, yc_s, wa, wb, wc)


def _ffn_kernel(x_ref, g_ref, wu_ref, wd_ref, o_ref, xn_ref, acc_ref):
    f = pl.program_id(1)

    @pl.when(f == 0)
    def _():
        xn_ref[...] = (_rms(x_ref[...]) * g_ref[...]).astype(BF16)
        acc_ref[...] = jnp.zeros(acc_ref.shape, F32)

    a = jnp.maximum(_dot(xn_ref[...], wu_ref[...]), 0.0)
    acc_ref[...] += _dot((a * a).astype(BF16), wd_ref[...])

    @pl.when(f == pl.num_programs(1) - 1)
    def _():
        o_ref[...] = x_ref[...] + acc_ref[...]


def _ffn(x, g, wu, wd, l):
    m = x.shape[0]
    tm = _pick(m, (512, 256, 128))
    tf = 1024
    return pl.pallas_call(
        _ffn_kernel,
        grid=(m // tm, D_FF // tf),
        in_specs=[pl.BlockSpec((tm, D_MODEL), lambda i, f: (i, 0)),
                  pl.BlockSpec((None, 1, D_MODEL), lambda i, f: (l, 0, 0)),
                  pl.BlockSpec((None, D_MODEL, tf), lambda i, f: (l, 0, f)),
                  pl.BlockSpec((None, tf, D_MODEL), lambda i, f: (l, f, 0))],
        out_specs=pl.BlockSpec((tm, D_MODEL), lambda i, f: (i, 0)),
        out_shape=jax.ShapeDtypeStruct((m, D_MODEL), F32),
        scratch_shapes=[pltpu.VMEM((tm, D_MODEL), BF16), pltpu.VMEM((tm, D_MODEL), F32)],
        compiler_params=_cparams(("arbitrary", "arbitrary")),
        name="ffn",
    )(x, g, wu, wd)


def _norm_kernel(x_ref, g_ref, o_ref):
    o_ref[...] = _rms(x_ref[...]) * g_ref[...]


def _final_norm(x, g, row0, rows):
    tm = _pick(rows, (1024, 512, 256, 128))
    assert row0 % tm == 0
    return pl.pallas_call(
        _norm_kernel,
        grid=(rows // tm,),
        in_specs=[pl.BlockSpec((tm, D_MODEL), lambda i: (row0 // tm + i, 0)),
                  pl.BlockSpec((1, D_MODEL), lambda i: (0, 0))],
        out_specs=pl.BlockSpec((tm, D_MODEL), lambda i: (i, 0)),
        out_shape=jax.ShapeDtypeStruct((rows, D_MODEL), F32),
        compiler_params=_cparams(("arbitrary",)),
        name="final_norm",
    )(x, g)


def _t5_bucket(rel):
    half = N_BUCKETS // 2
    max_exact = half // 2
    n = jnp.abs(rel)
    nf = jnp.maximum(n, 1).astype(F32)
    large = max_exact + (jnp.log(nf / max_exact) / jnp.log(MAX_DISTANCE / max_exact)
                         * (half - max_exact)).astype(I32)
    large = jnp.minimum(large, half - 1)
    return jnp.where(rel > 0, half, 0) + jnp.where(n < max_exact, n, large)


def _bias_table(rel_bias, tq, nw, copies=1):
    j = jnp.arange(nw, dtype=I32)[:, None]
    t = jnp.arange(tq, dtype=I32)[None, :]
    hot = jax.nn.one_hot(_t5_bucket(j - NEAR_SPAN - t), N_BUCKETS, dtype=F32)
    near = jnp.einsum("jtb,bh->jth", hot, rel_bias, precision=lax.Precision.HIGHEST)
    far = rel_bias[_t5_bucket(jnp.int32(-(NEAR_SPAN + 1)))]
    rep = N_HEADS_C // N_KV_C
    near = ((near - far) * LOG2E).astype(F32).reshape(nw, tq, N_KV_C, rep)
    near = jnp.transpose(near, (2, 0, 3, 1))[:, :, :, None, :]
    return jnp.broadcast_to(near, (N_KV_C, nw, rep, copies, tq)).reshape(N_KV_C, nw, rep * copies * tq)


def _permute_w_in(w):
    wt = jnp.swapaxes(w, 1, 2)
    sizes = (WIDTH_A, WIDTH_A, QK_B, QK_B, WIDTH_B, GATE_RANK_B, WIDTH_B, WIDTH_C, KV_C, KV_C, QI_C,
             D_IDX, N_IDX_HEADS)
    parts, off = [], 0
    for n in sizes:
        parts.append(wt[:, off:off + n])
        off += n
    ua, va, qb, kb, vb, lrb, og, qc, kc, vc, qic, kic, wic = parts
    pad = jnp.zeros((w.shape[0], H_WIDTH - (C_MISC + D_IDX + GATE_RANK_B + N_IDX_HEADS), w.shape[1]), w.dtype)
    return jnp.concatenate([qb, kb, vb, og, qc, kc, vc, qic, ua, va, kic, lrb, wic, pad], axis=1).astype(BF16)


def kernel(x_prompt, x_sample, cache_c_k, cache_c_v, cache_c_kidx, state_b_s, norm_mix, w_in, norm_a_v,
           w_s_a, b_s_a, w_gate_b, b_gate_b, norm_b_o, rel_bias, w_o, norm_ffn, w_up, w_down, norm_final):
    depth = w_in.shape[0]
    nbp, t, _ = x_prompt.shape
    nbs, ts, _ = x_sample.shape
    p = cache_c_k.shape[2]
    assert nbp == 1 and ts == CHUNK and t % FAR_KW == 0 and t % GMLP_ROWS == 0 and (nbs * ts) % GMLP_ROWS == 0
    assert p % NEAR_SPAN == 0 and p >= 2 * NEAR_SPAN
    assert t % PROMPT_TQ == 0 and t % (GLA_SUB * CHUNK) == 0 and nbs % GLA_SUB == 0

    x = jnp.concatenate([x_prompt[0], x_sample.reshape(nbs * ts, D_MODEL)], axis=0)
    w_in_p = _permute_w_in(w_in)
    wo_a = w_o[:, :WIDTH_A].astype(BF16)
    wo_b = w_o[:, WIDTH_A:WIDTH_A + WIDTH_B].astype(BF16)
    wo_c = w_o[:, WIDTH_A + WIDTH_B:].astype(BF16)
    w_up_b = w_up.astype(BF16)
    w_down_b = w_down.astype(BF16)
    w_gate = w_gate_b.astype(BF16)
    np_, ns_ = GMLP_ROWS // GROUP_A, GMLP_ROWS // CHUNK
    w_s2 = jnp.stack([jnp.tile(w_s_a, (1, 1, np_, np_)),
                      jnp.tile(w_s_a[:, :, :CHUNK, :CHUNK], (1, 1, ns_, ns_))], axis=1)
    b_s2 = jnp.stack([jnp.tile(b_s_a, (1, 1, np_)), jnp.tile(b_s_a[:, :, :CHUNK], (1, 1, ns_))], axis=1)
    b_s2 = jnp.swapaxes(b_s2, 2, 3)
    s_all = jnp.concatenate([jnp.zeros((depth, GLA_SUB) + state_b_s.shape[2:], F32), state_b_s], axis=1)
    bias_near = _bias_table(rel_bias, PROMPT_TQ, NEAR_SPAN + PROMPT_TQ)
    bias_near_s = _bias_table(rel_bias, CHUNK, NEAR_SPAN + CHUNK, copies=SAMPLE_PAIR)
    pk_all = cache_c_k.reshape(depth, nbs, N_KV_C * p, HD_C)
    pv_all = cache_c_v.reshape(depth, nbs, N_KV_C * p, HD_C)
    pki_all = jnp.swapaxes(cache_c_kidx, 2, 3)

    row = lambda a: a[:, None, :]
    outs = [[] for _ in range(9)]
    for l in range(depth):
        h = _inproj(x, row(norm_mix), w_in_p, l)
        hp, hs = h[:t], h[t:]
        ya, vn = _gmlp(h, row(norm_a_v), w_s2, b_s2, t, l)
        yb, s_out = _gla(h, w_gate, row(b_gate_b), row(norm_b_o), s_all, t, l)
        kb = hp[:, C_KC:C_KC + KV_C].astype(BF16)
        vt = hp[:, C_VC:C_VC + KV_C].astype(BF16).T
        vt_far = jnp.swapaxes(vt.reshape(KV_C, t // FAR_KW, FAR_KW), 0, 1)
        vt_near = jnp.swapaxes(vt.reshape(KV_C, t // NEAR_SPAN, NEAR_SPAN), 0, 1)
        kib = hp[:, C_MISC:C_MISC + D_IDX].astype(BF16)
        yc_p = _dsa_prompt(h, kb, vt_far, vt_near, kib, bias_near, t)
        yc_s = _dsa_sample(h, pk_all, pv_all, pki_all, bias_near_s, t, l)
        x = _outproj(x, ya, yb, yc_p, yc_s, wo_a, wo_b, wo_c, l)
        x = _ffn(x, row(norm_ffn), w_up_b, w_down_b, l)

        outs[0].append(hp[:, C_KC:C_KC + KV_C].reshape(1, t, N_KV_C, HD_C))
        outs[1].append(hp[:, C_VC:C_VC + KV_C].reshape(1, t, N_KV_C, HD_C))
        outs[2].append(hp[:, C_MISC:C_MISC + D_IDX].reshape(1, t, D_IDX))
        outs[3].append(s_out[GLA_SUB - 1:GLA_SUB])
        outs[4].append(hs[:, C_KC:C_KC + KV_C].reshape(nbs, ts, N_KV_C, HD_C))
        outs[5].append(hs[:, C_VC:C_VC + KV_C].reshape(nbs, ts, N_KV_C, HD_C))
        outs[6].append(hs[:, C_MISC:C_MISC + D_IDX].reshape(nbs, ts, D_IDX))
        outs[7].append(s_out[GLA_SUB:])
        outs[8].append(vn.reshape(nbs, ts, WIDTH_A))

    y_prompt = _final_norm(x, norm_final[None], 0, t)
    y_sample = _final_norm(x, norm_final[None], t, nbs * ts)
    return (y_prompt[None], y_sample.reshape(nbs, ts, D_MODEL)) + tuple(jnp.stack(o) for o in outs)
```

```python
import functools
import math

import jax
import jax.numpy as jnp
from jax import lax
from jax.experimental import pallas as pl
from jax.experimental.pallas import tpu as pltpu

BF16 = jnp.bfloat16
F32 = jnp.float32
I32 = jnp.int32

D_MODEL = 2048
EPS = 1e-6
CHUNK = 64
GROUP_A = 128
N_GROUPS_A = 4
WIDTH_A = 512
N_HEADS_B = 6
DK_B = 64
DV_B = 128
GATE_RANK_B = 16
GATE_TEMP_B = 16.0
WIDTH_B = 768
N_HEADS_C = 6
N_KV_C = 2
HD_C = 128
N_IDX_HEADS = 8
D_IDX = 64
TOPK_MAX = 256
WIDTH_C = 768
N_BUCKETS = 32
MAX_DISTANCE = 128
D_FF = 4 * D_MODEL
QK_B = N_HEADS_B * DK_B
KV_C = N_KV_C * HD_C
QI_C = N_IDX_HEADS * D_IDX

LANES = 128
SUBLANES = 8
ONES_ROWS = 16
VMEM_LIMIT = 56 * 1024 * 1024

C_QB = 0
C_KB = 384
C_VB = 768
C_OG = 1536
C_QC = 2304
C_KC = 3072
C_VC = 3328
C_QIC = 3584
C_UA = 4096
C_VA = 4608
C_MISC = 5120
MISC_LRB = D_IDX
MISC_WIC = D_IDX + GATE_RANK_B
H_WIDTH = 5376

INT_MIN = -2147483648
INT_MAX = 2147483647
SNAP_EVERY = 8
BISECT_AFTER = 40
MAX_SEARCH_STEPS = 80
MASKED = -1e30
LOG2E = 1.4426950408889634
NEAR_SPAN = 128
FAR_KW = 512
PROMPT_TQ = 256
SOFTMAX_ROWS = 64
FOLD_WAYS = 8
MXU_PIECE = 128
GLA_SUB = 4
GMLP_ROWS = 256


def _pick(n, cands):
    for c in cands:
        if n % c == 0:
            return c
    raise ValueError(f"no tile for {n}")


def _cparams(sem):
    return pltpu.CompilerParams(dimension_semantics=sem, vmem_limit_bytes=VMEM_LIMIT)


def _rms(x):
    return x * lax.rsqrt(jnp.mean(x * x, axis=-1, keepdims=True) + EPS)


def _dot(a, b):
    return jnp.dot(a, b, preferred_element_type=F32)


def _dot_nt(a, b):
    return lax.dot_general(a, b, (((1,), (1,)), ((), ())), preferred_element_type=F32)


def _dot_tn(a, b):
    return lax.dot_general(a, b, (((0,), (0,)), ((), ())), preferred_element_type=F32)


def _inproj_kernel(x_ref, g_ref, w_ref, o_ref, xn_ref):
    @pl.when(pl.program_id(1) == 0)
    def _():
        xn_ref[...] = (_rms(x_ref[...]) * g_ref[...]).astype(BF16)

    o_ref[...] = _dot_nt(xn_ref[...], w_ref[...])


def _inproj(x, g, w, l):
    m = x.shape[0]
    tm = _pick(m, (1024, 512, 256, 128))
    tn = 896
    return pl.pallas_call(
        _inproj_kernel,
        grid=(m // tm, H_WIDTH // tn),
        in_specs=[pl.BlockSpec((tm, D_MODEL), lambda i, j: (i, 0)),
                  pl.BlockSpec((None, 1, D_MODEL), lambda i, j: (l, 0, 0)),
                  pl.BlockSpec((None, tn, D_MODEL), lambda i, j: (l, j, 0))],
        out_specs=pl.BlockSpec((tm, tn), lambda i, j: (i, j)),
        out_shape=jax.ShapeDtypeStruct((m, H_WIDTH), F32),
        scratch_shapes=[pltpu.VMEM((tm, D_MODEL), BF16)],
        compiler_params=_cparams(("arbitrary", "arbitrary")),
        name="inproj",
    )(x, g, w)


def _gmlp_kernel(h_ref, gv_ref, w_ref, b_ref, ya_ref, vn_ref, *, n_prompt_blocks):
    i = pl.program_id(0)
    hv = h_ref[...]
    u = jax.nn.gelu(hv[:, :WIDTH_A])
    v = jax.nn.gelu(hv[:, WIDTH_A:])
    r = lax.broadcasted_iota(I32, (GMLP_ROWS, GMLP_ROWS), 0)
    c = lax.broadcasted_iota(I32, (GMLP_ROWS, GMLP_ROWS), 1)
    shift = 7 - (i >= n_prompt_blocks).astype(I32)
    keep = (c <= r) & (jnp.right_shift(r, shift) == jnp.right_shift(c, shift))
    for g in range(N_GROUPS_A):
        sl = slice(g * GROUP_A, (g + 1) * GROUP_A)
        vn = _rms(v[:, sl]) * gv_ref[:, sl]
        vn_ref[:, sl] = vn
        wm = jnp.where(keep, w_ref[0, g], 0.0).astype(BF16)
        z = _dot(wm, vn.astype(BF16)) + b_ref[0][:, g:g + 1]
        ya_ref[:, sl] = (u[:, sl] * z).astype(BF16)


def _gmlp(h, gv, w2, b2, n_prompt_rows, l):
    m = h.shape[0]
    rows = GMLP_ROWS
    npb = n_prompt_rows // rows
    nsb = (m - n_prompt_rows) // rows
    return pl.pallas_call(
        functools.partial(_gmlp_kernel, n_prompt_blocks=npb),
        grid=(npb + nsb,),
        in_specs=[pl.BlockSpec((rows, 2 * WIDTH_A), lambda i: (i, C_UA // (2 * WIDTH_A))),
                  pl.BlockSpec((None, 1, WIDTH_A), lambda i: (l, 0, 0)),
                  pl.BlockSpec((None, 1, N_GROUPS_A, rows, rows),
                               lambda i: (l, jnp.where(i >= npb, 1, 0), 0, 0, 0)),
                  pl.BlockSpec((None, 1, rows, N_GROUPS_A),
                               lambda i: (l, jnp.where(i >= npb, 1, 0), 0, 0))],
        out_specs=[pl.BlockSpec((rows, WIDTH_A), lambda i: (i, 0)),
                   pl.BlockSpec((rows, WIDTH_A), lambda i: (jnp.maximum(i - npb, 0), 0))],
        out_shape=[jax.ShapeDtypeStruct((m, WIDTH_A), BF16),
                   jax.ShapeDtypeStruct((nsb * rows, WIDTH_A), F32)],
        compiler_params=_cparams(("arbitrary",)),
        name="gmlp",
    )(h, gv, w2, b2)


def _gla_kernel(q_ref, k_ref, v_ref, og_ref, misc_ref, wg_ref, bg_ref, gbo_ref, s0_ref,
                yb_ref, sout_ref, st_ref, *, n_prompt_steps):
    j = pl.program_id(0)
    rows = GLA_SUB * CHUNK
    fresh = j >= n_prompt_steps

    lrb = misc_ref[:, MISC_LRB:MISC_LRB + GATE_RANK_B].astype(BF16)
    x = _dot(lrb, wg_ref[...]) + bg_ref[...]
    g = (jnp.minimum(x, 0.0) - jnp.log1p(jnp.exp(-jnp.abs(x)))) * (1.0 / GATE_TEMP_B)
    rr = lax.broadcasted_iota(I32, (rows, rows), 0)
    cc = lax.broadcasted_iota(I32, (rows, rows), 1)
    same = (rr >> 6) == (cc >> 6)
    tril = same & (cc <= rr)
    g1 = g.astype(BF16)
    r1 = g - g1.astype(F32)
    g2 = r1.astype(BF16)
    g3 = (r1 - g2.astype(F32)).astype(BF16)

    tri = jnp.where(tril, 1.0, 0.0).astype(BF16)
    b = _dot(tri, g1) + _dot(tri, g2) + _dot(tri, g3)

    def chunk_row(r):
        return jnp.concatenate([jnp.broadcast_to(b[c * CHUNK + r:c * CHUNK + r + 1], (CHUNK, QK_B))
                                for c in range(GLA_SUB)], axis=0)

    b_last = chunk_row(CHUNK - 1)
    b_mid = chunk_row(CHUNK // 2 - 1)
    qs = q_ref[...] * (DK_B ** -0.5)
    kk = k_ref[...]
    q_inter = (qs * jnp.exp(b)).astype(BF16)
    q_intra = (qs * jnp.exp(b - b_mid)).astype(BF16)
    k_intra = (kk * jnp.exp(b_mid - b)).astype(BF16)
    k_state = (kk * jnp.exp(b_last - b)).astype(BF16)
    dec = jnp.exp(b_last)
    for h in range(N_HEADS_B):
        sk = slice(h * DK_B, (h + 1) * DK_B)
        sv = slice(h * DV_B, (h + 1) * DV_B)
        vh = v_ref[:, sv].astype(BF16)
        att = jnp.where(tril, _dot_nt(q_intra[:, sk], k_intra[:, sk]), 0.0)
        o_intra = _dot(att.astype(BF16), vh)
        st = st_ref[h]
        o_inter = []
        for c in range(GLA_SUB):
            rc = slice(c * CHUNK, (c + 1) * CHUNK)
            start = fresh | ((j == 0) & (c == 0))
            st = jnp.where(start, s0_ref[c, h].T, st)
            o_inter.append(_dot_nt(q_inter[rc, sk], st.astype(BF16)))
            st = st * dec[c * CHUNK:c * CHUNK + 1, sk] + _dot_tn(vh[rc], k_state[rc, sk])
            sout_ref[c, h] = st.T
        st_ref[h] = st
        o = jnp.concatenate(o_inter, axis=0) + o_intra
        on = _rms(o) * gbo_ref[:, sv]
        yb_ref[:, sv] = (on * jax.nn.silu(og_ref[:, sv])).astype(BF16)


def _gla(h, wg, bg, gbo, s_in, n_prompt_rows, l):
    m = h.shape[0]
    rows = GLA_SUB * CHUNK
    nps = n_prompt_rows // rows
    nslots = s_in.shape[1]

    def blk(j):
        return jnp.where(j < nps, 0, j - nps + 1)

    return pl.pallas_call(
        functools.partial(_gla_kernel, n_prompt_steps=nps),
        grid=(m // rows,),
        in_specs=[pl.BlockSpec((rows, QK_B), lambda j: (j, C_QB // QK_B)),
                  pl.BlockSpec((rows, QK_B), lambda j: (j, C_KB // QK_B)),
                  pl.BlockSpec((rows, WIDTH_B), lambda j: (j, C_VB // WIDTH_B)),
                  pl.BlockSpec((rows, WIDTH_B), lambda j: (j, C_OG // WIDTH_B)),
                  pl.BlockSpec((rows, LANES), lambda j: (j, C_MISC // LANES)),
                  pl.BlockSpec((None, GATE_RANK_B, QK_B), lambda j: (l, 0, 0)),
                  pl.BlockSpec((None, 1, QK_B), lambda j: (l, 0, 0)),
                  pl.BlockSpec((None, 1, WIDTH_B), lambda j: (l, 0, 0)),
                  pl.BlockSpec((None, GLA_SUB, N_HEADS_B, DK_B, DV_B), lambda j: (l, blk(j), 0, 0, 0))],
        out_specs=[pl.BlockSpec((rows, WIDTH_B), lambda j: (j, 0)),
                   pl.BlockSpec((GLA_SUB, N_HEADS_B, DK_B, DV_B), lambda j: (blk(j), 0, 0, 0))],
        out_shape=[jax.ShapeDtypeStruct((m, WIDTH_B), BF16),
                   jax.ShapeDtypeStruct((nslots, N_HEADS_B, DK_B, DV_B), F32)],
        scratch_shapes=[pltpu.VMEM((N_HEADS_B, DV_B, DK_B), F32)],
        compiler_params=_cparams(("arbitrary",)),
        name="gla",
    )(h, h, h, h, h, wg, bg, gbo, s_in)


def _fold_rows(x, op, each=lambda v: v):
    tile = SUBLANES * (4 // x.dtype.itemsize)
    n_tiles = x.shape[0] // tile
    parts = [each(x[j * tile:(j + 1) * tile]) for j in range(min(FOLD_WAYS, n_tiles))]
    for j in range(FOLD_WAYS, n_tiles):
        parts[j % FOLD_WAYS] = op(parts[j % FOLD_WAYS], each(x[j * tile:(j + 1) * tile]))
    while len(parts) > 1:
        nxt = [op(parts[j], parts[j + 1]) for j in range(0, len(parts) - 1, 2)]
        parts = nxt + parts[len(parts) - len(parts) % 2:]
    return parts[0]


def _loop(n, body, init):
    if isinstance(n, int):
        carry = init
        for c in range(n):
            carry = body(c, carry)
        return carry
    return lax.fori_loop(0, n, body, init)


REP_C = N_HEADS_C // N_KV_C
KV_SLICES = [slice(g * HD_C, (g + 1) * HD_C) for g in range(N_KV_C)]


def _scaled_queries(q, qi, misc_t):
    qi_s = (qi * (D_IDX ** -0.5)).astype(BF16)
    wi_t = misc_t[MISC_WIC:MISC_WIC + N_IDX_HEADS, :] * (N_IDX_HEADS ** -0.5)
    qs = (q * (HD_C ** -0.5 * LOG2E)).astype(BF16)
    return ([qi_s[:, h * D_IDX:(h + 1) * D_IDX] for h in range(N_IDX_HEADS)], wi_t,
            [qs[:, hh * HD_C:(hh + 1) * HD_C] for hh in range(N_HEADS_C)])


def _dsa_core(qi_all, wi_t, q_g, far_ki, far_k, far_vx, nf, far_adm, near_ki, near_k, near_vx, near_adm, n_adm,
              bias_near_ref, emit, scratch, *, tq, kw, nw, topk):
    fkeys_ref, nkeys_ref, m_ref, acc_ref, ss_ref, p_ref = scratch
    rep = REP_C
    groups = range(N_KV_C)
    kf = float(topk)

    def extremes(key, carry):
        kmax, kmin = carry
        kmax = jnp.maximum(kmax, _fold_rows(key, jnp.maximum))
        kmin = jnp.minimum(kmin, _fold_rows(key, jnp.minimum, lambda v: jnp.where(v == INT_MIN, INT_MAX, v)))
        return kmax, kmin

    def fill_keys(ki_blk, adm, put, carry):
        n = ki_blk.shape[0]

        def dots(p0):
            return _dot_nt(ki_blk[p0:min(p0 + MXU_PIECE, n)], qi_all)

        starts = list(range(0, n, MXU_PIECE))
        d_next = dots(starts[0])
        for i, p0 in enumerate(starts):
            d = d_next
            if i + 1 < len(starts):
                d_next = dots(starts[i + 1])
            for r0 in range(0, d.shape[0], SOFTMAX_ROWS):
                acc = None
                for h in range(N_IDX_HEADS):
                    t = wi_t[h:h + 1, :] * jnp.maximum(d[r0:r0 + SOFTMAX_ROWS, h * tq:(h + 1) * tq], 0.0)
                    acc = t if acc is None else acc + t
                bits = lax.bitcast_convert_type(acc, I32)
                key = bits ^ ((bits >> 31) & 0x7FFFFFFF)
                key = jnp.where(acc == 0.0, 0, key)
                rows = slice(p0 + r0, p0 + r0 + SOFTMAX_ROWS)
                if adm is not None:
                    key = jnp.where(adm(rows), key, INT_MIN)
                put(rows, key)
                carry = extremes(key, carry)
        return carry

    def fill(c, carry):
        def put(rows, key):
            fkeys_ref[c, rows, :] = key

        return fill_keys(far_ki(c), None if far_adm is None else far_adm(c), put, carry)

    def put_near(rows, key):
        nkeys_ref[rows, :] = key

    carry = _loop(nf, fill, (jnp.full((SUBLANES, tq), INT_MIN, I32), jnp.full((SUBLANES, tq), INT_MAX, I32)))
    kmax, kmin = fill_keys(near_ki, near_adm, put_near, carry)
    kmax = jnp.max(kmax, axis=0, keepdims=True)
    kmin = jnp.min(kmin, axis=0, keepdims=True)

    def count(thr, strict):
        def cmp(x):
            return _fold_rows(x, jnp.add, lambda v: jnp.where((v > thr) if strict else (v >= thr), 1.0, 0.0))

        part = _loop(nf, lambda c, a: a + cmp(fkeys_ref[c]), jnp.zeros((SUBLANES, tq), F32))
        part = part + cmp(nkeys_ref[...])
        return jnp.sum(part, axis=0, keepdims=True)

    def bits_flip(x):
        return x ^ ((x >> 31) & 0x7FFFFFFF)

    def any_lane(flag):
        return jnp.max(jnp.where(flag, 1.0, 0.0))

    def snap(lo, hi):
        def pull(x, carry):
            kin, kax = carry
            kin = jnp.minimum(kin, _fold_rows(x, jnp.minimum, lambda v: jnp.where(v >= lo, v, INT_MAX)))
            kax = jnp.maximum(kax, _fold_rows(x, jnp.maximum, lambda v: jnp.where(v < hi, v, INT_MIN)))
            return kin, kax

        init = (jnp.full((SUBLANES, tq), INT_MAX, I32), jnp.full((SUBLANES, tq), INT_MIN, I32))
        kin, kax = pull(nkeys_ref[...], _loop(nf, lambda c, carry: pull(fkeys_ref[c], carry), init))
        return jnp.min(kin, axis=0, keepdims=True), jnp.max(kax, axis=0, keepdims=True)

    enough = n_adm >= kf
    log_k = math.log2(topk)

    def open_lanes(lo, hi, flo):
        return enough & (flo > kf) & (hi > lo + 1)

    def log_excess(c):
        return jnp.log2(jnp.maximum(c, 0.5)) - log_k

    def step(state):
        it, _, lo, hi, flo, fhi, glo, ghi, side = state
        active = open_lanes(lo, hi, flo)

        def pull_in(_):
            kin, kax = snap(lo, hi)
            return jnp.where(active, kin, lo), jnp.where(active, kax + 1, hi), flo, fhi, glo, ghi, side

        def probe(_):
            v_lo = lax.bitcast_convert_type(bits_flip(lo), F32)
            v_hi = lax.bitcast_convert_type(bits_flip(hi), F32)
            frac = jnp.minimum(jnp.maximum(glo / (glo - ghi), 0.02), 0.98)
            t_int = bits_flip(lax.bitcast_convert_type(v_lo + (v_hi - v_lo) * frac, I32))
            t_mid = (lo & hi) + ((lo ^ hi) >> 1)
            t = jnp.where(it >= BISECT_AFTER, t_mid, t_int)
            t = jnp.minimum(jnp.maximum(t, lo + 1), hi - 1)
            c = count(t, False)
            g = log_excess(c)
            up = active & (c >= kf)
            dn = active & (c < kf)
            ghi2 = jnp.where(up & (side > 0.0), ghi * 0.5, ghi)
            glo2 = jnp.where(dn & (side < 0.0), glo * 0.5, glo)
            return (jnp.where(up, t, lo), jnp.where(dn, t, hi), jnp.where(up, c, flo), jnp.where(dn, c, fhi),
                    jnp.where(up, g, glo2), jnp.where(dn, g, ghi2), jnp.where(up, 1.0, jnp.where(dn, -1.0, side)))

        is_snap = (it % SNAP_EVERY == SNAP_EVERY - 1) & (it < BISECT_AFTER)
        lo, hi, flo, fhi, glo, ghi, side = lax.cond(is_snap, pull_in, probe, 0)
        return it + 1, any_lane(open_lanes(lo, hi, flo)), lo, hi, flo, fhi, glo, ghi, side

    hi0 = kmax + 1
    zero = jnp.zeros((1, tq), F32)
    state = (jnp.int32(0), any_lane(open_lanes(kmin, hi0, n_adm)), kmin, hi0, n_adm, zero,
             log_excess(n_adm), log_excess(zero), zero)
    state = lax.while_loop(lambda st: (st[1] > 0.0) & (st[0] < MAX_SEARCH_STEPS), step, state)
    thr = jnp.where(enough, state[2], INT_MIN)
    has_tie = any_lane(enough & (state[4] > kf)) > 0.0

    @pl.when(has_tie)
    def _():
        need = kf - count(thr, True)

        def lower(n):
            a = lax.broadcasted_iota(I32, (n, n), 0)
            b = lax.broadcasted_iota(I32, (n, n), 1)
            return jnp.where(b < a, 1.0, 0.0).astype(BF16)

        def demote(keys, run, lt):
            eq = (keys == thr) & (keys > INT_MIN)
            eqf = jnp.where(eq, 1.0, 0.0)
            before = _dot(lt, eqf.astype(BF16)) + run
            keys = jnp.where(eq & (before >= need), INT_MIN, keys)
            return keys, run + jnp.sum(eqf, axis=0, keepdims=True)

        lt_far = lower(kw)

        def step(c, run):
            keys, run = demote(fkeys_ref[c], run, lt_far)
            fkeys_ref[c] = keys
            return run

        run = _loop(nf, step, jnp.zeros((1, tq), F32))
        keys, _ = demote(nkeys_ref[...], run, lower(nw))
        nkeys_ref[...] = keys

    thr_sel = jnp.maximum(thr, INT_MIN + 1)
    m_ref[...] = jnp.full(m_ref.shape, MASKED, F32)
    acc_ref[...] = jnp.zeros(acc_ref.shape, F32)

    def logits(k_blk):
        return [_dot_nt(k_blk[g], q_g[g]) for g in groups]

    def attend(slot, keys_rows, vx_blk, bias_ref, n):
        tiles = [slice(r0, r0 + SOFTMAX_ROWS) for r0 in range(0, n, SOFTMAX_ROWS)]
        heads = [slice(r * tq, (r + 1) * tq) for r in range(rep)]

        run = [None] * N_KV_C
        for rows in tiles:
            selb = jnp.where(keys_rows(rows) >= thr_sel, 0.0, MASKED)
            for g in groups:
                s = ss_ref[slot, g, rows, :]
                if bias_ref is not None:
                    s = s + bias_ref[g, rows, :]
                tops = []
                for hd in heads:
                    sm = s[:, hd] + selb
                    ss_ref[slot, g, rows, hd] = sm
                    tops.append(_fold_rows(sm, jnp.maximum))
                top = jnp.concatenate(tops, axis=1)
                run[g] = top if run[g] is None else jnp.maximum(run[g], top)
        m_new = []
        for g in groups:
            m_old = m_ref[g]
            m_new.append(jnp.maximum(m_old, jnp.max(run[g], axis=0, keepdims=True)))
            acc_ref[g] = jnp.exp2(m_old - m_new[g]) * acc_ref[g]
            m_ref[g] = m_new[g]
        for rows in tiles:
            for g in groups:
                p_ref[g, rows, :] = jnp.exp2(ss_ref[slot, g, rows, :] - m_new[g]).astype(BF16)
        for g in groups:
            acc_ref[g] += _dot(vx_blk[g], p_ref[g, :n, :])

    last = fkeys_ref.shape[0] - 1

    def put_logits(slot, k_blk, n):
        for g, s in enumerate(logits(k_blk)):
            ss_ref[slot, g, :n, :] = s

    def far_body(c, cur):
        put_logits(1 - cur, far_k(min(c + 1, last) if isinstance(c, int) else jnp.minimum(c + 1, last)), kw)
        attend(cur, lambda rows: fkeys_ref[c, rows, :], far_vx(c), None, kw)

    def far_step(c, carry):
        if isinstance(c, int):
            far_body(c, c % 2)
        else:
            for cur in range(2):
                pl.when(c % 2 == cur)(functools.partial(far_body, c, cur))
        return carry

    put_logits(0, far_k(0), kw)
    _loop(nf, far_step, 0)
    put_logits(0, near_k, nw)
    attend(0, lambda rows: nkeys_ref[rows, :], near_vx, bias_near_ref, nw)

    for g in groups:
        emit(g, acc_ref[g])


def _with_ones(vt):
    return jnp.concatenate([vt, jnp.ones((ONES_ROWS, vt.shape[1]), BF16)], axis=0)


def _dsa_scratch(nf, tq, kw, nw, vx_rows):
    rep = REP_C
    return [pltpu.VMEM((nf, kw, tq), I32),
            pltpu.VMEM((nw, tq), I32),
            pltpu.VMEM((N_KV_C, 1, rep * tq), F32),
            pltpu.VMEM((N_KV_C, vx_rows, rep * tq), F32),
            pltpu.VMEM((2, N_KV_C, max(kw, nw), rep * tq), F32),
            pltpu.VMEM((N_KV_C, max(kw, nw), rep * tq), BF16)]


def _dsa_prompt_kernel(q_ref, qi_ref, misc_ref, k_ref, vt_far_ref, vt_near_ref, ki_ref, bias_near_ref,
                       out_ref, *scratch, tq, topk):
    i = pl.program_id(0)
    kw = FAR_KW
    nw = NEAR_SPAN + tq
    q0 = i * tq
    far_limit = q0 - NEAR_SPAN
    near0 = jnp.maximum(far_limit, 0)
    nf = (near0 + kw - 1) // kw

    def rows(c):
        return pl.ds(pl.multiple_of(c * kw, kw), kw)

    def far_adm(c):
        def rows_ok(rows):
            n = rows.stop - rows.start
            return (c * kw + rows.start + lax.broadcasted_iota(I32, (n, tq), 0)) < far_limit

        return rows_ok

    left = pl.ds(pl.multiple_of(near0, NEAR_SPAN), NEAR_SPAN)
    right = pl.ds(pl.multiple_of(q0, tq), tq)
    near_ki = jnp.concatenate([ki_ref[left, :], ki_ref[right, :]], axis=0)
    near_k = jnp.concatenate([k_ref[left, :], k_ref[right, :]], axis=0)
    per = tq // NEAR_SPAN
    near_vt = jnp.concatenate([vt_near_ref[near0 // NEAR_SPAN]] + [vt_near_ref[i * per + j] for j in range(per)],
                              axis=1)
    first_key = jnp.where(i > 0, 0, NEAR_SPAN)

    def near_adm(rows):
        n = rows.stop - rows.start
        key = rows.start + lax.broadcasted_iota(I32, (n, tq), 0)
        qry = lax.broadcasted_iota(I32, (n, tq), 1)
        return (key >= first_key) & (((key - NEAR_SPAN) >> 6) <= (qry >> 6))

    n_adm = ((((q0 + lax.broadcasted_iota(I32, (1, tq), 1)) >> 6) + 1) * CHUNK).astype(F32)

    qi_h, wi_t, q_h = _scaled_queries(q_ref[...], qi_ref[...], misc_ref[...].T)
    qi_all = jnp.concatenate(qi_h, axis=0)
    q_g = [jnp.concatenate(q_h[g * REP_C:(g + 1) * REP_C], axis=0) for g in range(N_KV_C)]

    def emit(g, acc):
        o = acc[:HD_C] / acc[HD_C:HD_C + 1]
        for r in range(REP_C):
            hh = g * REP_C + r
            out_ref[:, hh * HD_C:(hh + 1) * HD_C] = o[:, r * tq:(r + 1) * tq].T.astype(BF16)

    _dsa_core(qi_all, wi_t, q_g,
              lambda c: ki_ref[rows(c), :], lambda c: [k_ref[rows(c), sl] for sl in KV_SLICES],
              lambda c: [_with_ones(vt_far_ref[c, sl, :]) for sl in KV_SLICES],
              nf, far_adm, near_ki, [near_k[:, sl] for sl in KV_SLICES],
              [_with_ones(near_vt[sl, :]) for sl in KV_SLICES], near_adm, n_adm,
              bias_near_ref, emit, scratch,
              tq=tq, kw=kw, nw=nw, topk=topk)


def _dsa_prompt(h, kb, vt_far, vt_near, kib, bias_near, t):
    tq = PROMPT_TQ
    topk = min(TOPK_MAX, t // 4)
    return pl.pallas_call(
        functools.partial(_dsa_prompt_kernel, tq=tq, topk=topk),
        grid=(t // tq,),
        in_specs=[pl.BlockSpec((tq, WIDTH_C), lambda i: (i, C_QC // WIDTH_C)),
                  pl.BlockSpec((tq, QI_C), lambda i: (i, C_QIC // QI_C)),
                  pl.BlockSpec((tq, LANES), lambda i: (i, C_MISC // LANES)),
                  pl.BlockSpec((t, KV_C), lambda i: (0, 0)),
                  pl.BlockSpec(vt_far.shape, lambda i: (0, 0, 0)),
                  pl.BlockSpec(vt_near.shape, lambda i: (0, 0, 0)),
                  pl.BlockSpec((t, D_IDX), lambda i: (0, 0)),
                  pl.BlockSpec(bias_near.shape, lambda i: (0, 0, 0))],
        out_specs=pl.BlockSpec((tq, WIDTH_C), lambda i: (i, 0)),
        out_shape=jax.ShapeDtypeStruct((t, WIDTH_C), BF16),
        scratch_shapes=_dsa_scratch(t // FAR_KW, tq, FAR_KW, NEAR_SPAN + tq, HD_C + ONES_ROWS),
        compiler_params=_cparams(("arbitrary",)),
        name="dsa_prompt",
    )(h, h, h, kb, vt_far, vt_near, kib, bias_near)


def _dsa_sample_kernel(q_ref, qi_ref, misc_ref, kvn_ref, pk_ref, pv_ref, pki_ref, bias_near_ref,
                       out_ref, *scratch, kw, nf, topk):
    tq = CHUNK
    far_len = kw * nf
    misc = misc_ref[...]
    kvn = kvn_ref[...]
    qi_h, wi_t, q_h = _scaled_queries(q_ref[...], qi_ref[...], misc.T)
    qi_all = jnp.concatenate(qi_h, axis=0)
    q_g = [jnp.concatenate(q_h[g * REP_C:(g + 1) * REP_C], axis=0) for g in range(N_KV_C)]

    def group_rows(ref, start, n, g):
        return ref[0, pl.ds(2 * start + g, n, stride=2), :]

    def far_ki(c):
        return pki_ref[0, :, pl.ds(c * kw, kw)].T.astype(BF16)

    def far_k(c):
        return [group_rows(pk_ref, c * kw, kw, g).astype(BF16) for g in range(N_KV_C)]

    def far_vx(c):
        return [_with_ones(group_rows(pv_ref, c * kw, kw, g).T.astype(BF16)) for g in range(N_KV_C)]

    near_ki = jnp.concatenate([pki_ref[0, :, pl.ds(far_len, NEAR_SPAN)].T, misc[:, :D_IDX]], axis=0).astype(BF16)
    near_k = [jnp.concatenate([group_rows(pk_ref, far_len, NEAR_SPAN, g), kvn[:, KV_SLICES[g]]],
                              axis=0).astype(BF16) for g in range(N_KV_C)]
    near_vx = [_with_ones(jnp.concatenate(
        [group_rows(pv_ref, far_len, NEAR_SPAN, g).T, kvn[:, KV_C + g * HD_C:KV_C + (g + 1) * HD_C].T],
        axis=1).astype(BF16)) for g in range(N_KV_C)]

    def emit(g, acc):
        o = acc[:HD_C] / acc[HD_C:HD_C + 1]
        for r in range(REP_C):
            hh = g * REP_C + r
            out_ref[:, hh * HD_C:(hh + 1) * HD_C] = o[:, r * tq:(r + 1) * tq].T.astype(BF16)

    _dsa_core(qi_all, wi_t, q_g, far_ki, far_k, far_vx, nf, None, near_ki, near_k, near_vx, None,
              jnp.full((1, tq), float(far_len + NEAR_SPAN + CHUNK), F32),
              bias_near_ref, emit, scratch,
              tq=tq, kw=kw, nw=NEAR_SPAN + CHUNK, topk=topk)


def _dsa_sample(h, pk, pv, pki, bias_near, t, l):
    nb, p = pki.shape[1], pki.shape[3]
    tq = CHUNK
    far_len = p - NEAR_SPAN
    kw = _pick(far_len, (640, 512, 384, 256, 128))
    nf = far_len // kw
    nw = NEAR_SPAN + CHUNK
    topk = min(TOPK_MAX, (p + CHUNK) // 4)
    row0 = t // tq
    return pl.pallas_call(
        functools.partial(_dsa_sample_kernel, kw=kw, nf=nf, topk=topk),
        grid=(nb,),
        in_specs=[pl.BlockSpec((tq, WIDTH_C), lambda b: (row0 + b, C_QC // WIDTH_C)),
                  pl.BlockSpec((tq, QI_C), lambda b: (row0 + b, C_QIC // QI_C)),
                  pl.BlockSpec((tq, LANES), lambda b: (row0 + b, C_MISC // LANES)),
                  pl.BlockSpec((tq, 2 * KV_C), lambda b: (row0 + b, C_KC // (2 * KV_C))),
                  pl.BlockSpec((None, 1, N_KV_C * p, HD_C), lambda b: (l, b, 0, 0)),
                  pl.BlockSpec((None, 1, N_KV_C * p, HD_C), lambda b: (l, b, 0, 0)),
                  pl.BlockSpec((None, 1, D_IDX, p), lambda b: (l, b, 0, 0)),
                  pl.BlockSpec(bias_near.shape, lambda b: (0, 0, 0))],
        out_specs=pl.BlockSpec((tq, WIDTH_C), lambda b: (b, 0)),
        out_shape=jax.ShapeDtypeStruct((nb * tq, WIDTH_C), BF16),
        scratch_shapes=_dsa_scratch(nf, tq, kw, nw, HD_C + ONES_ROWS),
        compiler_params=_cparams(("arbitrary",)),
        name="dsa_sample",
    )(h, h, h, h, pk, pv, pki, bias_near)


def _outproj_kernel(x_ref, ya_ref, yb_ref, ycp_ref, ycs_ref, wa_ref, wb_ref, wc_ref, o_ref, *, n_prompt_blocks):
    yc = jnp.where(pl.program_id(0) < n_prompt_blocks, ycp_ref[...], ycs_ref[...])
    o_ref[...] = (x_ref[...] + _dot(ya_ref[...], wa_ref[...]) + _dot(yb_ref[...], wb_ref[...])
                  + _dot(yc, wc_ref[...]))


def _outproj(x, ya, yb, yc_p, yc_s, wa, wb, wc, l):
    m = x.shape[0]
    tm = _pick(yc_s.shape[0], (1024, 512, 256, 128))
    tn = 1024
    npb = yc_p.shape[0] // tm
    assert yc_p.shape[0] % tm == 0
    return pl.pallas_call(
        functools.partial(_outproj_kernel, n_prompt_blocks=npb),
        grid=(m // tm, D_MODEL // tn),
        in_specs=[pl.BlockSpec((tm, tn), lambda i, j: (i, j)),
                  pl.BlockSpec((tm, WIDTH_A), lambda i, j: (i, 0)),
                  pl.BlockSpec((tm, WIDTH_B), lambda i, j: (i, 0)),
                  pl.BlockSpec((tm, WIDTH_C), lambda i, j: (jnp.minimum(i, npb - 1), 0)),
                  pl.BlockSpec((tm, WIDTH_C), lambda i, j: (jnp.maximum(i - npb, 0), 0)),
                  pl.BlockSpec((None, WIDTH_A, tn), lambda i, j: (l, 0, j)),
                  pl.BlockSpec((None, WIDTH_B, tn), lambda i, j: (l, 0, j)),
                  pl.BlockSpec((None, WIDTH_C, tn), lambda i, j: (l, 0, j))],
        out_specs=pl.BlockSpec((tm, tn), lambda i, j: (i, j)),
        out_shape=jax.ShapeDtypeStruct((m, D_MODEL), F32),
        compiler_params=_cparams(("arbitrary", "arbitrary")),
        name="outproj",
    )(x, ya, yb, yc_p, yc_s, wa, wb, wc)


def _ffn_kernel(x_ref, g_ref, wu_ref, wd_ref, o_ref, xn_ref, acc_ref):
    f = pl.program_id(1)

    @pl.when(f == 0)
    def _():
        xn_ref[...] = (_rms(x_ref[...]) * g_ref[...]).astype(BF16)
        acc_ref[...] = jnp.zeros(acc_ref.shape, F32)

    a = jnp.maximum(_dot(xn_ref[...], wu_ref[...]), 0.0)
    acc_ref[...] += _dot((a * a).astype(BF16), wd_ref[...])

    @pl.when(f == pl.num_programs(1) - 1)
    def _():
        o_ref[...] = x_ref[...] + acc_ref[...]


def _ffn(x, g, wu, wd, l):
    m = x.shape[0]
    tm = _pick(m, (512, 256, 128))
    tf = 1024
    return pl.pallas_call(
        _ffn_kernel,
        grid=(m // tm, D_FF // tf),
        in_specs=[pl.BlockSpec((tm, D_MODEL), lambda i, f: (i, 0)),
                  pl.BlockSpec((None, 1, D_MODEL), lambda i, f: (l, 0, 0)),
                  pl.BlockSpec((None, D_MODEL, tf), lambda i, f: (l, 0, f)),
                  pl.BlockSpec((None, tf, D_MODEL), lambda i, f: (l, f, 0))],
        out_specs=pl.BlockSpec((tm, D_MODEL), lambda i, f: (i, 0)),
        out_shape=jax.ShapeDtypeStruct((m, D_MODEL), F32),
        scratch_shapes=[pltpu.VMEM((tm, D_MODEL), BF16), pltpu.VMEM((tm, D_MODEL), F32)],
        compiler_params=_cparams(("arbitrary", "arbitrary")),
        name="ffn",
    )(x, g, wu, wd)


def _norm_kernel(x_ref, g_ref, o_ref):
    o_ref[...] = _rms(x_ref[...]) * g_ref[...]


def _final_norm(x, g, row0, rows):
    tm = _pick(rows, (1024, 512, 256, 128))
    assert row0 % tm == 0
    return pl.pallas_call(
        _norm_kernel,
        grid=(rows // tm,),
        in_specs=[pl.BlockSpec((tm, D_MODEL), lambda i: (row0 // tm + i, 0)),
                  pl.BlockSpec((1, D_MODEL), lambda i: (0, 0))],
        out_specs=pl.BlockSpec((tm, D_MODEL), lambda i: (i, 0)),
        out_shape=jax.ShapeDtypeStruct((rows, D_MODEL), F32),
        compiler_params=_cparams(("arbitrary",)),
        name="final_norm",
    )(x, g)


def _t5_bucket(rel):
    half = N_BUCKETS // 2
    max_exact = half // 2
    n = jnp.abs(rel)
    nf = jnp.maximum(n, 1).astype(F32)
    large = max_exact + (jnp.log(nf / max_exact) / jnp.log(MAX_DISTANCE / max_exact)
                         * (half - max_exact)).astype(I32)
    large = jnp.minimum(large, half - 1)
    return jnp.where(rel > 0, half, 0) + jnp.where(n < max_exact, n, large)


def _bias_table(rel_bias, tq, nw):
    j = jnp.arange(nw, dtype=I32)[:, None]
    t = jnp.arange(tq, dtype=I32)[None, :]
    hot = jax.nn.one_hot(_t5_bucket(j - NEAR_SPAN - t), N_BUCKETS, dtype=F32)
    near = jnp.einsum("jtb,bh->jth", hot, rel_bias, precision=lax.Precision.HIGHEST)
    far = rel_bias[_t5_bucket(jnp.int32(-(NEAR_SPAN + 1)))]
    near = ((near - far) * LOG2E).astype(F32).reshape(nw, tq, N_KV_C, REP_C)
    return jnp.transpose(near, (2, 0, 3, 1)).reshape(N_KV_C, nw, REP_C * tq)


def _permute_w_in(w):
    wt = jnp.swapaxes(w, 1, 2)
    sizes = (WIDTH_A, WIDTH_A, QK_B, QK_B, WIDTH_B, GATE_RANK_B, WIDTH_B, WIDTH_C, KV_C, KV_C, QI_C,
             D_IDX, N_IDX_HEADS)
    parts, off = [], 0
    for n in sizes:
        parts.append(wt[:, off:off + n])
        off += n
    ua, va, qb, kb, vb, lrb, og, qc, kc, vc, qic, kic, wic = parts
    pad = jnp.zeros((w.shape[0], H_WIDTH - (C_MISC + D_IDX + GATE_RANK_B + N_IDX_HEADS), w.shape[1]), w.dtype)
    return jnp.concatenate([qb, kb, vb, og, qc, kc, vc, qic, ua, va, kic, lrb, wic, pad], axis=1).astype(BF16)


def kernel(x_prompt, x_sample, cache_c_k, cache_c_v, cache_c_kidx, state_b_s, norm_mix, w_in, norm_a_v,
           w_s_a, b_s_a, w_gate_b, b_gate_b, norm_b_o, rel_bias, w_o, norm_ffn, w_up, w_down, norm_final):
    depth = w_in.shape[0]
    nbp, t, _ = x_prompt.shape
    nbs, ts, _ = x_sample.shape
    p = cache_c_k.shape[2]
    assert nbp == 1 and ts == CHUNK and t % FAR_KW == 0 and t % GMLP_ROWS == 0 and (nbs * ts) % GMLP_ROWS == 0
    assert p % NEAR_SPAN == 0 and p >= 2 * NEAR_SPAN
    assert t % PROMPT_TQ == 0 and t % (GLA_SUB * CHUNK) == 0 and nbs % GLA_SUB == 0

    x = jnp.concatenate([x_prompt[0], x_sample.reshape(nbs * ts, D_MODEL)], axis=0)
    w_in_p = _permute_w_in(w_in)
    wo_a = w_o[:, :WIDTH_A].astype(BF16)
    wo_b = w_o[:, WIDTH_A:WIDTH_A + WIDTH_B].astype(BF16)
    wo_c = w_o[:, WIDTH_A + WIDTH_B:].astype(BF16)
    w_up_b = w_up.astype(BF16)
    w_down_b = w_down.astype(BF16)
    w_gate = w_gate_b.astype(BF16)
    np_, ns_ = GMLP_ROWS // GROUP_A, GMLP_ROWS // CHUNK
    w_s2 = jnp.stack([jnp.tile(w_s_a, (1, 1, np_, np_)),
                      jnp.tile(w_s_a[:, :, :CHUNK, :CHUNK], (1, 1, ns_, ns_))], axis=1)
    b_s2 = jnp.stack([jnp.tile(b_s_a, (1, 1, np_)), jnp.tile(b_s_a[:, :, :CHUNK], (1, 1, ns_))], axis=1)
    b_s2 = jnp.swapaxes(b_s2, 2, 3)
    s_all = jnp.concatenate([jnp.zeros((depth, GLA_SUB) + state_b_s.shape[2:], F32), state_b_s], axis=1)
    bias_near = _bias_table(rel_bias, PROMPT_TQ, NEAR_SPAN + PROMPT_TQ)
    bias_near_s = _bias_table(rel_bias, CHUNK, NEAR_SPAN + CHUNK)
    pk_all = cache_c_k.reshape(depth, nbs, N_KV_C * p, HD_C)
    pv_all = cache_c_v.reshape(depth, nbs, N_KV_C * p, HD_C)
    pki_all = jnp.swapaxes(cache_c_kidx, 2, 3)

    row = lambda a: a[:, None, :]
    outs = [[] for _ in range(9)]
    for l in range(depth):
        h = _inproj(x, row(norm_mix), w_in_p, l)
        hp, hs = h[:t], h[t:]
        ya, vn = _gmlp(h, row(norm_a_v), w_s2, b_s2, t, l)
        yb, s_out = _gla(h, w_gate, row(b_gate_b), row(norm_b_o), s_all, t, l)
        kb = hp[:, C_KC:C_KC + KV_C].astype(BF16)
        vt = hp[:, C_VC:C_VC + KV_C].astype(BF16).T
        vt_far = jnp.swapaxes(vt.reshape(KV_C, t // FAR_KW, FAR_KW), 0, 1)
        vt_near = jnp.swapaxes(vt.reshape(KV_C, t // NEAR_SPAN, NEAR_SPAN), 0, 1)
        kib = hp[:, C_MISC:C_MISC + D_IDX].astype(BF16)
        yc_p = _dsa_prompt(h, kb, vt_far, vt_near, kib, bias_near, t)
        yc_s = _dsa_sample(h, pk_all, pv_all, pki_all, bias_near_s, t, l)
        x = _outproj(x, ya, yb, yc_p, yc_s, wo_a, wo_b, wo_c, l)
        x = _ffn(x, row(norm_ffn), w_up_b, w_down_b, l)

        outs[0].append(hp[:, C_KC:C_KC + KV_C].reshape(1, t, N_KV_C, HD_C))
        outs[1].append(hp[:, C_VC:C_VC + KV_C].reshape(1, t, N_KV_C, HD_C))
        outs[2].append(hp[:, C_MISC:C_MISC + D_IDX].reshape(1, t, D_IDX))
        outs[3].append(s_out[GLA_SUB - 1:GLA_SUB])
        outs[4].append(hs[:, C_KC:C_KC + KV_C].reshape(nbs, ts, N_KV_C, HD_C))
        outs[5].append(hs[:, C_VC:C_VC + KV_C].reshape(nbs, ts, N_KV_C, HD_C))
        outs[6].append(hs[:, C_MISC:C_MISC + D_IDX].reshape(nbs, ts, D_IDX))
        outs[7].append(s_out[GLA_SUB:])
        outs[8].append(vn.reshape(nbs, ts, WIDTH_A))

    y_prompt = _final_norm(x, norm_final[None], 0, t)
    y_sample = _final_norm(x, norm_final[None], t, nbs * ts)
    return (y_prompt[None], y_sample.reshape(nbs, ts, D_MODEL)) + tuple(jnp.stack(o) for o in outs)
```

```python
import functools
import math

import jax
import jax.numpy as jnp
from jax import lax
from jax.experimental import pallas as pl
from jax.experimental.pallas import tpu as pltpu

BF16 = jnp.bfloat16
F32 = jnp.float32
I32 = jnp.int32

D_MODEL = 2048
EPS = 1e-6
CHUNK = 64
GROUP_A = 128
N_GROUPS_A = 4
WIDTH_A = 512
N_HEADS_B = 6
DK_B = 64
DV_B = 128
GATE_RANK_B = 16
GATE_TEMP_B = 16.0
WIDTH_B = 768
N_HEADS_C = 6
N_KV_C = 2
HD_C = 128
N_IDX_HEADS = 8
D_IDX = 64
TOPK_MAX = 256
WIDTH_C = 768
N_BUCKETS = 32
MAX_DISTANCE = 128
D_FF = 4 * D_MODEL
QK_B = N_HEADS_B * DK_B
KV_C = N_KV_C * HD_C
QI_C = N_IDX_HEADS * D_IDX

LANES = 128
SUBLANES = 8
ONES_ROWS = 16
VMEM_LIMIT = 56 * 1024 * 1024

C_QB = 0
C_KB = 384
C_VB = 768
C_OG = 1536
C_QC = 2304
C_KC = 3072
C_VC = 3328
C_QIC = 3584
C_UA = 4096
C_VA = 4608
C_MISC = 5120
MISC_LRB = D_IDX
MISC_WIC = D_IDX + GATE_RANK_B
H_WIDTH = 5376

INT_MIN = -2147483648
INT_MAX = 2147483647
SNAP_EVERY = 8
BISECT_AFTER = 40
MAX_SEARCH_STEPS = 80
MASKED = -1e30
LOG2E = 1.4426950408889634
NEAR_SPAN = 128
FAR_KW = 512
PROMPT_TQ = 256
SOFTMAX_ROWS = 64
FOLD_WAYS = 8
MXU_PIECE = 128
GLA_SUB = 4
GMLP_ROWS = 256


def _pick(n, cands):
    for c in cands:
        if n % c == 0:
            return c
    raise ValueError(f"no tile for {n}")


def _cparams(sem):
    return pltpu.CompilerParams(dimension_semantics=sem, vmem_limit_bytes=VMEM_LIMIT)


def _rms(x):
    return x * lax.rsqrt(jnp.mean(x * x, axis=-1, keepdims=True) + EPS)


def _dot(a, b):
    return jnp.dot(a, b, preferred_element_type=F32)


def _dot_nt(a, b):
    return lax.dot_general(a, b, (((1,), (1,)), ((), ())), preferred_element_type=F32)


def _dot_tn(a, b):
    return lax.dot_general(a, b, (((0,), (0,)), ((), ())), preferred_element_type=F32)


def _inproj_kernel(x_ref, g_ref, w_ref, o_ref, xn_ref):
    @pl.when(pl.program_id(1) == 0)
    def _():
        xn_ref[...] = (_rms(x_ref[...]) * g_ref[...]).astype(BF16)

    o_ref[...] = _dot_nt(xn_ref[...], w_ref[...])


def _inproj(x, g, w, l):
    m = x.shape[0]
    tm = _pick(m, (1024, 512, 256, 128))
    tn = 768
    return pl.pallas_call(
        _inproj_kernel,
        grid=(m // tm, H_WIDTH // tn),
        in_specs=[pl.BlockSpec((tm, D_MODEL), lambda i, j: (i, 0)),
                  pl.BlockSpec((None, 1, D_MODEL), lambda i, j: (l, 0, 0)),
                  pl.BlockSpec((None, tn, D_MODEL), lambda i, j: (l, j, 0))],
        out_specs=pl.BlockSpec((tm, tn), lambda i, j: (i, j)),
        out_shape=jax.ShapeDtypeStruct((m, H_WIDTH), F32),
        scratch_shapes=[pltpu.VMEM((tm, D_MODEL), BF16)],
        compiler_params=_cparams(("arbitrary", "arbitrary")),
        name="inproj",
    )(x, g, w)


def _gmlp_kernel(h_ref, gv_ref, w_ref, b_ref, ya_ref, vn_ref, *, n_prompt_blocks):
    i = pl.program_id(0)
    hv = h_ref[...]
    u = jax.nn.gelu(hv[:, :WIDTH_A])
    v = jax.nn.gelu(hv[:, WIDTH_A:])
    r = lax.broadcasted_iota(I32, (GMLP_ROWS, GMLP_ROWS), 0)
    c = lax.broadcasted_iota(I32, (GMLP_ROWS, GMLP_ROWS), 1)
    shift = 7 - (i >= n_prompt_blocks).astype(I32)
    keep = (c <= r) & (jnp.right_shift(r, shift) == jnp.right_shift(c, shift))
    for g in range(N_GROUPS_A):
        sl = slice(g * GROUP_A, (g + 1) * GROUP_A)
        vn = _rms(v[:, sl]) * gv_ref[:, sl]
        vn_ref[:, sl] = vn
        wm = jnp.where(keep, w_ref[0, g], 0.0).astype(BF16)
        z = _dot(wm, vn.astype(BF16)) + b_ref[0][:, g:g + 1]
        ya_ref[:, sl] = (u[:, sl] * z).astype(BF16)


def _gmlp(h, gv, w2, b2, n_prompt_rows, l):
    m = h.shape[0]
    rows = GMLP_ROWS
    npb = n_prompt_rows // rows
    nsb = (m - n_prompt_rows) // rows
    return pl.pallas_call(
        functools.partial(_gmlp_kernel, n_prompt_blocks=npb),
        grid=(npb + nsb,),
        in_specs=[pl.BlockSpec((rows, 2 * WIDTH_A), lambda i: (i, C_UA // (2 * WIDTH_A))),
                  pl.BlockSpec((None, 1, WIDTH_A), lambda i: (l, 0, 0)),
                  pl.BlockSpec((None, 1, N_GROUPS_A, rows, rows),
                               lambda i: (l, jnp.where(i >= npb, 1, 0), 0, 0, 0)),
                  pl.BlockSpec((None, 1, rows, N_GROUPS_A),
                               lambda i: (l, jnp.where(i >= npb, 1, 0), 0, 0))],
        out_specs=[pl.BlockSpec((rows, WIDTH_A), lambda i: (i, 0)),
                   pl.BlockSpec((rows, WIDTH_A), lambda i: (jnp.maximum(i - npb, 0), 0))],
        out_shape=[jax.ShapeDtypeStruct((m, WIDTH_A), BF16),
                   jax.ShapeDtypeStruct((nsb * rows, WIDTH_A), F32)],
        compiler_params=_cparams(("arbitrary",)),
        name="gmlp",
    )(h, gv, w2, b2)


def _gla_kernel(q_ref, k_ref, v_ref, og_ref, misc_ref, wg_ref, bg_ref, gbo_ref, s0_ref,
                yb_ref, sout_ref, st_ref, *, n_prompt_steps):
    j = pl.program_id(0)
    rows = GLA_SUB * CHUNK
    fresh = j >= n_prompt_steps

    lrb = misc_ref[:, MISC_LRB:MISC_LRB + GATE_RANK_B].astype(BF16)
    x = _dot(lrb, wg_ref[...]) + bg_ref[...]
    g = (jnp.minimum(x, 0.0) - jnp.log1p(jnp.exp(-jnp.abs(x)))) * (1.0 / GATE_TEMP_B)
    rr = lax.broadcasted_iota(I32, (rows, rows), 0)
    cc = lax.broadcasted_iota(I32, (rows, rows), 1)
    same = (rr >> 6) == (cc >> 6)
    tril = same & (cc <= rr)
    g1 = g.astype(BF16)
    r1 = g - g1.astype(F32)
    g2 = r1.astype(BF16)
    g3 = (r1 - g2.astype(F32)).astype(BF16)

    tri = jnp.where(tril, 1.0, 0.0).astype(BF16)
    b = _dot(tri, g1) + _dot(tri, g2) + _dot(tri, g3)

    def chunk_row(r):
        return jnp.concatenate([jnp.broadcast_to(b[c * CHUNK + r:c * CHUNK + r + 1], (CHUNK, QK_B))
                                for c in range(GLA_SUB)], axis=0)

    b_last = chunk_row(CHUNK - 1)
    b_mid = chunk_row(CHUNK // 2 - 1)
    qs = q_ref[...] * (DK_B ** -0.5)
    kk = k_ref[...]
    q_inter = (qs * jnp.exp(b)).astype(BF16)
    q_intra = (qs * jnp.exp(b - b_mid)).astype(BF16)
    k_intra = (kk * jnp.exp(b_mid - b)).astype(BF16)
    k_state = (kk * jnp.exp(b_last - b)).astype(BF16)
    dec = jnp.exp(b_last)
    for h in range(N_HEADS_B):
        sk = slice(h * DK_B, (h + 1) * DK_B)
        sv = slice(h * DV_B, (h + 1) * DV_B)
        vh = v_ref[:, sv].astype(BF16)
        att = jnp.where(tril, _dot_nt(q_intra[:, sk], k_intra[:, sk]), 0.0)
        o_intra = _dot(att.astype(BF16), vh)
        st = st_ref[h]
        o_inter = []
        for c in range(GLA_SUB):
            rc = slice(c * CHUNK, (c + 1) * CHUNK)
            start = fresh | ((j == 0) & (c == 0))
            st = jnp.where(start, s0_ref[c, h].T, st)
            o_inter.append(_dot_nt(q_inter[rc, sk], st.astype(BF16)))
            st = st * dec[c * CHUNK:c * CHUNK + 1, sk] + _dot_tn(vh[rc], k_state[rc, sk])
            sout_ref[c, h] = st.T
        st_ref[h] = st
        o = jnp.concatenate(o_inter, axis=0) + o_intra
        on = _rms(o) * gbo_ref[:, sv]
        yb_ref[:, sv] = (on * jax.nn.silu(og_ref[:, sv])).astype(BF16)


def _gla(h, wg, bg, gbo, s_in, n_prompt_rows, l):
    m = h.shape[0]
    rows = GLA_SUB * CHUNK
    nps = n_prompt_rows // rows
    nslots = s_in.shape[1]

    def blk(j):
        return jnp.where(j < nps, 0, j - nps + 1)

    return pl.pallas_call(
        functools.partial(_gla_kernel, n_prompt_steps=nps),
        grid=(m // rows,),
        in_specs=[pl.BlockSpec((rows, QK_B), lambda j: (j, C_QB // QK_B)),
                  pl.BlockSpec((rows, QK_B), lambda j: (j, C_KB // QK_B)),
                  pl.BlockSpec((rows, WIDTH_B), lambda j: (j, C_VB // WIDTH_B)),
                  pl.BlockSpec((rows, WIDTH_B), lambda j: (j, C_OG // WIDTH_B)),
                  pl.BlockSpec((rows, LANES), lambda j: (j, C_MISC // LANES)),
                  pl.BlockSpec((None, GATE_RANK_B, QK_B), lambda j: (l, 0, 0)),
                  pl.BlockSpec((None, 1, QK_B), lambda j: (l, 0, 0)),
                  pl.BlockSpec((None, 1, WIDTH_B), lambda j: (l, 0, 0)),
                  pl.BlockSpec((None, GLA_SUB, N_HEADS_B, DK_B, DV_B), lambda j: (l, blk(j), 0, 0, 0))],
        out_specs=[pl.BlockSpec((rows, WIDTH_B), lambda j: (j, 0)),
                   pl.BlockSpec((GLA_SUB, N_HEADS_B, DK_B, DV_B), lambda j: (blk(j), 0, 0, 0))],
        out_shape=[jax.ShapeDtypeStruct((m, WIDTH_B), BF16),
                   jax.ShapeDtypeStruct((nslots, N_HEADS_B, DK_B, DV_B), F32)],
        scratch_shapes=[pltpu.VMEM((N_HEADS_B, DV_B, DK_B), F32)],
        compiler_params=_cparams(("arbitrary",)),
        name="gla",
    )(h, h, h, h, h, wg, bg, gbo, s_in)


def _fold_rows(x, op, each=lambda v: v):
    tile = SUBLANES * (4 // x.dtype.itemsize)
    n_tiles = x.shape[0] // tile
    parts = [each(x[j * tile:(j + 1) * tile]) for j in range(min(FOLD_WAYS, n_tiles))]
    for j in range(FOLD_WAYS, n_tiles):
        parts[j % FOLD_WAYS] = op(parts[j % FOLD_WAYS], each(x[j * tile:(j + 1) * tile]))
    while len(parts) > 1:
        nxt = [op(parts[j], parts[j + 1]) for j in range(0, len(parts) - 1, 2)]
        parts = nxt + parts[len(parts) - len(parts) % 2:]
    return parts[0]


def _loop(n, body, init):
    if isinstance(n, int):
        carry = init
        for c in range(n):
            carry = body(c, carry)
        return carry
    return lax.fori_loop(0, n, body, init)


REP_C = N_HEADS_C // N_KV_C
KV_SLICES = [slice(g * HD_C, (g + 1) * HD_C) for g in range(N_KV_C)]


def _scaled_queries(q, qi, misc_t):
    qi_s = (qi * (D_IDX ** -0.5)).astype(BF16)
    wi_t = misc_t[MISC_WIC:MISC_WIC + N_IDX_HEADS, :] * (N_IDX_HEADS ** -0.5)
    qs = (q * (HD_C ** -0.5 * LOG2E)).astype(BF16)
    return ([qi_s[:, h * D_IDX:(h + 1) * D_IDX] for h in range(N_IDX_HEADS)], wi_t,
            [qs[:, hh * HD_C:(hh + 1) * HD_C] for hh in range(N_HEADS_C)])


def _dsa_core(qi_all, wi_t, q_g, far_ki, far_k, far_vx, nf, far_adm, near_ki, near_k, near_vx, near_adm, n_adm,
              bias_near_ref, emit, scratch, *, tq, kw, nw, topk):
    fkeys_ref, nkeys_ref, m_ref, acc_ref, ss_ref, p_ref = scratch
    rep = REP_C
    groups = range(N_KV_C)
    kf = float(topk)

    def extremes(key, carry):
        kmax, kmin = carry
        kmax = jnp.maximum(kmax, _fold_rows(key, jnp.maximum))
        kmin = jnp.minimum(kmin, _fold_rows(key, jnp.minimum, lambda v: jnp.where(v == INT_MIN, INT_MAX, v)))
        return kmax, kmin

    def fill_keys(ki_blk, adm, put, carry):
        n = ki_blk.shape[0]

        def dots(p0):
            return _dot_nt(ki_blk[p0:min(p0 + MXU_PIECE, n)], qi_all)

        starts = list(range(0, n, MXU_PIECE))
        d_next = dots(starts[0])
        for i, p0 in enumerate(starts):
            d = d_next
            if i + 1 < len(starts):
                d_next = dots(starts[i + 1])
            for r0 in range(0, d.shape[0], SOFTMAX_ROWS):
                acc = None
                for h in range(N_IDX_HEADS):
                    t = wi_t[h:h + 1, :] * jnp.maximum(d[r0:r0 + SOFTMAX_ROWS, h * tq:(h + 1) * tq], 0.0)
                    acc = t if acc is None else acc + t
                bits = lax.bitcast_convert_type(acc, I32)
                key = bits ^ ((bits >> 31) & 0x7FFFFFFF)
                key = jnp.where(acc == 0.0, 0, key)
                rows = slice(p0 + r0, p0 + r0 + SOFTMAX_ROWS)
                if adm is not None:
                    key = jnp.where(adm(rows), key, INT_MIN)
                put(rows, key)
                carry = extremes(key, carry)
        return carry

    def fill(c, carry):
        def put(rows, key):
            fkeys_ref[c, rows, :] = key

        return fill_keys(far_ki(c), None if far_adm is None else far_adm(c), put, carry)

    def put_near(rows, key):
        nkeys_ref[rows, :] = key

    carry = _loop(nf, fill, (jnp.full((SUBLANES, tq), INT_MIN, I32), jnp.full((SUBLANES, tq), INT_MAX, I32)))
    kmax, kmin = fill_keys(near_ki, near_adm, put_near, carry)
    kmax = jnp.max(kmax, axis=0, keepdims=True)
    kmin = jnp.min(kmin, axis=0, keepdims=True)

    def count(thr, strict):
        def cmp(x):
            return _fold_rows(x, jnp.add, lambda v: jnp.where((v > thr) if strict else (v >= thr), 1.0, 0.0))

        part = _loop(nf, lambda c, a: a + cmp(fkeys_ref[c]), jnp.zeros((SUBLANES, tq), F32))
        part = part + cmp(nkeys_ref[...])
        return jnp.sum(part, axis=0, keepdims=True)

    def bits_flip(x):
        return x ^ ((x >> 31) & 0x7FFFFFFF)

    def any_lane(flag):
        return jnp.max(jnp.where(flag, 1.0, 0.0))

    def snap(lo, hi):
        def pull(x, carry):
            kin, kax = carry
            kin = jnp.minimum(kin, _fold_rows(x, jnp.minimum, lambda v: jnp.where(v >= lo, v, INT_MAX)))
            kax = jnp.maximum(kax, _fold_rows(x, jnp.maximum, lambda v: jnp.where(v < hi, v, INT_MIN)))
            return kin, kax

        init = (jnp.full((SUBLANES, tq), INT_MAX, I32), jnp.full((SUBLANES, tq), INT_MIN, I32))
        kin, kax = pull(nkeys_ref[...], _loop(nf, lambda c, carry: pull(fkeys_ref[c], carry), init))
        return jnp.min(kin, axis=0, keepdims=True), jnp.max(kax, axis=0, keepdims=True)

    enough = n_adm >= kf
    log_k = math.log2(topk)

    def open_lanes(lo, hi, flo):
        return enough & (flo > kf) & (hi > lo + 1)

    def log_excess(c):
        return jnp.log2(jnp.maximum(c, 0.5)) - log_k

    def step(state):
        it, _, lo, hi, flo, fhi, glo, ghi, side = state
        active = open_lanes(lo, hi, flo)

        def pull_in(_):
            kin, kax = snap(lo, hi)
            return jnp.where(active, kin, lo), jnp.where(active, kax + 1, hi), flo, fhi, glo, ghi, side

        def probe(_):
            v_lo = lax.bitcast_convert_type(bits_flip(lo), F32)
            v_hi = lax.bitcast_convert_type(bits_flip(hi), F32)
            frac = jnp.minimum(jnp.maximum(glo / (glo - ghi), 0.02), 0.98)
            t_int = bits_flip(lax.bitcast_convert_type(v_lo + (v_hi - v_lo) * frac, I32))
            t_mid = (lo & hi) + ((lo ^ hi) >> 1)
            t = jnp.where(it >= BISECT_AFTER, t_mid, t_int)
            t = jnp.minimum(jnp.maximum(t, lo + 1), hi - 1)
            c = count(t, False)
            g = log_excess(c)
            up = active & (c >= kf)
            dn = active & (c < kf)
            ghi2 = jnp.where(up & (side > 0.0), ghi * 0.5, ghi)
            glo2 = jnp.where(dn & (side < 0.0), glo * 0.5, glo)
            return (jnp.where(up, t, lo), jnp.where(dn, t, hi), jnp.where(up, c, flo), jnp.where(dn, c, fhi),
                    jnp.where(up, g, glo2), jnp.where(dn, g, ghi2), jnp.where(up, 1.0, jnp.where(dn, -1.0, side)))

        is_snap = (it % SNAP_EVERY == SNAP_EVERY - 1) & (it < BISECT_AFTER)
        lo, hi, flo, fhi, glo, ghi, side = lax.cond(is_snap, pull_in, probe, 0)
        return it + 1, any_lane(open_lanes(lo, hi, flo)), lo, hi, flo, fhi, glo, ghi, side

    hi0 = kmax + 1
    zero = jnp.zeros((1, tq), F32)
    state = (jnp.int32(0), any_lane(open_lanes(kmin, hi0, n_adm)), kmin, hi0, n_adm, zero,
             log_excess(n_adm), log_excess(zero), zero)
    state = lax.while_loop(lambda st: (st[1] > 0.0) & (st[0] < MAX_SEARCH_STEPS), step, state)
    thr = jnp.where(enough, state[2], INT_MIN)
    has_tie = any_lane(enough & (state[4] > kf)) > 0.0

    @pl.when(has_tie)
    def _():
        need = kf - count(thr, True)

        def lower(n):
            a = lax.broadcasted_iota(I32, (n, n), 0)
            b = lax.broadcasted_iota(I32, (n, n), 1)
            return jnp.where(b < a, 1.0, 0.0).astype(BF16)

        def demote(keys, run, lt):
            eq = (keys == thr) & (keys > INT_MIN)
            eqf = jnp.where(eq, 1.0, 0.0)
            before = _dot(lt, eqf.astype(BF16)) + run
            keys = jnp.where(eq & (before >= need), INT_MIN, keys)
            return keys, run + jnp.sum(eqf, axis=0, keepdims=True)

        lt_far = lower(kw)

        def step(c, run):
            keys, run = demote(fkeys_ref[c], run, lt_far)
            fkeys_ref[c] = keys
            return run

        run = _loop(nf, step, jnp.zeros((1, tq), F32))
        keys, _ = demote(nkeys_ref[...], run, lower(nw))
        nkeys_ref[...] = keys

    thr_sel = jnp.maximum(thr, INT_MIN + 1)
    m_ref[...] = jnp.full(m_ref.shape, MASKED, F32)
    acc_ref[...] = jnp.zeros(acc_ref.shape, F32)

    def logits(k_blk):
        return [_dot_nt(k_blk[g], q_g[g]) for g in groups]

    def attend(slot, keys_rows, vx_blk, bias_ref, n):
        tiles = [slice(r0, r0 + SOFTMAX_ROWS) for r0 in range(0, n, SOFTMAX_ROWS)]
        heads = [slice(r * tq, (r + 1) * tq) for r in range(rep)]

        run = [None] * N_KV_C
        for rows in tiles:
            selb = jnp.where(keys_rows(rows) >= thr_sel, 0.0, MASKED)
            for g in groups:
                s = ss_ref[slot, g, rows, :]
                if bias_ref is not None:
                    s = s + bias_ref[g, rows, :]
                tops = []
                for hd in heads:
                    sm = s[:, hd] + selb
                    ss_ref[slot, g, rows, hd] = sm
                    tops.append(_fold_rows(sm, jnp.maximum))
                top = jnp.concatenate(tops, axis=1)
                run[g] = top if run[g] is None else jnp.maximum(run[g], top)
        m_new = []
        for g in groups:
            m_old = m_ref[g]
            m_new.append(jnp.maximum(m_old, jnp.max(run[g], axis=0, keepdims=True)))
            acc_ref[g] = jnp.exp2(m_old - m_new[g]) * acc_ref[g]
            m_ref[g] = m_new[g]
        for rows in tiles:
            for g in groups:
                p_ref[g, rows, :] = jnp.exp2(ss_ref[slot, g, rows, :] - m_new[g]).astype(BF16)
        for g in groups:
            acc_ref[g] += _dot(vx_blk[g], p_ref[g, :n, :])

    last = fkeys_ref.shape[0] - 1

    def put_logits(slot, k_blk, n):
        for g, s in enumerate(logits(k_blk)):
            ss_ref[slot, g, :n, :] = s

    def far_body(c, cur):
        put_logits(1 - cur, far_k(min(c + 1, last) if isinstance(c, int) else jnp.minimum(c + 1, last)), kw)
        attend(cur, lambda rows: fkeys_ref[c, rows, :], far_vx(c), None, kw)

    def far_step(c, carry):
        if isinstance(c, int):
            far_body(c, c % 2)
        else:
            for cur in range(2):
                pl.when(c % 2 == cur)(functools.partial(far_body, c, cur))
        return carry

    put_logits(0, far_k(0), kw)
    _loop(nf, far_step, 0)
    put_logits(0, near_k, nw)
    attend(0, lambda rows: nkeys_ref[rows, :], near_vx, bias_near_ref, nw)

    for g in groups:
        emit(g, acc_ref[g])


def _with_ones(vt):
    return jnp.concatenate([vt, jnp.ones((ONES_ROWS, vt.shape[1]), BF16)], axis=0)


def _dsa_scratch(nf, tq, kw, nw, vx_rows):
    rep = REP_C
    return [pltpu.VMEM((nf, kw, tq), I32),
            pltpu.VMEM((nw, tq), I32),
            pltpu.VMEM((N_KV_C, 1, rep * tq), F32),
            pltpu.VMEM((N_KV_C, vx_rows, rep * tq), F32),
            pltpu.VMEM((2, N_KV_C, max(kw, nw), rep * tq), F32),
            pltpu.VMEM((N_KV_C, max(kw, nw), rep * tq), BF16)]


def _dsa_prompt_kernel(q_ref, qi_ref, misc_ref, k_ref, vt_far_ref, vt_near_ref, ki_ref, bias_near_ref,
                       out_ref, *scratch, tq, topk):
    i = pl.program_id(0)
    kw = FAR_KW
    nw = NEAR_SPAN + tq
    q0 = i * tq
    far_limit = q0 - NEAR_SPAN
    near0 = jnp.maximum(far_limit, 0)
    nf = (near0 + kw - 1) // kw

    def rows(c):
        return pl.ds(pl.multiple_of(c * kw, kw), kw)

    def far_adm(c):
        def rows_ok(rows):
            n = rows.stop - rows.start
            return (c * kw + rows.start + lax.broadcasted_iota(I32, (n, tq), 0)) < far_limit

        return rows_ok

    left = pl.ds(pl.multiple_of(near0, NEAR_SPAN), NEAR_SPAN)
    right = pl.ds(pl.multiple_of(q0, tq), tq)
    near_ki = jnp.concatenate([ki_ref[left, :], ki_ref[right, :]], axis=0)
    near_k = jnp.concatenate([k_ref[left, :], k_ref[right, :]], axis=0)
    per = tq // NEAR_SPAN
    near_vt = jnp.concatenate([vt_near_ref[near0 // NEAR_SPAN]] + [vt_near_ref[i * per + j] for j in range(per)],
                              axis=1)
    first_key = jnp.where(i > 0, 0, NEAR_SPAN)

    def near_adm(rows):
        n = rows.stop - rows.start
        key = rows.start + lax.broadcasted_iota(I32, (n, tq), 0)
        qry = lax.broadcasted_iota(I32, (n, tq), 1)
        return (key >= first_key) & (((key - NEAR_SPAN) >> 6) <= (qry >> 6))

    n_adm = ((((q0 + lax.broadcasted_iota(I32, (1, tq), 1)) >> 6) + 1) * CHUNK).astype(F32)

    qi_h, wi_t, q_h = _scaled_queries(q_ref[...], qi_ref[...], misc_ref[...].T)
    qi_all = jnp.concatenate(qi_h, axis=0)
    q_g = [jnp.concatenate(q_h[g * REP_C:(g + 1) * REP_C], axis=0) for g in range(N_KV_C)]

    def emit(g, acc):
        o = acc[:HD_C] / acc[HD_C:HD_C + 1]
        for r in range(REP_C):
            hh = g * REP_C + r
            out_ref[:, hh * HD_C:(hh + 1) * HD_C] = o[:, r * tq:(r + 1) * tq].T.astype(BF16)

    _dsa_core(qi_all, wi_t, q_g,
              lambda c: ki_ref[rows(c), :], lambda c: [k_ref[rows(c), sl] for sl in KV_SLICES],
              lambda c: [_with_ones(vt_far_ref[c, sl, :]) for sl in KV_SLICES],
              nf, far_adm, near_ki, [near_k[:, sl] for sl in KV_SLICES],
              [_with_ones(near_vt[sl, :]) for sl in KV_SLICES], near_adm, n_adm,
              bias_near_ref, emit, scratch,
              tq=tq, kw=kw, nw=nw, topk=topk)


def _dsa_prompt(h, kb, vt_far, vt_near, kib, bias_near, t):
    tq = PROMPT_TQ
    topk = min(TOPK_MAX, t // 4)
    return pl.pallas_call(
        functools.partial(_dsa_prompt_kernel, tq=tq, topk=topk),
        grid=(t // tq,),
        in_specs=[pl.BlockSpec((tq, WIDTH_C), lambda i: (i, C_QC // WIDTH_C)),
                  pl.BlockSpec((tq, QI_C), lambda i: (i, C_QIC // QI_C)),
                  pl.BlockSpec((tq, LANES), lambda i: (i, C_MISC // LANES)),
                  pl.BlockSpec((t, KV_C), lambda i: (0, 0)),
                  pl.BlockSpec(vt_far.shape, lambda i: (0, 0, 0)),
                  pl.BlockSpec(vt_near.shape, lambda i: (0, 0, 0)),
                  pl.BlockSpec((t, D_IDX), lambda i: (0, 0)),
                  pl.BlockSpec(bias_near.shape, lambda i: (0, 0, 0))],
        out_specs=pl.BlockSpec((tq, WIDTH_C), lambda i: (i, 0)),
        out_shape=jax.ShapeDtypeStruct((t, WIDTH_C), BF16),
        scratch_shapes=_dsa_scratch(t // FAR_KW, tq, FAR_KW, NEAR_SPAN + tq, HD_C + ONES_ROWS),
        compiler_params=_cparams(("arbitrary",)),
        name="dsa_prompt",
    )(h, h, h, kb, vt_far, vt_near, kib, bias_near)


def _dsa_sample_kernel(q_ref, qi_ref, misc_ref, kvn_ref, pk_ref, pv_ref, pki_ref, bias_near_ref,
                       out_ref, *scratch, kw, nf, topk):
    tq = CHUNK
    far_len = kw * nf
    misc = misc_ref[...]
    kvn = kvn_ref[...]
    qi_h, wi_t, q_h = _scaled_queries(q_ref[...], qi_ref[...], misc.T)
    qi_all = jnp.concatenate(qi_h, axis=0)
    q_g = [jnp.concatenate(q_h[g * REP_C:(g + 1) * REP_C], axis=0) for g in range(N_KV_C)]

    def group_rows(ref, start, n, g):
        return ref[0, pl.ds(2 * start + g, n, stride=2), :]

    def far_ki(c):
        return pki_ref[0, :, pl.ds(c * kw, kw)].T.astype(BF16)

    def far_k(c):
        return [group_rows(pk_ref, c * kw, kw, g).astype(BF16) for g in range(N_KV_C)]

    def far_vx(c):
        return [_with_ones(group_rows(pv_ref, c * kw, kw, g).T.astype(BF16)) for g in range(N_KV_C)]

    near_ki = jnp.concatenate([pki_ref[0, :, pl.ds(far_len, NEAR_SPAN)].T, misc[:, :D_IDX]], axis=0).astype(BF16)
    near_k = [jnp.concatenate([group_rows(pk_ref, far_len, NEAR_SPAN, g), kvn[:, KV_SLICES[g]]],
                              axis=0).astype(BF16) for g in range(N_KV_C)]
    near_vx = [_with_ones(jnp.concatenate(
        [group_rows(pv_ref, far_len, NEAR_SPAN, g).T, kvn[:, KV_C + g * HD_C:KV_C + (g + 1) * HD_C].T],
        axis=1).astype(BF16)) for g in range(N_KV_C)]

    def emit(g, acc):
        o = acc[:HD_C] / acc[HD_C:HD_C + 1]
        for r in range(REP_C):
            hh = g * REP_C + r
            out_ref[:, hh * HD_C:(hh + 1) * HD_C] = o[:, r * tq:(r + 1) * tq].T.astype(BF16)

    _dsa_core(qi_all, wi_t, q_g, far_ki, far_k, far_vx, nf, None, near_ki, near_k, near_vx, None,
              jnp.full((1, tq), float(far_len + NEAR_SPAN + CHUNK), F32),
              bias_near_ref, emit, scratch,
              tq=tq, kw=kw, nw=NEAR_SPAN + CHUNK, topk=topk)


def _dsa_sample(h, pk, pv, pki, bias_near, t, l):
    nb, p = pki.shape[1], pki.shape[3]
    tq = CHUNK
    far_len = p - NEAR_SPAN
    kw = _pick(far_len, (640, 512, 384, 256, 128))
    nf = far_len // kw
    nw = NEAR_SPAN + CHUNK
    topk = min(TOPK_MAX, (p + CHUNK) // 4)
    row0 = t // tq
    return pl.pallas_call(
        functools.partial(_dsa_sample_kernel, kw=kw, nf=nf, topk=topk),
        grid=(nb,),
        in_specs=[pl.BlockSpec((tq, WIDTH_C), lambda b: (row0 + b, C_QC // WIDTH_C)),
                  pl.BlockSpec((tq, QI_C), lambda b: (row0 + b, C_QIC // QI_C)),
                  pl.BlockSpec((tq, LANES), lambda b: (row0 + b, C_MISC // LANES)),
                  pl.BlockSpec((tq, 2 * KV_C), lambda b: (row0 + b, C_KC // (2 * KV_C))),
                  pl.BlockSpec((None, 1, N_KV_C * p, HD_C), lambda b: (l, b, 0, 0)),
                  pl.BlockSpec((None, 1, N_KV_C * p, HD_C), lambda b: (l, b, 0, 0)),
                  pl.BlockSpec((None, 1, D_IDX, p), lambda b: (l, b, 0, 0)),
                  pl.BlockSpec(bias_near.shape, lambda b: (0, 0, 0))],
        out_specs=pl.BlockSpec((tq, WIDTH_C), lambda b: (b, 0)),
        out_shape=jax.ShapeDtypeStruct((nb * tq, WIDTH_C), BF16),
        scratch_shapes=_dsa_scratch(nf, tq, kw, nw, HD_C + ONES_ROWS),
        compiler_params=_cparams(("arbitrary",)),
        name="dsa_sample",
    )(h, h, h, h, pk, pv, pki, bias_near)


def _outproj_kernel(x_ref, ya_ref, yb_ref, ycp_ref, ycs_ref, wa_ref, wb_ref, wc_ref, o_ref, *, n_prompt_blocks):
    yc = jnp.where(pl.program_id(0) < n_prompt_blocks, ycp_ref[...], ycs_ref[...])
    o_ref[...] = (x_ref[...] + _dot(ya_ref[...], wa_ref[...]) + _dot(yb_ref[...], wb_ref[...])
                  + _dot(yc, wc_ref[...]))


def _outproj(x, ya, yb, yc_p, yc_s, wa, wb, wc, l):
    m = x.shape[0]
    tm = _pick(yc_s.shape[0], (1024, 512, 256, 128))
    tn = 1024
    npb = yc_p.shape[0] // tm
    assert yc_p.shape[0] % tm == 0
    return pl.pallas_call(
        functools.partial(_outproj_kernel, n_prompt_blocks=npb),
        grid=(m // tm, D_MODEL // tn),
        in_specs=[pl.BlockSpec((tm, tn), lambda i, j: (i, j)),
                  pl.BlockSpec((tm, WIDTH_A), lambda i, j: (i, 0)),
                  pl.BlockSpec((tm, WIDTH_B), lambda i, j: (i, 0)),
                  pl.BlockSpec((tm, WIDTH_C), lambda i, j: (jnp.minimum(i, npb - 1), 0)),
                  pl.BlockSpec((tm, WIDTH_C), lambda i, j: (jnp.maximum(i - npb, 0), 0)),
                  pl.BlockSpec((None, WIDTH_A, tn), lambda i, j: (l, 0, j)),
                  pl.BlockSpec((None, WIDTH_B, tn), lambda i, j: (l, 0, j)),
                  pl.BlockSpec((None, WIDTH_C, tn), lambda i, j: (l, 0, j))],
        out_specs=pl.BlockSpec((tm, tn), lambda i, j: (i, j)),
        out_shape=jax.ShapeDtypeStruct((m, D_MODEL), F32),
        compiler_params=_cparams(("arbitrary", "arbitrary")),
        name="outproj",
    )(x, ya, yb, yc_p, yc_s, wa, wb, wc)


def _ffn_kernel(x_ref, g_ref, wu_ref, wd_ref, o_ref, xn_ref, acc_ref):
    f = pl.program_id(1)

    @pl.when(f == 0)
    def _():
        xn_ref[...] = (_rms(x_ref[...]) * g_ref[...]).astype(BF16)
        acc_ref[...] = jnp.zeros(acc_ref.shape, F32)

    a = jnp.maximum(_dot(xn_ref[...], wu_ref[...]), 0.0)
    acc_ref[...] += _dot((a * a).astype(BF16), wd_ref[...])

    @pl.when(f == pl.num_programs(1) - 1)
    def _():
        o_ref[...] = x_ref[...] + acc_ref[...]


def _ffn(x, g, wu, wd, l):
    m = x.shape[0]
    tm = _pick(m, (512, 256, 128))
    tf = 1024
    return pl.pallas_call(
        _ffn_kernel,
        grid=(m // tm, D_FF // tf),
        in_specs=[pl.BlockSpec((tm, D_MODEL), lambda i, f: (i, 0)),
                  pl.BlockSpec((None, 1, D_MODEL), lambda i, f: (l, 0, 0)),
                  pl.BlockSpec((None, D_MODEL, tf), lambda i, f: (l, 0, f)),
                  pl.BlockSpec((None, tf, D_MODEL), lambda i, f: (l, f, 0))],
        out_specs=pl.BlockSpec((tm, D_MODEL), lambda i, f: (i, 0)),
        out_shape=jax.ShapeDtypeStruct((m, D_MODEL), F32),
        scratch_shapes=[pltpu.VMEM((tm, D_MODEL), BF16), pltpu.VMEM((tm, D_MODEL), F32)],
        compiler_params=_cparams(("arbitrary", "arbitrary")),
        name="ffn",
    )(x, g, wu, wd)


def _norm_kernel(x_ref, g_ref, o_ref):
    o_ref[...] = _rms(x_ref[...]) * g_ref[...]


def _final_norm(x, g, row0, rows):
    tm = _pick(rows, (1024, 512, 256, 128))
    assert row0 % tm == 0
    return pl.pallas_call(
        _norm_kernel,
        grid=(rows // tm,),
        in_specs=[pl.BlockSpec((tm, D_MODEL), lambda i: (row0 // tm + i, 0)),
                  pl.BlockSpec((1, D_MODEL), lambda i: (0, 0))],
        out_specs=pl.BlockSpec((tm, D_MODEL), lambda i: (i, 0)),
        out_shape=jax.ShapeDtypeStruct((rows, D_MODEL), F32),
        compiler_params=_cparams(("arbitrary",)),
        name="final_norm",
    )(x, g)


def _t5_bucket(rel):
    half = N_BUCKETS // 2
    max_exact = half // 2
    n = jnp.abs(rel)
    nf = jnp.maximum(n, 1).astype(F32)
    large = max_exact + (jnp.log(nf / max_exact) / jnp.log(MAX_DISTANCE / max_exact)
                         * (half - max_exact)).astype(I32)
    large = jnp.minimum(large, half - 1)
    return jnp.where(rel > 0, half, 0) + jnp.where(n < max_exact, n, large)


def _bias_table(rel_bias, tq, nw):
    j = jnp.arange(nw, dtype=I32)[:, None]
    t = jnp.arange(tq, dtype=I32)[None, :]
    hot = jax.nn.one_hot(_t5_bucket(j - NEAR_SPAN - t), N_BUCKETS, dtype=F32)
    near = jnp.einsum("jtb,bh->jth", hot, rel_bias, precision=lax.Precision.HIGHEST)
    far = rel_bias[_t5_bucket(jnp.int32(-(NEAR_SPAN + 1)))]
    near = ((near - far) * LOG2E).astype(F32).reshape(nw, tq, N_KV_C, REP_C)
    return jnp.transpose(near, (2, 0, 3, 1)).reshape(N_KV_C, nw, REP_C * tq)


def _permute_w_in(w):
    wt = jnp.swapaxes(w, 1, 2)
    sizes = (WIDTH_A, WIDTH_A, QK_B, QK_B, WIDTH_B, GATE_RANK_B, WIDTH_B, WIDTH_C, KV_C, KV_C, QI_C,
             D_IDX, N_IDX_HEADS)
    parts, off = [], 0
    for n in sizes:
        parts.append(wt[:, off:off + n])
        off += n
    ua, va, qb, kb, vb, lrb, og, qc, kc, vc, qic, kic, wic = parts
    pad = jnp.zeros((w.shape[0], H_WIDTH - (C_MISC + D_IDX + GATE_RANK_B + N_IDX_HEADS), w.shape[1]), w.dtype)
    return jnp.concatenate([qb, kb, vb, og, qc, kc, vc, qic, ua, va, kic, lrb, wic, pad], axis=1).astype(BF16)


def kernel(x_prompt, x_sample, cache_c_k, cache_c_v, cache_c_kidx, state_b_s, norm_mix, w_in, norm_a_v,
           w_s_a, b_s_a, w_gate_b, b_gate_b, norm_b_o, rel_bias, w_o, norm_ffn, w_up, w_down, norm_final):
    depth = w_in.shape[0]
    nbp, t, _ = x_prompt.shape
    nbs, ts, _ = x_sample.shape
    p = cache_c_k.shape[2]
    assert nbp == 1 and ts == CHUNK and t % FAR_KW == 0 and t % GMLP_ROWS == 0 and (nbs * ts) % GMLP_ROWS == 0
    assert p % NEAR_SPAN == 0 and p >= 2 * NEAR_SPAN
    assert t % PROMPT_TQ == 0 and t % (GLA_SUB * CHUNK) == 0 and nbs % GLA_SUB == 0

    x = jnp.concatenate([x_prompt[0], x_sample.reshape(nbs * ts, D_MODEL)], axis=0)
    w_in_p = _permute_w_in(w_in)
    wo_a = w_o[:, :WIDTH_A].astype(BF16)
    wo_b = w_o[:, WIDTH_A:WIDTH_A + WIDTH_B].astype(BF16)
    wo_c = w_o[:, WIDTH_A + WIDTH_B:].astype(BF16)
    w_up_b = w_up.astype(BF16)
    w_down_b = w_down.astype(BF16)
    w_gate = w_gate_b.astype(BF16)
    np_, ns_ = GMLP_ROWS // GROUP_A, GMLP_ROWS // CHUNK
    w_s2 = jnp.stack([jnp.tile(w_s_a, (1, 1, np_, np_)),
                      jnp.tile(w_s_a[:, :, :CHUNK, :CHUNK], (1, 1, ns_, ns_))], axis=1)
    b_s2 = jnp.stack([jnp.tile(b_s_a, (1, 1, np_)), jnp.tile(b_s_a[:, :, :CHUNK], (1, 1, ns_))], axis=1)
    b_s2 = jnp.swapaxes(b_s2, 2, 3)
    s_all = jnp.concatenate([jnp.zeros((depth, GLA_SUB) + state_b_s.shape[2:], F32), state_b_s], axis=1)
    bias_near = _bias_table(rel_bias, PROMPT_TQ, NEAR_SPAN + PROMPT_TQ)
    bias_near_s = _bias_table(rel_bias, CHUNK, NEAR_SPAN + CHUNK)
    pk_all = cache_c_k.reshape(depth, nbs, N_KV_C * p, HD_C)
    pv_all = cache_c_v.reshape(depth, nbs, N_KV_C * p, HD_C)
    pki_all = jnp.swapaxes(cache_c_kidx, 2, 3)

    row = lambda a: a[:, None, :]
    outs = [[] for _ in range(9)]
    for l in range(depth):
        h = _inproj(x, row(norm_mix), w_in_p, l)
        hp, hs = h[:t], h[t:]
        ya, vn = _gmlp(h, row(norm_a_v), w_s2, b_s2, t, l)
        yb, s_out = _gla(h, w_gate, row(b_gate_b), row(norm_b_o), s_all, t, l)
        kb = hp[:, C_KC:C_KC + KV_C].astype(BF16)
        vt = hp[:, C_VC:C_VC + KV_C].astype(BF16).T
        vt_far = jnp.swapaxes(vt.reshape(KV_C, t // FAR_KW, FAR_KW), 0, 1)
        vt_near = jnp.swapaxes(vt.reshape(KV_C, t // NEAR_SPAN, NEAR_SPAN), 0, 1)
        kib = hp[:, C_MISC:C_MISC + D_IDX].astype(BF16)
        yc_p = _dsa_prompt(h, kb, vt_far, vt_near, kib, bias_near, t)
        yc_s = _dsa_sample(h, pk_all, pv_all, pki_all, bias_near_s, t, l)
        x = _outproj(x, ya, yb, yc_p, yc_s, wo_a, wo_b, wo_c, l)
        x = _ffn(x, row(norm_ffn), w_up_b, w_down_b, l)

        outs[0].append(hp[:, C_KC:C_KC + KV_C].reshape(1, t, N_KV_C, HD_C))
        outs[1].append(hp[:, C_VC:C_VC + KV_C].reshape(1, t, N_KV_C, HD_C))
        outs[2].append(hp[:, C_MISC:C_MISC + D_IDX].reshape(1, t, D_IDX))
        outs[3].append(s_out[GLA_SUB - 1:GLA_SUB])
        outs[4].append(hs[:, C_KC:C_KC + KV_C].reshape(nbs, ts, N_KV_C, HD_C))
        outs[5].append(hs[:, C_VC:C_VC + KV_C].reshape(nbs, ts, N_KV_C, HD_C))
        outs[6].append(hs[:, C_MISC:C_MISC + D_IDX].reshape(nbs, ts, D_IDX))
        outs[7].append(s_out[GLA_SUB:])
        outs[8].append(vn.reshape(nbs, ts, WIDTH_A))

    y_prompt = _final_norm(x, norm_final[None], 0, t)
    y_sample = _final_norm(x, norm_final[None], t, nbs * ts)
    return (y_prompt[None], y_sample.reshape(nbs, ts, D_MODEL)) + tuple(jnp.stack(o) for o in outs)
```
